```python
import math
import jax, jax.numpy as jnp
from jax import lax
import numpy as np

D_MODEL = 4096
BATCH = 2
SEQ = 4096
DEPTH = 2

CHUNK = 64
Q_BLOCK = 128
NORM_EPS = 1e-6

MLA_HEADS = 16
MLA_Q_RANK = 1024
MLA_KV_RANK = 512
MLA_NOPE = 128
MLA_ROPE = 64
MLA_V = 128
ROPE_BASE = 10000.0

S5_WIDTH = 2048
S5_GROUP = 16
S5_GROUPS = S5_WIDTH // S5_GROUP
S5_STATE = 64
S5_DT_MIN = 1e-3
S5_DT_MAX = 1e-1

GDN_HEADS = 16
GDN_DK = 128
GDN_DV = 128
GDN_CONV = 4

SB_HEADS = 16
SB_D = 128

FFN_HIDDEN = 11008
FFN_CONV = 3

EVEN_IN = MLA_Q_RANK + MLA_KV_RANK + MLA_ROPE + S5_WIDTH
EVEN_OUT = MLA_HEADS * MLA_V + S5_WIDTH
GDN_QKV = GDN_HEADS * (2 * GDN_DK + GDN_DV)
ODD_IN = GDN_QKV + 2 * GDN_HEADS + GDN_HEADS * GDN_DV + 3 * SB_HEADS * SB_D
ODD_OUT = GDN_HEADS * GDN_DV + SB_HEADS * SB_D

kernel_name = "hybrid_mla_s5_gdn_stickbreak_trunk"


def rmsnorm(x, g):
    xf = x.astype(jnp.float32)
    y = xf * lax.rsqrt(jnp.mean(xf * xf, axis=-1, keepdims=True) + NORM_EPS)
    return (y * g.astype(jnp.float32)).astype(x.dtype)


def l2norm(x):
    return x * lax.rsqrt(jnp.sum(x * x, axis=-1, keepdims=True) + NORM_EPS)


def causal_dwconv(x, w):
    k, c = w.shape
    return lax.conv_general_dilated(
        x, w[:, None, :].astype(x.dtype), window_strides=(1,), padding=[(k - 1, 0)],
        dimension_numbers=('NWC', 'WIO', 'NWC'), feature_group_count=c)


def apply_rope(x, pos):
    half = x.shape[-1] // 2
    inv_freq = ROPE_BASE ** (-jnp.arange(half, dtype=jnp.float32) / half)
    ang = pos.astype(jnp.float32)[:, None] * inv_freq[None, :]
    shape = (1, x.shape[1]) + (1,) * (x.ndim - 3) + (half,)
    cos = jnp.cos(ang).reshape(shape)
    sin = jnp.sin(ang).reshape(shape)
    xf = x.astype(jnp.float32)
    x1, x2 = xf[..., :half], xf[..., half:]
    return jnp.concatenate([x1 * cos - x2 * sin, x1 * sin + x2 * cos], axis=-1).astype(x.dtype)


def sweep_query_blocks(fn, *qs):
    b, seq = qs[0].shape[0], qs[0].shape[1]
    nb = seq // Q_BLOCK
    blocked = tuple(jnp.moveaxis(q.reshape((b, nb, Q_BLOCK) + q.shape[2:]), 1, 0) for q in qs)
    out = lax.map(lambda args: fn(*args), (jnp.arange(nb),) + blocked)
    out = jnp.moveaxis(out, 0, 1)
    return out.reshape((b, seq) + out.shape[3:])


def mla_mixer(c_q, c_kv, k_rope, q_norm, kv_norm, w_uq, w_ukv):
    b, seq, _ = c_q.shape
    pos = jnp.arange(seq)
    q = (rmsnorm(c_q, q_norm) @ w_uq).reshape(b, seq, MLA_HEADS, MLA_NOPE + MLA_ROPE)
    kv = (rmsnorm(c_kv, kv_norm) @ w_ukv).reshape(b, seq, MLA_HEADS, MLA_NOPE + MLA_V)
    q_nope, q_rope = q[..., :MLA_NOPE], apply_rope(q[..., MLA_NOPE:], pos)
    k_nope, v = kv[..., :MLA_NOPE], kv[..., MLA_NOPE:]
    k_rope = apply_rope(k_rope, pos)
    scale = (MLA_NOPE + MLA_ROPE) ** -0.5
    key_chunk = jnp.arange(seq) // CHUNK

    def block(blk, qn, qr):
        s = (jnp.einsum('bqhd,bkhd->bhqk', qn, k_nope)
             + jnp.einsum('bqhr,bkr->bhqk', qr, k_rope)).astype(jnp.float32) * scale
        q_chunk = (blk * Q_BLOCK + jnp.arange(Q_BLOCK)) // CHUNK
        allowed = key_chunk[None, :] <= q_chunk[:, None]
        p = jax.nn.softmax(jnp.where(allowed, s, -jnp.inf), axis=-1).astype(v.dtype)
        return jnp.einsum('bhqk,bkhd->bqhd', p, v)

    o = sweep_query_blocks(block, q_nope, q_rope)
    return o.reshape(b, seq, MLA_HEADS * MLA_V)


def s5_mixer(u, a_re, a_im, log_step, b_re, b_im, c_re, c_im, d_skip, w_glu):
    b, seq, _ = u.shape
    f32 = jnp.float32
    n_chunks = seq // CHUNK
    lam = lax.complex(a_re.astype(f32), a_im.astype(f32))
    step = jnp.exp(log_step.astype(f32))[:, None]
    lam_dt = lam * step
    lam_bar = jnp.exp(lam_dt)
    b_cplx = lax.complex(b_re.astype(f32), b_im.astype(f32))
    b_bar = ((lam_bar - 1.0) / lam)[..., None] * b_cplx
    c_cplx = lax.complex(c_re.astype(f32), c_im.astype(f32))
    powers = jnp.exp(lam_dt[None] * jnp.arange(1, CHUNK + 1, dtype=f32)[:, None, None])

    uf = u.astype(f32)
    bu = jnp.einsum('bncgi,gpi->bncgp',
                    uf.reshape(b, n_chunks, CHUNK, S5_GROUPS, S5_GROUP).astype(jnp.complex64), b_bar)

    def combine(e1, e2):
        a1, x1 = e1
        a2, x2 = e2
        return a2 * a1, a2 * x1 + x2

    def chunk_step(h, bu_c):
        a = jnp.broadcast_to(lam_bar, bu_c.shape)
        _, local = lax.associative_scan(combine, (a, bu_c), axis=1)
        states = local + powers[None] * h[:, None]
        y = jnp.einsum('bcgp,gip->bcgi', states, c_cplx).real
        return states[:, -1], y

    h0 = jnp.zeros((b, S5_GROUPS, S5_STATE), jnp.complex64)
    _, ys = lax.scan(chunk_step, h0, jnp.moveaxis(bu, 1, 0))
    y = jnp.moveaxis(ys, 0, 1).reshape(b, seq, S5_WIDTH) + d_skip.astype(f32) * uf
    g = jax.nn.gelu(y)
    out = g * jax.nn.sigmoid(g @ w_glu.astype(f32))
    return out.astype(u.dtype)


def chunk_gated_delta_rule(q, k, v, g, beta):
    b, seq, h, dk = q.shape
    dv = v.shape[-1]
    n = seq // CHUNK
    f32 = jnp.float32

    def to_chunks(t):
        return jnp.swapaxes(t.astype(f32).reshape((b, n, CHUNK, h) + t.shape[3:]), 2, 3)

    q, k, v, g, beta = map(to_chunks, (q, k, v, g, beta))
    g_cum = jnp.cumsum(g, axis=-1)
    idx = jnp.arange(CHUNK)
    causal = idx[:, None] >= idx[None, :]
    strict = idx[:, None] > idx[None, :]
    decay = jnp.exp(jnp.where(causal, g_cum[..., :, None] - g_cum[..., None, :], -jnp.inf))
    k_beta = k * beta[..., None]
    v_beta = v * beta[..., None]
    lmat = jnp.where(strict, jnp.einsum('bnhid,bnhjd->bnhij', k_beta, k) * decay, 0.0)
    eye = jnp.eye(CHUNK, dtype=f32)
    rhs = jnp.concatenate([v_beta, k_beta * jnp.exp(g_cum)[..., None]], axis=-1)
    sol = lax.linalg.triangular_solve(eye + lmat, rhs, left_side=True, lower=True,
                                      unit_diagonal=True)
    u_c, w_c = sol[..., :dv], sol[..., dv:]
    intra = jnp.einsum('bnhid,bnhjd->bnhij', q, k) * decay
    q_dec = q * jnp.exp(g_cum)[..., None]
    g_last = g_cum[..., -1]
    k_dec = k * jnp.exp(g_last[..., None] - g_cum)[..., None]

    def step(state, xs):
        uu, ww, qq, kk, att, gl = xs
        v_new = uu - jnp.einsum('bhcd,bhde->bhce', ww, state)
        o = jnp.einsum('bhcd,bhde->bhce', qq, state) + jnp.einsum('bhij,bhje->bhie', att, v_new)
        state = state * jnp.exp(gl)[..., None, None] + jnp.einsum('bhcd,bhce->bhde', kk, v_new)
        return state, o

    xs = tuple(jnp.moveaxis(t, 1, 0) for t in (u_c, w_c, q_dec, k_dec, intra, g_last))
    _, o = lax.scan(step, jnp.zeros((b, h, dk, dv), f32), xs)
    return jnp.swapaxes(jnp.moveaxis(o, 0, 1), 2, 3).reshape(b, seq, h, dv)


def gdn_mixer(qkv, a_raw, b_raw, gate, conv_w, a_log, dt_bias, norm_g):
    b, seq, _ = qkv.shape
    f32 = jnp.float32
    qkv = jax.nn.silu(causal_dwconv(qkv, conv_w)).astype(f32)
    q = qkv[..., :GDN_HEADS * GDN_DK].reshape(b, seq, GDN_HEADS, GDN_DK)
    k = qkv[..., GDN_HEADS * GDN_DK:2 * GDN_HEADS * GDN_DK].reshape(b, seq, GDN_HEADS, GDN_DK)
    v = qkv[..., 2 * GDN_HEADS * GDN_DK:].reshape(b, seq, GDN_HEADS, GDN_DV)
    q = l2norm(q) * (GDN_DK ** -0.5)
    k = l2norm(k)
    log_decay = -jnp.exp(a_log.astype(f32)) * jax.nn.softplus(a_raw.astype(f32) + dt_bias.astype(f32))
    beta = jax.nn.sigmoid(b_raw.astype(f32))
    o = chunk_gated_delta_rule(q, k, v, log_decay, beta)
    o = rmsnorm(o, norm_g) * jax.nn.silu(gate.astype(f32).reshape(b, seq, GDN_HEADS, GDN_DV))
    return o.reshape(b, seq, GDN_HEADS * GDN_DV).astype(gate.dtype)


def stick_breaking_mixer(q, k, v):
    seq = q.shape[1]
    scale = SB_D ** -0.5
    key_pos = jnp.arange(seq)

    def block(blk, qb):
        z = jnp.einsum('bqhd,bkhd->bhqk', qb, k).astype(jnp.float32) * scale
        q_pos = blk * Q_BLOCK + jnp.arange(Q_BLOCK)
        before = key_pos[None, :] < q_pos[:, None]
        log_1m = jnp.where(before, jax.nn.log_sigmoid(-z), 0.0)
        rest = lax.cumsum(log_1m, axis=3, reverse=True) - log_1m
        w = jnp.where(before, jnp.exp(jax.nn.log_sigmoid(z) + rest), 0.0)
        return jnp.einsum('bhqk,bkhd->bqhd', w.astype(v.dtype), v)

    o = sweep_query_blocks(block, q)
    return o.reshape(q.shape[0], seq, SB_HEADS * SB_D)


def conv_glu_ffn(x, w_up, conv_w, w_down):
    h = causal_dwconv(x @ w_up, conv_w)
    a, bval = jnp.split(h, 2, axis=-1)
    return (jax.nn.silu(a) * bval) @ w_down


def even_mixer(hn, w_in, w_out, q_norm, kv_norm, w_uq, w_ukv,
               a_re, a_im, log_step, b_re, b_im, c_re, c_im, d_skip, w_glu):
    proj = hn @ w_in
    o1 = MLA_Q_RANK
    o2 = o1 + MLA_KV_RANK
    o3 = o2 + MLA_ROPE
    out_a = mla_mixer(proj[..., :o1], proj[..., o1:o2], proj[..., o2:o3],
                      q_norm, kv_norm, w_uq, w_ukv)
    out_b = s5_mixer(proj[..., o3:], a_re, a_im, log_step, b_re, b_im, c_re, c_im, d_skip, w_glu)
    return jnp.concatenate([out_a, out_b], axis=-1) @ w_out


def odd_mixer(hn, w_in, w_out, conv_w, a_log, dt_bias, norm_g):
    b, seq, _ = hn.shape
    proj = hn @ w_in
    o1 = GDN_QKV
    o2 = o1 + GDN_HEADS
    o3 = o2 + GDN_HEADS
    o4 = o3 + GDN_HEADS * GDN_DV
    out_c = gdn_mixer(proj[..., :o1], proj[..., o1:o2], proj[..., o2:o3], proj[..., o3:o4],
                      conv_w, a_log, dt_bias, norm_g)
    sb = proj[..., o4:].reshape(b, seq, 3, SB_HEADS, SB_D)
    out_d = stick_breaking_mixer(sb[:, :, 0], sb[:, :, 1], sb[:, :, 2])
    return jnp.concatenate([out_c, out_d], axis=-1) @ w_out


def setup_inputs(seed: int = 0) -> dict:
    key = jax.random.key(seed)
    ks = iter(jax.random.split(key, 40))
    f32 = jnp.float32
    n_even = (DEPTH + 1) // 2
    n_odd = DEPTH // 2

    def nrm(shape, scale):
        return jax.random.normal(next(ks), shape, f32) * scale

    def gain(shape):
        return 1.0 + nrm(shape, 0.01)

    res_scale = (2.0 * DEPTH) ** -0.5
    inp = {}
    inp['x'] = nrm((BATCH, SEQ, D_MODEL), 1.0)
    inp['ln_mix'] = gain((DEPTH, D_MODEL))
    inp['ln_ffn'] = gain((DEPTH, D_MODEL))
    inp['ln_final'] = gain((D_MODEL,))
    inp['ffn_w_up'] = nrm((DEPTH, D_MODEL, 2 * FFN_HIDDEN), D_MODEL ** -0.5)
    inp['ffn_conv'] = nrm((DEPTH, FFN_CONV, 2 * FFN_HIDDEN), FFN_CONV ** -0.5)
    inp['ffn_w_down'] = nrm((DEPTH, FFN_HIDDEN, D_MODEL), FFN_HIDDEN ** -0.5 * res_scale)
    inp['ev_w_in'] = nrm((n_even, D_MODEL, EVEN_IN), D_MODEL ** -0.5)
    inp['ev_w_out'] = nrm((n_even, EVEN_OUT, D_MODEL), EVEN_OUT ** -0.5 * res_scale)
    inp['mla_q_norm'] = gain((n_even, MLA_Q_RANK))
    inp['mla_kv_norm'] = gain((n_even, MLA_KV_RANK))
    inp['mla_w_uq'] = nrm((n_even, MLA_Q_RANK, MLA_HEADS * (MLA_NOPE + MLA_ROPE)), MLA_Q_RANK ** -0.5)
    inp['mla_w_ukv'] = nrm((n_even, MLA_KV_RANK, MLA_HEADS * (MLA_NOPE + MLA_V)), MLA_KV_RANK ** -0.5)
    inp['s5_a_re'] = -0.5 + nrm((n_even, S5_GROUPS, S5_STATE), 0.01)
    inp['s5_a_im'] = (math.pi * jnp.arange(S5_STATE, dtype=f32))[None, None, :] + nrm((n_even, S5_GROUPS, S5_STATE), 0.01)
    inp['s5_log_step'] = jax.random.uniform(next(ks), (n_even, S5_GROUPS), f32,
                                            math.log(S5_DT_MIN), math.log(S5_DT_MAX))
    inp['s5_b_re'] = nrm((n_even, S5_GROUPS, S5_STATE, S5_GROUP), (2.0 * S5_GROUP) ** -0.5)
    inp['s5_b_im'] = nrm((n_even, S5_GROUPS, S5_STATE, S5_GROUP), (2.0 * S5_GROUP) ** -0.5)
    inp['s5_c_re'] = nrm((n_even, S5_GROUPS, S5_GROUP, S5_STATE), S5_STATE ** -0.5)
    inp['s5_c_im'] = nrm((n_even, S5_GROUPS, S5_GROUP, S5_STATE), S5_STATE ** -0.5)
    inp['s5_d'] = nrm((n_even, S5_WIDTH), 1.0)
    inp['s5_w_glu'] = nrm((n_even, S5_WIDTH, S5_WIDTH), S5_WIDTH ** -0.5)
    inp['od_w_in'] = nrm((n_odd, D_MODEL, ODD_IN), D_MODEL ** -0.5)
    inp['od_w_out'] = nrm((n_odd, ODD_OUT, D_MODEL), ODD_OUT ** -0.5 * res_scale)
    inp['gdn_conv'] = nrm((n_odd, GDN_CONV, GDN_QKV), GDN_CONV ** -0.5)
    inp['gdn_a_log'] = jnp.log(jax.random.uniform(next(ks), (n_odd, GDN_HEADS), f32, 1.0, 16.0))
    dt = jnp.exp(jax.random.uniform(next(ks), (n_odd, GDN_HEADS), f32, math.log(1e-3), math.log(1e-1)))
    inp['gdn_dt_bias'] = dt + jnp.log(-jnp.expm1(-dt))
    inp['gdn_norm'] = gain((n_odd, GDN_DV))
    return inp


def reference(x, ln_mix, ln_ffn, ln_final, ffn_w_up, ffn_conv, ffn_w_down,
              ev_w_in, ev_w_out, mla_q_norm, mla_kv_norm, mla_w_uq, mla_w_ukv,
              s5_a_re, s5_a_im, s5_log_step, s5_b_re, s5_b_im, s5_c_re, s5_c_im, s5_d, s5_w_glu,
              od_w_in, od_w_out, gdn_conv, gdn_a_log, gdn_dt_bias, gdn_norm):
    h = x
    for layer in range(DEPTH):
        i = layer // 2
        hn = rmsnorm(h, ln_mix[layer])
        if layer % 2 == 0:
            h = h + even_mixer(hn, ev_w_in[i], ev_w_out[i], mla_q_norm[i], mla_kv_norm[i],
                               mla_w_uq[i], mla_w_ukv[i], s5_a_re[i], s5_a_im[i], s5_log_step[i],
                               s5_b_re[i], s5_b_im[i], s5_c_re[i], s5_c_im[i], s5_d[i], s5_w_glu[i])
        else:
            h = h + odd_mixer(hn, od_w_in[i], od_w_out[i], gdn_conv[i], gdn_a_log[i],
                              gdn_dt_bias[i], gdn_norm[i])
        h = h + conv_glu_ffn(rmsnorm(h, ln_ffn[layer]), ffn_w_up[layer], ffn_conv[layer],
                             ffn_w_down[layer])
    return rmsnorm(h, ln_final)
```

```python
import functools
import math

import jax
import jax.numpy as jnp
from jax import lax
from jax.experimental import pallas as pl
from jax.experimental.pallas import tpu as pltpu

F32 = jnp.float32
BF16 = jnp.bfloat16
HI = lax.Precision.HIGHEST

CHUNK = 64
NORM_EPS = 1e-6
MLA_NOPE = 128
MLA_ROPE = 64
MLA_V = 128
MLA_QK = 256
ROPE_BASE = 10000.0
S5_GROUP = 16
S5_STATE = 64
GDN_DK = 128
GDN_DV = 128
SB_D = 128

LANE = 128
SUBLANE = 8
VMEM_LIMIT_MAX = 58 * 1024 * 1024
VMEM_LIMIT_MIN = 32 * 1024 * 1024
MASK_VALUE = -1e30


def _cparams(sem, vmem_est):
    limit = int(min(max(vmem_est * 5 // 4 + (4 << 20), VMEM_LIMIT_MIN), VMEM_LIMIT_MAX))
    return pltpu.CompilerParams(dimension_semantics=sem, vmem_limit_bytes=limit)


def _pick(n, pref, mult):
    if n <= pref:
        return n
    t = (pref // mult) * mult
    while t >= mult:
        if n % t == 0:
            return t
        t -= mult
    return n


def _rmsnorm_kernel(x_ref, g_ref, o_ref):
    x = x_ref[...].astype(F32)
    ms = jnp.mean(x * x, axis=-1, keepdims=True)
    o_ref[...] = (x * lax.rsqrt(ms + NORM_EPS) * g_ref[...]).astype(o_ref.dtype)


def _rmsnorm(x, g, out_dtype, *, col0=0, width=None):
    m = x.shape[0]
    width = x.shape[1] if width is None else width
    tm = _pick(m, 256, SUBLANE)
    cb = col0 // width
    est = 2 * tm * width * (x.dtype.itemsize + jnp.dtype(out_dtype).itemsize) + 4 * tm * width * 4
    return pl.pallas_call(
        _rmsnorm_kernel,
        grid=(m // tm,),
        in_specs=[pl.BlockSpec((tm, width), lambda i: (i, cb)),
                  pl.BlockSpec((1, width), lambda i: (0, 0))],
        out_specs=pl.BlockSpec((tm, width), lambda i: (i, 0)),
        out_shape=jax.ShapeDtypeStruct((m, width), out_dtype),
        compiler_params=_cparams(("parallel",), est),
        name="rmsnorm",
    )(x, g.reshape(1, width).astype(F32))


def _mm_kernel(*refs, n_pairs, has_res):
    o_ref = refs[-1]
    acc = None
    for p in range(n_pairs):
        a = refs[2 * p][...].astype(BF16)
        d = jnp.dot(a, refs[2 * p + 1][...], preferred_element_type=F32)
        acc = d if acc is None else acc + d
    if has_res:
        acc = acc + refs[2 * n_pairs][...]
    o_ref[...] = acc.astype(o_ref.dtype)


def _mm_tiles(m, ks, n, a_bytes, out_bytes, has_res, budget=40 << 20):
    for bm_p, bn_p in ((1024, 1024), (1024, 512), (512, 512), (512, 256), (256, 256), (256, 128), (128, 128)):
        bm = _pick(m, bm_p, SUBLANE)
        bn = _pick(n, bn_p, LANE)
        est = bm * bn * 4
        for k, ab in zip(ks, a_bytes):
            est += 2 * (bm * k * ab + k * bn * 2)
        est += 2 * bm * bn * (out_bytes + (4 if has_res else 0))
        if est <= budget:
            return bm, bn, est
    return bm, bn, est


def _matmul(pairs, out_dtype, residual=None, *, a_col0=None):
    m = pairs[0][0].shape[0]
    n = pairs[0][1].shape[1]
    ks = [b.shape[0] for _, b in pairs]
    a_col0 = [0] * len(pairs) if a_col0 is None else a_col0
    a_bytes = [a.dtype.itemsize for a, _ in pairs]
    bm, bn, est = _mm_tiles(m, ks, n, a_bytes, jnp.dtype(out_dtype).itemsize, residual is not None)
    in_specs, args = [], []
    for (a, b), k, c0 in zip(pairs, ks, a_col0):
        cb = c0 // k
        in_specs.append(pl.BlockSpec((bm, k), lambda i, j, cb=cb: (i, cb)))
        in_specs.append(pl.BlockSpec((k, bn), lambda i, j: (0, j)))
        args += [a, b]
    if residual is not None:
        in_specs.append(pl.BlockSpec((bm, bn), lambda i, j: (i, j)))
        args.append(residual)
    return pl.pallas_call(
        functools.partial(_mm_kernel, n_pairs=len(pairs), has_res=residual is not None),
        grid=(m // bm, n // bn),
        in_specs=in_specs,
        out_specs=pl.BlockSpec((bm, bn), lambda i, j: (i, j)),
        out_shape=jax.ShapeDtypeStruct((m, n), out_dtype),
        compiler_params=_cparams(("parallel", "parallel"), est),
        name="matmul",
    )(*args)


def _rope_tables(seq):
    half = MLA_ROPE // 2
    inv_freq = ROPE_BASE ** (-jnp.arange(half, dtype=F32) / half)
    ang = jnp.arange(seq, dtype=F32)[:, None] * inv_freq[None, :]
    cos, sin = jnp.cos(ang), jnp.sin(ang)
    zeros = jnp.zeros((seq, LANE - MLA_ROPE), F32)
    return (jnp.concatenate([cos, cos, zeros], axis=1),
            jnp.concatenate([-sin, sin, zeros], axis=1))


def _rope_lanes(x, cosf, sinf):
    half = MLA_ROPE // 2
    lane = lax.broadcasted_iota(jnp.int32, x.shape, 1)
    swapped = jnp.where(lane < half, pltpu.roll(x, LANE - half, 1), pltpu.roll(x, half, 1))
    return x * cosf + swapped * sinf


def _rope_kernel(x_ref, cos_ref, sin_ref, o_ref):
    o_ref[...] = _rope_lanes(x_ref[...], cos_ref[...], sin_ref[...]).astype(o_ref.dtype)


def _rope_k(x, col0, cosf, sinf, seq):
    m = x.shape[0]
    tm = _pick(seq, 512, SUBLANE)
    per_seq = seq // tm
    cb = col0 // LANE
    return pl.pallas_call(
        _rope_kernel,
        grid=(m // tm,),
        in_specs=[pl.BlockSpec((tm, LANE), lambda i: (i, cb)),
                  pl.BlockSpec((tm, LANE), lambda i: (i % per_seq, 0)),
                  pl.BlockSpec((tm, LANE), lambda i: (i % per_seq, 0))],
        out_specs=pl.BlockSpec((tm, LANE), lambda i: (i, 0)),
        out_shape=jax.ShapeDtypeStruct((m, LANE), BF16),
        compiler_params=_cparams(("parallel",), 16 * tm * LANE * 4),
        name="rope_k",
    )(x, cosf, sinf)


def _mla_kernel(q_ref, kv_ref, kr_ref, cos_ref, sin_ref, o_ref, qs, m_s, l_s, acc_s, *, tq, scale):
    i = pl.program_id(2)
    q = q_ref[...]
    qs[:, :MLA_NOPE] = q[:, :MLA_NOPE].astype(BF16)
    qs[:, MLA_NOPE:] = _rope_lanes(q[:, MLA_NOPE:], cos_ref[...], sin_ref[...]).astype(BF16)

    def tile(j0):
        kn = kv_ref[pl.ds(j0, tq), :MLA_NOPE]
        kr = kr_ref[pl.ds(j0, tq), :]
        k = jnp.concatenate([kn, kr], axis=1)
        s = lax.dot_general(qs[...], k, (((1,), (1,)), ((), ())), preferred_element_type=F32) * scale
        return s, kv_ref[pl.ds(j0, tq), MLA_NOPE:]

    s, v = tile(pl.multiple_of(i * tq, tq))
    rc = lax.broadcasted_iota(jnp.int32, (tq, tq), 0) // CHUNK
    cc = lax.broadcasted_iota(jnp.int32, (tq, tq), 1) // CHUNK
    s = jnp.where(cc <= rc, s, MASK_VALUE)
    m = jnp.max(s, axis=1, keepdims=True)
    p = jnp.exp(s - m)
    m_s[...] = m
    l_s[...] = jnp.sum(p, axis=1, keepdims=True)
    acc_s[...] = jnp.dot(p.astype(BF16), v, preferred_element_type=F32)

    def body(j, carry):
        s, v = tile(pl.multiple_of(j * tq, tq))
        m_prev = m_s[...]
        m_new = jnp.maximum(m_prev, jnp.max(s, axis=1, keepdims=True))
        alpha = jnp.exp(m_prev - m_new)
        p = jnp.exp(s - m_new)
        l_s[...] = alpha * l_s[...] + jnp.sum(p, axis=1, keepdims=True)
        acc_s[...] = alpha * acc_s[...] + jnp.dot(p.astype(BF16), v, preferred_element_type=F32)
        m_s[...] = m_new
        return carry

    lax.fori_loop(0, i, body, 0)
    o_ref[...] = (acc_s[...] / l_s[...]).astype(o_ref.dtype)


def _mla_attention(q, kv, kr, cosf, sinf, batch, seq, heads):
    tq = _pick(seq, 512, CHUNK)
    nq = seq // tq
    scale = (MLA_NOPE + MLA_ROPE) ** -0.5
    est = (2 * tq * MLA_QK * 4 + 2 * seq * (MLA_QK + LANE) * 2 + 4 * tq * LANE * 4 + 2 * tq * LANE * 2
           + tq * MLA_QK * 2 + 3 * tq * LANE * 4 + 6 * tq * tq * 4)
    return pl.pallas_call(
        functools.partial(_mla_kernel, tq=tq, scale=scale),
        grid=(batch, heads, nq),
        in_specs=[pl.BlockSpec((tq, MLA_QK), lambda b, h, i: (b * nq + i, h)),
                  pl.BlockSpec((seq, MLA_QK), lambda b, h, i: (b, h)),
                  pl.BlockSpec((seq, LANE), lambda b, h, i: (b, 0)),
                  pl.BlockSpec((tq, LANE), lambda b, h, i: (i, 0)),
                  pl.BlockSpec((tq, LANE), lambda b, h, i: (i, 0))],
        out_specs=pl.BlockSpec((tq, MLA_V), lambda b, h, i: (b * nq + i, h)),
        out_shape=jax.ShapeDtypeStruct((batch * seq, heads * MLA_V), BF16),
        scratch_shapes=[pltpu.VMEM((tq, MLA_QK), BF16), pltpu.VMEM((tq, 1), F32),
                        pltpu.VMEM((tq, 1), F32), pltpu.VMEM((tq, MLA_V), F32)],
        compiler_params=_cparams(("parallel", "parallel", "arbitrary"), est),
        name="mla_attention",
    )(q, kv, kr, cosf, sinf)


S5_GROUPS_PER_BLOCK = 8


def _s5_kernel(u_ref, bre_ref, bim_ref, cre_ref, cim_ref, pw_ref, d_ref, o_ref, sre, sim, car, *, tt):
    t = pl.program_id(2)

    @pl.when(t == 0)
    def _():
        car[...] = jnp.zeros_like(car)

    u = u_ref[...]
    sre[...] = jnp.dot(u, bre_ref[0], precision=HI, preferred_element_type=F32)
    sim[...] = jnp.dot(u, bim_ref[0], precision=HI, preferred_element_type=F32)
    ns = sre.shape[1]
    row = lax.broadcasted_iota(jnp.int32, (SUBLANE, ns), 0)
    steps = [(1, pw_ref[0, 0], pw_ref[0, 1]), (2, pw_ref[0, 2], pw_ref[0, 3]), (4, pw_ref[0, 4], pw_ref[0, 5])]
    pwr, pwi = pw_ref[0, 6], pw_ref[0, 7]

    def body(r, carry):
        cr, ci = carry
        off = pl.multiple_of(r * SUBLANE, SUBLANE)
        xr = sre[pl.ds(off, SUBLANE), :]
        xi = sim[pl.ds(off, SUBLANE), :]
        for k, ar, ai in steps:
            sr = jnp.where(row >= k, pltpu.roll(xr, k, 0), 0.0)
            si = jnp.where(row >= k, pltpu.roll(xi, k, 0), 0.0)
            xr, xi = xr + ar * sr - ai * si, xi + ar * si + ai * sr
        xr, xi = xr + pwr * cr - pwi * ci, xi + pwr * ci + pwi * cr
        sre[pl.ds(off, SUBLANE), :] = xr
        sim[pl.ds(off, SUBLANE), :] = xi
        return xr[SUBLANE - 1:, :], xi[SUBLANE - 1:, :]

    cr, ci = lax.fori_loop(0, tt // SUBLANE, body, (car[0:1, :], car[1:2, :]))
    car[0:1, :] = cr
    car[1:2, :] = ci
    y = (jnp.dot(sre[...], cre_ref[0], precision=HI, preferred_element_type=F32)
         - jnp.dot(sim[...], cim_ref[0], precision=HI, preferred_element_type=F32))
    y = y + d_ref[...] * u
    o_ref[...] = jax.nn.gelu(y, approximate=True)


def _s5_params(a_re, a_im, log_step, b_re, b_im, c_re, c_im):
    g, p = a_re.shape
    gb = min(S5_GROUPS_PER_BLOCK, g)
    nb = g // gb
    lam = lax.complex(a_re.astype(F32), a_im.astype(F32))
    step = jnp.exp(log_step.astype(F32))[:, None]
    lam_dt = lam * step
    lam_bar = jnp.exp(lam_dt)
    b_bar = ((lam_bar - 1.0) / lam)[..., None] * lax.complex(b_re.astype(F32), b_im.astype(F32))
    c_c = lax.complex(c_re.astype(F32), c_im.astype(F32))
    eye = jnp.eye(gb, dtype=F32)

    def bd_in(x):
        x = x.reshape(nb, gb, p, S5_GROUP)
        return jnp.einsum('ngpi,gh->ngihp', x, eye).reshape(nb, gb * S5_GROUP, gb * p)

    def bd_out(x):
        x = x.reshape(nb, gb, S5_GROUP, p)
        return jnp.einsum('ngip,gh->ngphi', x, eye).reshape(nb, gb * p, gb * S5_GROUP)

    def rows(z):
        z = z.reshape(nb, 1, gb * p)
        return (jnp.broadcast_to(jnp.real(z), (nb, SUBLANE, gb * p)),
                jnp.broadcast_to(jnp.imag(z), (nb, SUBLANE, gb * p)))

    planes = []
    for k in (1, 2, 4):
        planes += list(rows(jnp.exp(lam_dt * float(k))))
    pw = jnp.exp(lam_dt[None] * jnp.arange(1, SUBLANE + 1, dtype=F32)[:, None, None])
    pw = jnp.moveaxis(pw.reshape(SUBLANE, nb, gb * p), 0, 1)
    planes += [jnp.real(pw), jnp.imag(pw)]
    return (bd_in(jnp.real(b_bar)), bd_in(jnp.imag(b_bar)), bd_out(jnp.real(c_c)), bd_out(jnp.imag(c_c)),
            jnp.stack(planes, axis=1), gb)


def _s5_scan(u_src, col0, params, d_skip, batch, seq):
    bre, bim, cre, cim, pw, gb = params
    nb = bre.shape[0]
    gc = gb * S5_GROUP
    ns = gb * S5_STATE
    tt = _pick(seq, 512, SUBLANE)
    nt = seq // tt
    cb0 = col0 // gc
    est = 4 * tt * gc * 4 + 8 * gc * ns * 4 + 4 * SUBLANE * 8 * ns * 4 + 6 * tt * ns * 4
    return pl.pallas_call(
        functools.partial(_s5_kernel, tt=tt),
        grid=(batch, nb, nt),
        in_specs=[pl.BlockSpec((tt, gc), lambda b, g, t: (b * nt + t, cb0 + g)),
                  pl.BlockSpec((1, gc, ns), lambda b, g, t: (g, 0, 0)),
                  pl.BlockSpec((1, gc, ns), lambda b, g, t: (g, 0, 0)),
                  pl.BlockSpec((1, ns, gc), lambda b, g, t: (g, 0, 0)),
                  pl.BlockSpec((1, ns, gc), lambda b, g, t: (g, 0, 0)),
                  pl.BlockSpec((1, 8, SUBLANE, ns), lambda b, g, t: (g, 0, 0, 0)),
                  pl.BlockSpec((1, gc), lambda b, g, t: (0, g))],
        out_specs=pl.BlockSpec((tt, gc), lambda b, g, t: (b * nt + t, g)),
        out_shape=jax.ShapeDtypeStruct((batch * seq, nb * gc), F32),
        scratch_shapes=[pltpu.VMEM((tt, ns), F32), pltpu.VMEM((tt, ns), F32), pltpu.VMEM((SUBLANE, ns), F32)],
        compiler_params=_cparams(("parallel", "parallel", "arbitrary"), est),
        name="s5_scan",
    )(u_src, bre, bim, cre, cim, pw, d_skip.reshape(1, -1).astype(F32))


def _glu_kernel(a_ref, w_ref, g_ref, o_ref):
    z = jnp.dot(a_ref[...].astype(BF16), w_ref[...], preferred_element_type=F32)
    o_ref[...] = (g_ref[...] * jax.nn.sigmoid(z)).astype(o_ref.dtype)


def _s5_glu(g, w_glu):
    m, k = g.shape
    bm, bn, est = _mm_tiles(m, [k], k, [4], 2, True)
    return pl.pallas_call(
        _glu_kernel,
        grid=(m // bm, k // bn),
        in_specs=[pl.BlockSpec((bm, k), lambda i, j: (i, 0)),
                  pl.BlockSpec((k, bn), lambda i, j: (0, j)),
                  pl.BlockSpec((bm, bn), lambda i, j: (i, j))],
        out_specs=pl.BlockSpec((bm, bn), lambda i, j: (i, j)),
        out_shape=jax.ShapeDtypeStruct((m, k), BF16),
        compiler_params=_cparams(("parallel", "parallel"), est),
        name="s5_glu",
    )(g, w_glu, g)


def _shift_rows(h, prev, k):
    r = pltpu.roll(h, k, 0)
    p = pltpu.roll(prev, k, 0)
    row = lax.broadcasted_iota(jnp.int32, prev.shape, 0)
    top = jnp.where(row < k, p, r[:SUBLANE])
    return jnp.concatenate([top, r[SUBLANE:]], axis=0)


def _ffn_up_kernel(a_ref, wa_ref, wb_ref, cwa_ref, cwb_ref, o_ref, carry, *, tiles_per_seq):
    i, j = pl.program_id(0), pl.program_id(1)
    a = a_ref[...]

    @pl.when((i % tiles_per_seq) == 0)
    def _():
        carry[j] = jnp.zeros(carry.shape[1:], F32)

    outs = []
    for s, (w_ref, cw_ref) in enumerate(((wa_ref, cwa_ref), (wb_ref, cwb_ref))):
        h = jnp.dot(a, w_ref[...], preferred_element_type=F32)
        prev = carry[j, s * SUBLANE:(s + 1) * SUBLANE, :]
        carry[j, s * SUBLANE:(s + 1) * SUBLANE, :] = h[h.shape[0] - SUBLANE:, :]
        cw = cw_ref[...]
        outs.append(cw[2:3] * h + cw[1:2] * _shift_rows(h, prev, 1) + cw[0:1] * _shift_rows(h, prev, 2))
    o_ref[...] = (jax.nn.silu(outs[0]) * outs[1]).astype(o_ref.dtype)


def _ffn_up(xn, w_up, conv_w, seq):
    m, k = xn.shape
    f = w_up.shape[1] // 2
    bm = _pick(seq, 1024, SUBLANE)
    bn = _pick(f, 256, LANE)
    nj = f // bn
    est = 2 * bm * k * 2 + 4 * k * bn * 2 + 2 * bm * bn * 2 + nj * 16 * bn * 4 + 10 * bm * bn * 4
    return pl.pallas_call(
        functools.partial(_ffn_up_kernel, tiles_per_seq=seq // bm),
        grid=(m // bm, nj),
        in_specs=[pl.BlockSpec((bm, k), lambda i, j: (i, 0)),
                  pl.BlockSpec((k, bn), lambda i, j: (0, j)),
                  pl.BlockSpec((k, bn), lambda i, j: (0, j + nj)),
                  pl.BlockSpec((SUBLANE, bn), lambda i, j: (0, j)),
                  pl.BlockSpec((SUBLANE, bn), lambda i, j: (0, j + nj))],
        out_specs=pl.BlockSpec((bm, bn), lambda i, j: (i, j)),
        out_shape=jax.ShapeDtypeStruct((m, f), BF16),
        scratch_shapes=[pltpu.VMEM((nj, 2 * SUBLANE, bn), F32)],
        compiler_params=_cparams(("arbitrary", "arbitrary"), est),
        name="ffn_up",
    )(xn, w_up, w_up, conv_w, conv_w)


def _gdn_pre_kernel(x_ref, w_ref, o_ref, *, nq, nk):
    j = pl.program_id(1)
    x = x_ref[...]
    w = w_ref[...]
    kw = 4
    row = lax.broadcasted_iota(jnp.int32, x.shape, 0)
    acc = w[kw - 1:kw] * x
    for s in range(1, kw):
        acc = acc + w[kw - 1 - s:kw - s] * jnp.where(row >= s, pltpu.roll(x, s, 0), 0.0)
    y = jax.nn.silu(acc)
    parts = []
    for c in range(x.shape[1] // GDN_DK):
        yh = y[:, c * GDN_DK:(c + 1) * GDN_DK]
        parts.append(yh * lax.rsqrt(jnp.sum(yh * yh, axis=-1, keepdims=True) + NORM_EPS))
    yn = parts[0] if len(parts) == 1 else jnp.concatenate(parts, axis=1)
    is_q = j < nq
    is_k = jnp.logical_and(j >= nq, j < nq + nk)
    o_ref[...] = jnp.where(is_q, yn * (GDN_DK ** -0.5), jnp.where(is_k, yn, y))


def _gdn_pre(src, conv_w, batch, seq, heads):
    width = conv_w.shape[1]
    cb = GDN_DK
    nq = heads * GDN_DK // cb
    est = 4 * seq * cb * 4 + 8 * seq * cb * 4
    return pl.pallas_call(
        functools.partial(_gdn_pre_kernel, nq=nq, nk=nq),
        grid=(batch, width // cb),
        in_specs=[pl.BlockSpec((seq, cb), lambda b, j: (b, j)),
                  pl.BlockSpec((SUBLANE, cb), lambda b, j: (0, j))],
        out_specs=pl.BlockSpec((seq, cb), lambda b, j: (b, j)),
        out_shape=jax.ShapeDtypeStruct((batch * seq, width), F32),
        compiler_params=_cparams(("parallel", "parallel"), est),
        name="gdn_pre",
    )(src, conv_w)


GDN_ROWS = 256


def _dot_hi(a, b):
    return jnp.dot(a, b, precision=HI, preferred_element_type=F32)


def _dot_nt_hi(a, b):
    return lax.dot_general(a, b, (((1,), (1,)), ((), ())), precision=HI, preferred_element_type=F32)


def _gdn_kernel(alog_ref, dtb_ref, q_ref, k_ref, v_ref, a_ref, b_ref, gate_ref, ng_ref, o_ref, state, *, rows):
    h = pl.program_id(1)
    r = pl.program_id(2)

    @pl.when(r == 0)
    def _():
        state[...] = jnp.zeros_like(state)

    q, k, v = q_ref[...], k_ref[...], v_ref[...]
    a_raw = a_ref[...] + dtb_ref[h]
    softplus = jnp.maximum(a_raw, 0.0) + jnp.log(1.0 + jnp.exp(-jnp.abs(a_raw)))
    g = -jnp.exp(jnp.zeros_like(a_raw) + alog_ref[h]) * softplus
    beta = jax.nn.sigmoid(b_ref[...])
    gb = jnp.broadcast_to(g, (rows, LANE))
    ri = lax.broadcasted_iota(jnp.int32, (rows, rows), 0)
    ci = lax.broadcasted_iota(jnp.int32, (rows, rows), 1)
    same = (ri // CHUNK) == (ci // CHUNK)
    tril = jnp.logical_and(same, ci <= ri)
    strict = jnp.logical_and(same, ci < ri)
    gc = _dot_hi(tril.astype(F32), gb)
    gl = _dot_hi(same.astype(F32), gb)
    gcol = jnp.concatenate([gc] * (rows // LANE), axis=1) if rows > LANE else gc[:, :rows]
    grow = _dot_hi(jnp.ones((rows, rows), F32), jnp.where(ri == ci, gcol, 0.0))
    decay = jnp.exp(jnp.where(tril, gcol - grow, MASK_VALUE))
    kb = k * beta
    vb = v * beta
    lmat = jnp.where(strict, _dot_nt_hi(kb, k) * decay, 0.0)
    tinv = jnp.where(ri == ci, 1.0, 0.0) - lmat
    pw = lmat
    for _ in range(int(math.log2(CHUNK)) - 1):
        pw = _dot_hi(pw, pw)
        tinv = tinv + _dot_hi(tinv, pw)
    eg = jnp.exp(gc)
    sol = _dot_hi(tinv, jnp.concatenate([vb, kb * eg], axis=1))
    u_c, w_c = sol[:, :GDN_DV], sol[:, GDN_DV:]
    intra = _dot_nt_hi(q, k) * decay
    q_dec = q * eg
    k_dec = k * jnp.exp(gl - gc)
    egl = jnp.exp(gl)
    s = state[...]
    outs = []
    for c in range(rows // CHUNK):
        lo, hi = c * CHUNK, (c + 1) * CHUNK
        v_new = u_c[lo:hi] - _dot_hi(w_c[lo:hi], s)
        outs.append(_dot_hi(q_dec[lo:hi], s) + _dot_hi(intra[lo:hi, lo:hi], v_new))
        s = s * egl[lo:lo + 1, :] + lax.dot_general(k_dec[lo:hi], v_new, (((0,), (0,)), ((), ())),
                                                     precision=HI, preferred_element_type=F32)
    state[...] = s
    o = jnp.concatenate(outs, axis=0)
    o = o * lax.rsqrt(jnp.mean(o * o, axis=-1, keepdims=True) + NORM_EPS) * ng_ref[...]
    o_ref[...] = (o * jax.nn.silu(gate_ref[...])).astype(o_ref.dtype)


def _gdn_core(qkv, a_col, b_col, gate_src, gate_col0, a_log, dt_bias, norm_g, batch, seq, heads):
    rows = _pick(seq, GDN_ROWS, CHUNK)
    nr = seq // rows
    gcb = gate_col0 // GDN_DV
    est = 12 * rows * LANE * 4 + 16 * rows * rows * 4 + 8 * rows * LANE * 4
    tok = lambda b, h, r: b * nr + r
    return pl.pallas_call(
        functools.partial(_gdn_kernel, rows=rows),
        grid=(batch, heads, nr),
        in_specs=[pl.BlockSpec(memory_space=pltpu.SMEM),
                  pl.BlockSpec(memory_space=pltpu.SMEM),
                  pl.BlockSpec((rows, GDN_DK), lambda b, h, r: (tok(b, h, r), h)),
                  pl.BlockSpec((rows, GDN_DK), lambda b, h, r: (tok(b, h, r), heads + h)),
                  pl.BlockSpec((rows, GDN_DV), lambda b, h, r: (tok(b, h, r), 2 * heads + h)),
                  pl.BlockSpec((None, None, rows, 1), lambda b, h, r: (b, h, r, 0)),
                  pl.BlockSpec((None, None, rows, 1), lambda b, h, r: (b, h, r, 0)),
                  pl.BlockSpec((rows, GDN_DV), lambda b, h, r: (tok(b, h, r), gcb + h)),
                  pl.BlockSpec((1, GDN_DV), lambda b, h, r: (0, 0))],
        out_specs=pl.BlockSpec((rows, GDN_DV), lambda b, h, r: (tok(b, h, r), h)),
        out_shape=jax.ShapeDtypeStruct((batch * seq, heads * GDN_DV), BF16),
        scratch_shapes=[pltpu.VMEM((GDN_DK, GDN_DV), F32)],
        compiler_params=_cparams(("parallel", "parallel", "arbitrary"), est),
        name="gdn_core",
    )(a_log.astype(F32), dt_bias.astype(F32), qkv, qkv, qkv, a_col, b_col, gate_src,
      norm_g.reshape(1, GDN_DV).astype(F32))


def _sb_kernel(q_ref, k_ref, v_ref, o_ref, acc_s, c_s, *, tq, scale):
    i = pl.program_id(2)
    q = q_ref[...]
    ri = lax.broadcasted_iota(jnp.int32, (tq, tq), 0)
    ci = lax.broadcasted_iota(jnp.int32, (tq, tq), 1)
    upper = jnp.where(ri > ci, 1.0, 0.0).astype(BF16)
    before = ci < ri
    acc_s[...] = jnp.zeros_like(acc_s)
    c_s[...] = jnp.zeros_like(c_s)

    def tile(j0, masked):
        k = k_ref[pl.ds(j0, tq), :]
        v = v_ref[pl.ds(j0, tq), :]
        z = lax.dot_general(q, k, (((1,), (1,)), ((), ())), preferred_element_type=F32) * scale
        lm = -(jnp.maximum(z, 0.0) + jnp.log(1.0 + jnp.exp(-jnp.abs(z))))
        if masked:
            lm = jnp.where(before, lm, 0.0)
        hi = lm.astype(BF16)
        lo = (lm - hi.astype(F32)).astype(BF16)
        rest = (jnp.dot(hi, upper, preferred_element_type=F32)
                + jnp.dot(lo, upper, preferred_element_type=F32))
        w = jnp.exp(z + lm + rest + c_s[...])
        if masked:
            w = jnp.where(before, w, 0.0)
        acc_s[...] += jnp.dot(w.astype(BF16), v, preferred_element_type=F32)
        c_s[...] += jnp.sum(lm, axis=1, keepdims=True)

    tile(pl.multiple_of(i * tq, tq), True)

    def body(jj, carry):
        tile(pl.multiple_of((i - 1 - jj) * tq, tq), False)
        return carry

    lax.fori_loop(0, i, body, 0)
    o_ref[...] = acc_s[...].astype(o_ref.dtype)


def _sb_attention(qkv, batch, seq, heads):
    tq = _pick(seq, 256, LANE)
    nq = seq // tq
    scale = SB_D ** -0.5
    est = 4 * tq * SB_D * 2 + 4 * seq * SB_D * 2 + 2 * tq * SB_D * 4 + 12 * tq * tq * 4
    return pl.pallas_call(
        functools.partial(_sb_kernel, tq=tq, scale=scale),
        grid=(batch, heads, nq),
        in_specs=[pl.BlockSpec((tq, SB_D), lambda b, h, i: (b * nq + i, h)),
                  pl.BlockSpec((seq, SB_D), lambda b, h, i: (b, heads + h)),
                  pl.BlockSpec((seq, SB_D), lambda b, h, i: (b, 2 * heads + h))],
        out_specs=pl.BlockSpec((tq, SB_D), lambda b, h, i: (b * nq + i, h)),
        out_shape=jax.ShapeDtypeStruct((batch * seq, heads * SB_D), BF16),
        scratch_shapes=[pltpu.VMEM((tq, SB_D), F32), pltpu.VMEM((tq, 1), F32)],
        compiler_params=_cparams(("parallel", "parallel", "arbitrary"), est),
        name="sb_attention",
    )(qkv, qkv, qkv)


def _pad_cols(w, n):
    return jnp.pad(w, ((0, 0), (0, n - w.shape[1])))


def _ffn(h, ln_g, w_up, conv_w, w_down, seq):
    xn = _rmsnorm(h, ln_g, BF16)
    conv8 = jnp.pad(conv_w.astype(F32), ((0, SUBLANE - conv_w.shape[0]), (0, 0)))
    act = _ffn_up(xn, w_up.astype(BF16), conv8, seq)
    return _matmul([(act, w_down.astype(BF16))], F32, residual=h)


def _even_layer(h, ln_g, w_in, w_out, q_norm, kv_norm, w_uq, w_ukv,
                a_re, a_im, log_step, b_re, b_im, c_re, c_im, d_skip, w_glu, batch, seq):
    d = h.shape[1]
    q_rank, kv_rank = q_norm.shape[0], kv_norm.shape[0]
    heads = w_uq.shape[1] // (MLA_NOPE + MLA_ROPE)
    s5_width = d_skip.shape[0]
    o_kr = q_rank + kv_rank
    o_s5 = o_kr + MLA_ROPE
    latent_w = -(-(o_kr + LANE) // s5_width) * s5_width if s5_width >= LANE else o_kr + LANE
    w_lat = _pad_cols(w_in[:, :o_s5], latent_w)
    w_cat = jnp.concatenate([w_lat, w_in[:, o_s5:]], axis=1).astype(BF16)
    hn = _rmsnorm(h, ln_g, BF16)
    proj = _matmul([(hn, w_cat)], F32)
    cq = _rmsnorm(proj, q_norm, BF16, col0=0, width=q_rank)
    ckv = _rmsnorm(proj, kv_norm, BF16, col0=q_rank, width=kv_rank)
    w_uq_p = jnp.pad(w_uq.reshape(q_rank, heads, MLA_NOPE + MLA_ROPE),
                     ((0, 0), (0, 0), (0, MLA_QK - MLA_NOPE - MLA_ROPE))).reshape(q_rank, heads * MLA_QK)
    q = _matmul([(cq, w_uq_p.astype(BF16))], F32)
    kv = _matmul([(ckv, w_ukv.astype(BF16))], BF16)
    cosf, sinf = _rope_tables(seq)
    kr = _rope_k(proj, o_kr, cosf, sinf, seq)
    out_a = _mla_attention(q, kv, kr, cosf, sinf, batch, seq, heads)
    params = _s5_params(a_re, a_im, log_step, b_re, b_im, c_re, c_im)
    g = _s5_scan(proj, latent_w, params, d_skip, batch, seq)
    out_b = _s5_glu(g, w_glu.astype(BF16))
    na = heads * MLA_V
    return _matmul([(out_a, w_out[:na].astype(BF16)), (out_b, w_out[na:].astype(BF16))], F32, residual=h)


def _odd_layer(h, ln_g, w_in, w_out, conv_w, a_log, dt_bias, norm_g, batch, seq):
    heads = a_log.shape[0]
    qkv_w = heads * (2 * GDN_DK + GDN_DV)
    o2 = qkv_w + 2 * heads
    o3 = o2 + heads * GDN_DV
    hn = _rmsnorm(h, ln_g, BF16)
    w_main = jnp.concatenate([w_in[:, :qkv_w], w_in[:, o2:o3]], axis=1).astype(BF16)
    w_ab = _pad_cols(w_in[:, qkv_w:o2], LANE).astype(BF16)
    w_sb = w_in[:, o3:].astype(BF16)
    main = _matmul([(hn, w_main)], F32)
    ab = _matmul([(hn, w_ab)], F32)
    sb = _matmul([(hn, w_sb)], BF16)
    conv8 = jnp.pad(conv_w.astype(F32), ((0, SUBLANE - conv_w.shape[0]), (0, 0)))
    qkv = _gdn_pre(main, conv8, batch, seq, heads)
    ab = ab.reshape(batch, seq, LANE)
    a_col = jnp.swapaxes(ab[:, :, :heads], 1, 2)[..., None]
    b_col = jnp.swapaxes(ab[:, :, heads:2 * heads], 1, 2)[..., None]
    out_c = _gdn_core(qkv, a_col, b_col, main, qkv_w, a_log, dt_bias, norm_g, batch, seq, heads)
    out_d = _sb_attention(sb, batch, seq, sb.shape[1] // (3 * SB_D))
    nc = heads * GDN_DV
    return _matmul([(out_c, w_out[:nc].astype(BF16)), (out_d, w_out[nc:].astype(BF16))], F32, residual=h)


def kernel(x, ln_mix, ln_ffn, ln_final, ffn_w_up, ffn_conv, ffn_w_down, ev_w_in, ev_w_out, mla_q_norm,
           mla_kv_norm, mla_w_uq, mla_w_ukv, s5_a_re, s5_a_im, s5_log_step, s5_b_re, s5_b_im, s5_c_re,
           s5_c_im, s5_d, s5_w_glu, od_w_in, od_w_out, gdn_conv, gdn_a_log, gdn_dt_bias, gdn_norm):
    batch, seq, d = x.shape
    h = x.reshape(batch * seq, d).astype(F32)
    for layer in range(ln_mix.shape[0]):
        i = layer // 2
        if layer % 2 == 0:
            h = _even_layer(h, ln_mix[layer], ev_w_in[i], ev_w_out[i], mla_q_norm[i], mla_kv_norm[i],
                            mla_w_uq[i], mla_w_ukv[i], s5_a_re[i], s5_a_im[i], s5_log_step[i], s5_b_re[i],
                            s5_b_im[i], s5_c_re[i], s5_c_im[i], s5_d[i], s5_w_glu[i], batch, seq)
        else:
            h = _odd_layer(h, ln_mix[layer], od_w_in[i], od_w_out[i], gdn_conv[i], gdn_a_log[i],
                           gdn_dt_bias[i], gdn_norm[i], batch, seq)
        h = _ffn(h, ln_ffn[layer], ffn_w_up[layer], ffn_conv[layer], ffn_w_down[layer], seq)
    return _rmsnorm(h, ln_final, x.dtype).reshape(batch, seq, d)
```

```python
import functools
import math

import jax
import jax.numpy as jnp
from jax import lax
from jax.experimental import pallas as pl
from jax.experimental.pallas import tpu as pltpu

F32 = jnp.float32
BF16 = jnp.bfloat16
HI = lax.Precision.HIGHEST

CHUNK = 64
NORM_EPS = 1e-6
MLA_NOPE = 128
MLA_ROPE = 64
MLA_V = 128
MLA_QK = 256
ROPE_BASE = 10000.0
S5_GROUP = 16
S5_STATE = 64
GDN_DK = 128
GDN_DV = 128
SB_D = 128

LANE = 128
SUBLANE = 8
VMEM_LIMIT_MAX = 58 * 1024 * 1024
VMEM_LIMIT_MIN = 32 * 1024 * 1024
MASK_VALUE = -1e30


def _cparams(sem, vmem_est):
    limit = int(min(max(vmem_est * 5 // 4 + (4 << 20), VMEM_LIMIT_MIN), VMEM_LIMIT_MAX))
    return pltpu.CompilerParams(dimension_semantics=sem, vmem_limit_bytes=limit)


def _pick(n, pref, mult):
    if n <= pref:
        return n
    t = (pref // mult) * mult
    while t >= mult:
        if n % t == 0:
            return t
        t -= mult
    return n


def _rmsnorm_kernel(x_ref, g_ref, o_ref):
    x = x_ref[...].astype(F32)
    ms = jnp.mean(x * x, axis=-1, keepdims=True)
    o_ref[...] = (x * lax.rsqrt(ms + NORM_EPS) * g_ref[...]).astype(o_ref.dtype)


def _rmsnorm(x, g, out_dtype, *, col0=0, width=None):
    m = x.shape[0]
    width = x.shape[1] if width is None else width
    tm = _pick(m, 256, SUBLANE)
    cb = col0 // width
    est = 2 * tm * width * (x.dtype.itemsize + jnp.dtype(out_dtype).itemsize) + 4 * tm * width * 4
    return pl.pallas_call(
        _rmsnorm_kernel,
        grid=(m // tm,),
        in_specs=[pl.BlockSpec((tm, width), lambda i: (i, cb)),
                  pl.BlockSpec((1, width), lambda i: (0, 0))],
        out_specs=pl.BlockSpec((tm, width), lambda i: (i, 0)),
        out_shape=jax.ShapeDtypeStruct((m, width), out_dtype),
        compiler_params=_cparams(("parallel",), est),
        name="rmsnorm",
    )(x, g.reshape(1, width).astype(F32))


def _mm_kernel(*refs, n_pairs, has_res):
    o_ref = refs[-1]
    acc = None
    for p in range(n_pairs):
        a = refs[2 * p][...].astype(BF16)
        d = jnp.dot(a, refs[2 * p + 1][...], preferred_element_type=F32)
        acc = d if acc is None else acc + d
    if has_res:
        acc = acc + refs[2 * n_pairs][...]
    o_ref[...] = acc.astype(o_ref.dtype)


def _mm_tiles(m, ks, n, a_bytes, out_bytes, has_res, budget=40 << 20):
    for bm_p, bn_p in ((1024, 1024), (1024, 512), (512, 512), (512, 256), (256, 256), (256, 128), (128, 128)):
        bm = _pick(m, bm_p, SUBLANE)
        bn = _pick(n, bn_p, LANE)
        est = bm * bn * 4
        for k, ab in zip(ks, a_bytes):
            est += 2 * (bm * k * ab + k * bn * 2)
        est += 2 * bm * bn * (out_bytes + (4 if has_res else 0))
        if est <= budget:
            return bm, bn, est
    return bm, bn, est


def _matmul(pairs, out_dtype, residual=None, *, a_col0=None):
    m = pairs[0][0].shape[0]
    n = pairs[0][1].shape[1]
    ks = [b.shape[0] for _, b in pairs]
    a_col0 = [0] * len(pairs) if a_col0 is None else a_col0
    a_bytes = [a.dtype.itemsize for a, _ in pairs]
    bm, bn, est = _mm_tiles(m, ks, n, a_bytes, jnp.dtype(out_dtype).itemsize, residual is not None)
    in_specs, args = [], []
    for (a, b), k, c0 in zip(pairs, ks, a_col0):
        cb = c0 // k
        in_specs.append(pl.BlockSpec((bm, k), lambda i, j, cb=cb: (i, cb)))
        in_specs.append(pl.BlockSpec((k, bn), lambda i, j: (0, j)))
        args += [a, b]
    if residual is not None:
        in_specs.append(pl.BlockSpec((bm, bn), lambda i, j: (i, j)))
        args.append(residual)
    return pl.pallas_call(
        functools.partial(_mm_kernel, n_pairs=len(pairs), has_res=residual is not None),
        grid=(m // bm, n // bn),
        in_specs=in_specs,
        out_specs=pl.BlockSpec((bm, bn), lambda i, j: (i, j)),
        out_shape=jax.ShapeDtypeStruct((m, n), out_dtype),
        compiler_params=_cparams(("parallel", "parallel"), est),
        name="matmul",
    )(*args)


def _rope_tables(seq):
    half = MLA_ROPE // 2
    inv_freq = ROPE_BASE ** (-jnp.arange(half, dtype=F32) / half)
    ang = jnp.arange(seq, dtype=F32)[:, None] * inv_freq[None, :]
    cos, sin = jnp.cos(ang), jnp.sin(ang)
    zeros = jnp.zeros((seq, LANE - MLA_ROPE), F32)
    return (jnp.concatenate([cos, cos, zeros], axis=1),
            jnp.concatenate([-sin, sin, zeros], axis=1))


def _rope_lanes(x, cosf, sinf):
    half = MLA_ROPE // 2
    lane = lax.broadcasted_iota(jnp.int32, x.shape, 1)
    swapped = jnp.where(lane < half, pltpu.roll(x, LANE - half, 1), pltpu.roll(x, half, 1))
    return x * cosf + swapped * sinf


def _rope_kernel(x_ref, cos_ref, sin_ref, o_ref):
    o_ref[...] = _rope_lanes(x_ref[...], cos_ref[...], sin_ref[...]).astype(o_ref.dtype)


def _rope_k(x, col0, cosf, sinf, seq):
    m = x.shape[0]
    tm = _pick(seq, 512, SUBLANE)
    per_seq = seq // tm
    cb = col0 // LANE
    return pl.pallas_call(
        _rope_kernel,
        grid=(m // tm,),
        in_specs=[pl.BlockSpec((tm, LANE), lambda i: (i, cb)),
                  pl.BlockSpec((tm, LANE), lambda i: (i % per_seq, 0)),
                  pl.BlockSpec((tm, LANE), lambda i: (i % per_seq, 0))],
        out_specs=pl.BlockSpec((tm, LANE), lambda i: (i, 0)),
        out_shape=jax.ShapeDtypeStruct((m, LANE), BF16),
        compiler_params=_cparams(("parallel",), 16 * tm * LANE * 4),
        name="rope_k",
    )(x, cosf, sinf)


def _mla_kernel(q_ref, kt_ref, v_ref, cos_ref, sin_ref, o_ref, qs, m_s, l_s, acc_s, *, tq, scale):
    i = pl.program_id(2)
    q = q_ref[...]
    qs[:, :MLA_NOPE] = q[:, :MLA_NOPE].astype(BF16)
    qs[:, MLA_NOPE:] = _rope_lanes(q[:, MLA_NOPE:], cos_ref[...], sin_ref[...]).astype(BF16)

    def tile(j):
        s = jnp.dot(qs[...], kt_ref[j], preferred_element_type=F32) * scale
        return s, v_ref[pl.ds(pl.multiple_of(j * tq, tq), tq), :]

    s, v = tile(i)
    rc = lax.broadcasted_iota(jnp.int32, (tq, tq), 0) // CHUNK
    cc = lax.broadcasted_iota(jnp.int32, (tq, tq), 1) // CHUNK
    s = jnp.where(cc <= rc, s, MASK_VALUE)
    m = jnp.max(s, axis=1, keepdims=True)
    p = jnp.exp(s - m)
    m_s[...] = m
    l_s[...] = jnp.sum(p, axis=1, keepdims=True)
    acc_s[...] = jnp.dot(p.astype(BF16), v, preferred_element_type=F32)

    def body(j, carry):
        s, v = tile(j)
        m_prev = m_s[...]
        m_new = jnp.maximum(m_prev, jnp.max(s, axis=1, keepdims=True))
        alpha = jnp.exp(m_prev - m_new)
        p = jnp.exp(s - m_new)
        l_s[...] = alpha * l_s[...] + jnp.sum(p, axis=1, keepdims=True)
        acc_s[...] = alpha * acc_s[...] + jnp.dot(p.astype(BF16), v, preferred_element_type=F32)
        m_s[...] = m_new
        return carry

    lax.fori_loop(0, i, body, 0)
    o_ref[...] = (acc_s[...] / l_s[...]).astype(o_ref.dtype)


MLA_TILE = 1024


def _mla_attention(q, kv, kr, cosf, sinf, batch, seq, heads):
    tq = _pick(seq, MLA_TILE, CHUNK)
    nq = seq // tq
    scale = (MLA_NOPE + MLA_ROPE) ** -0.5
    kn_t = jnp.transpose(kv.reshape(batch, nq, tq, heads, MLA_QK)[..., :MLA_NOPE], (0, 3, 1, 4, 2))
    kr_t = jnp.transpose(kr.reshape(batch, nq, tq, LANE), (0, 1, 3, 2))
    kt = jnp.concatenate([kn_t, jnp.broadcast_to(kr_t[:, None], (batch, heads, nq, LANE, tq))], axis=3)
    est = (2 * tq * MLA_QK * 4 + 2 * seq * (MLA_QK + LANE) * 2 + 4 * tq * LANE * 4 + 2 * tq * LANE * 2
           + tq * MLA_QK * 2 + 3 * tq * LANE * 4 + 6 * tq * tq * 4)
    return pl.pallas_call(
        functools.partial(_mla_kernel, tq=tq, scale=scale),
        grid=(batch, heads, nq),
        in_specs=[pl.BlockSpec((tq, MLA_QK), lambda b, h, i: (b * nq + i, h)),
                  pl.BlockSpec((None, None, nq, MLA_QK, tq), lambda b, h, i: (b, h, 0, 0, 0)),
                  pl.BlockSpec((seq, MLA_V), lambda b, h, i: (b, 2 * h + 1)),
                  pl.BlockSpec((tq, LANE), lambda b, h, i: (i, 0)),
                  pl.BlockSpec((tq, LANE), lambda b, h, i: (i, 0))],
        out_specs=pl.BlockSpec((tq, MLA_V), lambda b, h, i: (b * nq + i, h)),
        out_shape=jax.ShapeDtypeStruct((batch * seq, heads * MLA_V), BF16),
        scratch_shapes=[pltpu.VMEM((tq, MLA_QK), BF16), pltpu.VMEM((tq, 1), F32),
                        pltpu.VMEM((tq, 1), F32), pltpu.VMEM((tq, MLA_V), F32)],
        compiler_params=_cparams(("parallel", "parallel", "arbitrary"), est),
        name="mla_attention",
    )(q, kt, kv, cosf, sinf)


S5_GROUPS_PER_BLOCK = 8


def _s5_kernel(u_ref, bre_ref, bim_ref, cre_ref, cim_ref, pw_ref, d_ref, o_ref, sre, sim, car, *, tt):
    t = pl.program_id(2)

    @pl.when(t == 0)
    def _():
        car[...] = jnp.zeros_like(car)

    u = u_ref[...]
    ub = u.astype(BF16)
    sre[...] = jnp.dot(ub, bre_ref[0], preferred_element_type=F32)
    sim[...] = jnp.dot(ub, bim_ref[0], preferred_element_type=F32)
    ns = sre.shape[1]
    row = lax.broadcasted_iota(jnp.int32, (SUBLANE, ns), 0)
    steps = [(1, pw_ref[0, 0], pw_ref[0, 1]), (2, pw_ref[0, 2], pw_ref[0, 3]), (4, pw_ref[0, 4], pw_ref[0, 5])]
    pwr, pwi = pw_ref[0, 6], pw_ref[0, 7]

    def body(r, carry):
        cr, ci = carry
        off = pl.multiple_of(r * SUBLANE, SUBLANE)
        xr = sre[pl.ds(off, SUBLANE), :]
        xi = sim[pl.ds(off, SUBLANE), :]
        for k, ar, ai in steps:
            sr = jnp.where(row >= k, pltpu.roll(xr, k, 0), 0.0)
            si = jnp.where(row >= k, pltpu.roll(xi, k, 0), 0.0)
            xr, xi = xr + ar * sr - ai * si, xi + ar * si + ai * sr
        xr, xi = xr + pwr * cr - pwi * ci, xi + pwr * ci + pwi * cr
        sre[pl.ds(off, SUBLANE), :] = xr
        sim[pl.ds(off, SUBLANE), :] = xi
        return xr[SUBLANE - 1:, :], xi[SUBLANE - 1:, :]

    cr, ci = lax.fori_loop(0, tt // SUBLANE, body, (car[0:1, :], car[1:2, :]))
    car[0:1, :] = cr
    car[1:2, :] = ci
    y = (jnp.dot(sre[...].astype(BF16), cre_ref[0], preferred_element_type=F32)
         - jnp.dot(sim[...].astype(BF16), cim_ref[0], preferred_element_type=F32))
    y = y + d_ref[...] * u
    o_ref[...] = jax.nn.gelu(y, approximate=True)


def _s5_params(a_re, a_im, log_step, b_re, b_im, c_re, c_im):
    g, p = a_re.shape
    gb = min(S5_GROUPS_PER_BLOCK, g)
    nb = g // gb
    lam = lax.complex(a_re.astype(F32), a_im.astype(F32))
    step = jnp.exp(log_step.astype(F32))[:, None]
    lam_dt = lam * step
    lam_bar = jnp.exp(lam_dt)
    b_bar = ((lam_bar - 1.0) / lam)[..., None] * lax.complex(b_re.astype(F32), b_im.astype(F32))
    c_c = lax.complex(c_re.astype(F32), c_im.astype(F32))
    eye = jnp.eye(gb, dtype=F32)

    def bd_in(x):
        x = x.reshape(nb, gb, p, S5_GROUP)
        return jnp.einsum('ngpi,gh->ngihp', x, eye).reshape(nb, gb * S5_GROUP, gb * p)

    def bd_out(x):
        x = x.reshape(nb, gb, S5_GROUP, p)
        return jnp.einsum('ngip,gh->ngphi', x, eye).reshape(nb, gb * p, gb * S5_GROUP)

    def rows(z):
        z = z.reshape(nb, 1, gb * p)
        return (jnp.broadcast_to(jnp.real(z), (nb, SUBLANE, gb * p)),
                jnp.broadcast_to(jnp.imag(z), (nb, SUBLANE, gb * p)))

    planes = []
    for k in (1, 2, 4):
        planes += list(rows(jnp.exp(lam_dt * float(k))))
    pw = jnp.exp(lam_dt[None] * jnp.arange(1, SUBLANE + 1, dtype=F32)[:, None, None])
    pw = jnp.moveaxis(pw.reshape(SUBLANE, nb, gb * p), 0, 1)
    planes += [jnp.real(pw), jnp.imag(pw)]
    return (bd_in(jnp.real(b_bar)).astype(BF16), bd_in(jnp.imag(b_bar)).astype(BF16),
            bd_out(jnp.real(c_c)).astype(BF16), bd_out(jnp.imag(c_c)).astype(BF16),
            jnp.stack(planes, axis=1), gb)


def _s5_scan(u_src, col0, params, d_skip, batch, seq):
    bre, bim, cre, cim, pw, gb = params
    nb = bre.shape[0]
    gc = gb * S5_GROUP
    ns = gb * S5_STATE
    tt = _pick(seq, 512, SUBLANE)
    nt = seq // tt
    cb0 = col0 // gc
    est = 4 * tt * gc * 4 + 8 * gc * ns * 4 + 4 * SUBLANE * 8 * ns * 4 + 6 * tt * ns * 4
    return pl.pallas_call(
        functools.partial(_s5_kernel, tt=tt),
        grid=(batch, nb, nt),
        in_specs=[pl.BlockSpec((tt, gc), lambda b, g, t: (b * nt + t, cb0 + g)),
                  pl.BlockSpec((1, gc, ns), lambda b, g, t: (g, 0, 0)),
                  pl.BlockSpec((1, gc, ns), lambda b, g, t: (g, 0, 0)),
                  pl.BlockSpec((1, ns, gc), lambda b, g, t: (g, 0, 0)),
                  pl.BlockSpec((1, ns, gc), lambda b, g, t: (g, 0, 0)),
                  pl.BlockSpec((1, 8, SUBLANE, ns), lambda b, g, t: (g, 0, 0, 0)),
                  pl.BlockSpec((1, gc), lambda b, g, t: (0, g))],
        out_specs=pl.BlockSpec((tt, gc), lambda b, g, t: (b * nt + t, g)),
        out_shape=jax.ShapeDtypeStruct((batch * seq, nb * gc), F32),
        scratch_shapes=[pltpu.VMEM((tt, ns), F32), pltpu.VMEM((tt, ns), F32), pltpu.VMEM((SUBLANE, ns), F32)],
        compiler_params=_cparams(("parallel", "parallel", "arbitrary"), est),
        name="s5_scan",
    )(u_src, bre, bim, cre, cim, pw, d_skip.reshape(1, -1).astype(F32))


def _glu_kernel(a_ref, w_ref, g_ref, o_ref):
    z = jnp.dot(a_ref[...].astype(BF16), w_ref[...], preferred_element_type=F32)
    o_ref[...] = (g_ref[...] * jax.nn.sigmoid(z)).astype(o_ref.dtype)


def _s5_glu(g, w_glu):
    m, k = g.shape
    bm, bn, est = _mm_tiles(m, [k], k, [4], 2, True)
    return pl.pallas_call(
        _glu_kernel,
        grid=(m // bm, k // bn),
        in_specs=[pl.BlockSpec((bm, k), lambda i, j: (i, 0)),
                  pl.BlockSpec((k, bn), lambda i, j: (0, j)),
                  pl.BlockSpec((bm, bn), lambda i, j: (i, j))],
        out_specs=pl.BlockSpec((bm, bn), lambda i, j: (i, j)),
        out_shape=jax.ShapeDtypeStruct((m, k), BF16),
        compiler_params=_cparams(("parallel", "parallel"), est),
        name="s5_glu",
    )(g, w_glu, g)


def _shift_rows(h, prev, k):
    r = pltpu.roll(h, k, 0)
    p = pltpu.roll(prev, k, 0)
    row = lax.broadcasted_iota(jnp.int32, prev.shape, 0)
    top = jnp.where(row < k, p, r[:SUBLANE])
    return jnp.concatenate([top, r[SUBLANE:]], axis=0)


def _ffn_up_kernel(a_ref, wa_ref, wb_ref, cwa_ref, cwb_ref, o_ref, hbuf, carry, *, tiles_per_seq, nj):
    i, j = pl.program_id(0), pl.program_id(1)
    slot = i % 2
    rows = hbuf.shape[1]

    @pl.when(jnp.logical_and(i == 0, j == 0))
    def _():
        hbuf[...] = jnp.zeros_like(hbuf)
        carry[...] = jnp.zeros_like(carry)

    def epilogue():
        jp = jnp.maximum(j - 1, 0)
        keep = jnp.where((i % tiles_per_seq) == 0, 0.0, 1.0)
        outs = []
        for s, cw_ref in enumerate((cwa_ref, cwb_ref)):
            h = hbuf[s]
            prev = carry[1 - slot, jp, s * SUBLANE:(s + 1) * SUBLANE, :] * keep
            cw = cw_ref[...]
            outs.append(cw[2:3] * h + cw[1:2] * _shift_rows(h, prev, 1) + cw[0:1] * _shift_rows(h, prev, 2))
        o_ref[...] = (jax.nn.silu(outs[0]) * outs[1]).astype(o_ref.dtype)

    @pl.when(j < nj)
    def _():
        a = a_ref[...]
        ha = jnp.dot(a, wa_ref[...], preferred_element_type=F32)
        hb = jnp.dot(a, wb_ref[...], preferred_element_type=F32)
        epilogue()
        hbuf[0] = ha
        hbuf[1] = hb
        carry[slot, j, 0:SUBLANE, :] = ha[rows - SUBLANE:, :]
        carry[slot, j, SUBLANE:2 * SUBLANE, :] = hb[rows - SUBLANE:, :]

    @pl.when(j == nj)
    def _():
        epilogue()


def _ffn_up(xn, w_up, conv_w, seq):
    m, k = xn.shape
    f = w_up.shape[1] // 2
    bm = _pick(seq, 1024, SUBLANE)
    bn = _pick(f, 256, LANE)
    nj = f // bn
    est = (2 * bm * k * 2 + 4 * k * bn * 2 + 2 * bm * bn * 2 + 2 * nj * 16 * bn * 4 + 2 * bm * bn * 4
           + 12 * bm * bn * 4)
    cur = lambda j: jnp.minimum(j, nj - 1)
    prv = lambda j: jnp.maximum(j - 1, 0)
    return pl.pallas_call(
        functools.partial(_ffn_up_kernel, tiles_per_seq=seq // bm, nj=nj),
        grid=(m // bm, nj + 1),
        in_specs=[pl.BlockSpec((bm, k), lambda i, j: (i, 0)),
                  pl.BlockSpec((k, bn), lambda i, j: (0, cur(j))),
                  pl.BlockSpec((k, bn), lambda i, j: (0, cur(j) + nj)),
                  pl.BlockSpec((SUBLANE, bn), lambda i, j: (0, prv(j))),
                  pl.BlockSpec((SUBLANE, bn), lambda i, j: (0, prv(j) + nj))],
        out_specs=pl.BlockSpec((bm, bn), lambda i, j: (i, prv(j))),
        out_shape=jax.ShapeDtypeStruct((m, f), BF16),
        scratch_shapes=[pltpu.VMEM((2, bm, bn), F32), pltpu.VMEM((2, nj, 2 * SUBLANE, bn), F32)],
        compiler_params=_cparams(("arbitrary", "arbitrary"), est),
        name="ffn_up",
    )(xn, w_up, w_up, conv_w, conv_w)


def _gdn_pre_kernel(x_ref, w_ref, o_ref, *, nq, nk):
    j = pl.program_id(1)
    x = x_ref[...]
    w = w_ref[...]
    kw = 4
    row = lax.broadcasted_iota(jnp.int32, x.shape, 0)
    acc = w[kw - 1:kw] * x
    for s in range(1, kw):
        acc = acc + w[kw - 1 - s:kw - s] * jnp.where(row >= s, pltpu.roll(x, s, 0), 0.0)
    y = jax.nn.silu(acc)
    parts = []
    for c in range(x.shape[1] // GDN_DK):
        yh = y[:, c * GDN_DK:(c + 1) * GDN_DK]
        parts.append(yh * lax.rsqrt(jnp.sum(yh * yh, axis=-1, keepdims=True) + NORM_EPS))
    yn = parts[0] if len(parts) == 1 else jnp.concatenate(parts, axis=1)
    is_q = j < nq
    is_k = jnp.logical_and(j >= nq, j < nq + nk)
    o_ref[...] = jnp.where(is_q, yn * (GDN_DK ** -0.5), jnp.where(is_k, yn, y))


def _gdn_pre(src, conv_w, batch, seq, heads):
    width = conv_w.shape[1]
    cb = GDN_DK
    nq = heads * GDN_DK // cb
    est = 4 * seq * cb * 4 + 8 * seq * cb * 4
    return pl.pallas_call(
        functools.partial(_gdn_pre_kernel, nq=nq, nk=nq),
        grid=(batch, width // cb),
        in_specs=[pl.BlockSpec((seq, cb), lambda b, j: (b, j)),
                  pl.BlockSpec((SUBLANE, cb), lambda b, j: (0, j))],
        out_specs=pl.BlockSpec((seq, cb), lambda b, j: (b, j)),
        out_shape=jax.ShapeDtypeStruct((batch * seq, width), F32),
        compiler_params=_cparams(("parallel", "parallel"), est),
        name="gdn_pre",
    )(src, conv_w)


GDN_ROWS = 256
GDN_HEADS_PER_STEP = 2


def _dot_hi(a, b):
    return jnp.dot(a, b, precision=HI, preferred_element_type=F32)


def _bdot(a, b):
    return jnp.dot(a.astype(BF16), b.astype(BF16), preferred_element_type=F32)


def _bdot_nt(a, b):
    return lax.dot_general(a.astype(BF16), b.astype(BF16), (((1,), (1,)), ((), ())), preferred_element_type=F32)


def _bdot_tn(a, b):
    return lax.dot_general(a.astype(BF16), b.astype(BF16), (((0,), (0,)), ((), ())), preferred_element_type=F32)


def _gdn_head(alog, dtb, q, k, v, a_raw, b_raw, gate, ng, s, masks, rows):
    tril_f, same_f, tril, strict, eye_f = masks
    a_raw = a_raw + dtb
    softplus = jnp.maximum(a_raw, 0.0) + jnp.log(1.0 + jnp.exp(-jnp.abs(a_raw)))
    g = -jnp.exp(jnp.zeros_like(a_raw) + alog) * softplus
    beta = jax.nn.sigmoid(b_raw)
    gb = jnp.broadcast_to(g, (rows, LANE))
    gc = _dot_hi(tril_f, gb)
    gl = _dot_hi(same_f, gb)
    gcol = jnp.concatenate([gc] * (rows // LANE), axis=1) if rows > LANE else gc[:, :rows]
    grow = jnp.transpose(gcol)
    decay = jnp.exp(jnp.where(tril, gcol - grow, MASK_VALUE))
    kb = k * beta
    vb = v * beta
    lmat = jnp.where(strict, _bdot_nt(kb, k) * decay, 0.0)
    tinv = eye_f - lmat
    pw = lmat
    for _ in range(int(math.log2(CHUNK)) - 1):
        pw = _bdot(pw, pw)
        tinv = tinv + _bdot(tinv, pw)
    eg = jnp.exp(gc)
    sol = _bdot(tinv, jnp.concatenate([vb, kb * eg], axis=1))
    u_c, w_c = sol[:, :GDN_DV], sol[:, GDN_DV:]
    intra = _bdot_nt(q, k) * decay
    q_dec = q * eg
    k_dec = k * jnp.exp(gl - gc)
    egl = jnp.exp(gl)
    outs = []
    for c in range(rows // CHUNK):
        lo, hi = c * CHUNK, (c + 1) * CHUNK
        ws = _bdot(jnp.concatenate([w_c[lo:hi], q_dec[lo:hi]], axis=0), s)
        v_new = u_c[lo:hi] - ws[:CHUNK]
        outs.append(ws[CHUNK:] + _bdot(intra[lo:hi, lo:hi], v_new))
        s = s * egl[lo:lo + 1, :] + _bdot_tn(k_dec[lo:hi], v_new)
    o = jnp.concatenate(outs, axis=0)
    o = o * lax.rsqrt(jnp.mean(o * o, axis=-1, keepdims=True) + NORM_EPS) * ng
    return o * jax.nn.silu(gate), s


def _gdn_kernel(alog_ref, dtb_ref, q_ref, k_ref, v_ref, a_ref, b_ref, gate_ref, ng_ref, o_ref, state, *, rows, hps):
    hb = pl.program_id(1)
    r = pl.program_id(2)

    @pl.when(r == 0)
    def _():
        state[...] = jnp.zeros_like(state)

    ri = lax.broadcasted_iota(jnp.int32, (rows, rows), 0)
    ci = lax.broadcasted_iota(jnp.int32, (rows, rows), 1)
    same = (ri // CHUNK) == (ci // CHUNK)
    tril = jnp.logical_and(same, ci <= ri)
    strict = jnp.logical_and(same, ci < ri)
    masks = (tril.astype(F32), same.astype(F32), tril, strict, jnp.where(ri == ci, 1.0, 0.0))
    ng = ng_ref[...]
    for i in range(hps):
        cs = slice(i * GDN_DK, (i + 1) * GDN_DK)
        h = hb * hps + i
        o, s = _gdn_head(alog_ref[h], dtb_ref[h], q_ref[:, cs], k_ref[:, cs], v_ref[:, cs], a_ref[i], b_ref[i],
                         gate_ref[:, cs], ng, state[i], masks, rows)
        state[i] = s
        o_ref[:, cs] = o.astype(o_ref.dtype)


def _gdn_core(qkv, a_col, b_col, gate_src, gate_col0, a_log, dt_bias, norm_g, batch, seq, heads):
    rows = _pick(seq, GDN_ROWS, CHUNK)
    nr = seq // rows
    hps = GDN_HEADS_PER_STEP if heads % GDN_HEADS_PER_STEP == 0 else 1
    nhb = heads // hps
    wblk = hps * GDN_DK
    gcb = gate_col0 // wblk
    est = hps * (12 * rows * LANE * 4 + 16 * rows * rows * 4 + 8 * rows * LANE * 4)
    tok = lambda b, h, r: b * nr + r
    return pl.pallas_call(
        functools.partial(_gdn_kernel, rows=rows, hps=hps),
        grid=(batch, nhb, nr),
        in_specs=[pl.BlockSpec(memory_space=pltpu.SMEM),
                  pl.BlockSpec(memory_space=pltpu.SMEM),
                  pl.BlockSpec((rows, wblk), lambda b, h, r: (tok(b, h, r), h)),
                  pl.BlockSpec((rows, wblk), lambda b, h, r: (tok(b, h, r), nhb + h)),
                  pl.BlockSpec((rows, wblk), lambda b, h, r: (tok(b, h, r), 2 * nhb + h)),
                  pl.BlockSpec((None, hps, rows, 1), lambda b, h, r: (b, h, r, 0)),
                  pl.BlockSpec((None, hps, rows, 1), lambda b, h, r: (b, h, r, 0)),
                  pl.BlockSpec((rows, wblk), lambda b, h, r: (tok(b, h, r), gcb + h)),
                  pl.BlockSpec((1, GDN_DV), lambda b, h, r: (0, 0))],
        out_specs=pl.BlockSpec((rows, wblk), lambda b, h, r: (tok(b, h, r), h)),
        out_shape=jax.ShapeDtypeStruct((batch * seq, heads * GDN_DV), BF16),
        scratch_shapes=[pltpu.VMEM((hps, GDN_DK, GDN_DV), F32)],
        compiler_params=_cparams(("parallel", "parallel", "arbitrary"), est),
        name="gdn_core",
    )(a_log.astype(F32), dt_bias.astype(F32), qkv, qkv, qkv, a_col, b_col, gate_src,
      norm_g.reshape(1, GDN_DV).astype(F32))


SB_TQ = 512
SB_TK = 256


def _sb_kernel(q_ref, kt_ref, v_ref, o_ref, acc_s, c_s, *, tq, tk, scale):
    i = pl.program_id(2)
    r = tq // tk
    q = q_ref[...]
    ur = lax.broadcasted_iota(jnp.int32, (tk, tk), 0)
    uc = lax.broadcasted_iota(jnp.int32, (tk, tk), 1)
    upper = jnp.where(ur > uc, 1.0, 0.0).astype(BF16)
    ri = lax.broadcasted_iota(jnp.int32, (tq, tk), 0)
    ci = lax.broadcasted_iota(jnp.int32, (tq, tk), 1)
    acc_s[...] = jnp.zeros_like(acc_s)
    c_s[...] = jnp.zeros_like(c_s)

    def logits(j):
        z = jnp.dot(q, kt_ref[j], preferred_element_type=F32) * scale
        return z, -(jnp.maximum(z, 0.0) + jnp.log(1.0 + jnp.exp(-jnp.abs(z))))

    def weights(z, lm, c):
        hi = lm.astype(BF16)
        lo = (lm - hi.astype(F32)).astype(BF16)
        rest = (jnp.dot(hi, upper, preferred_element_type=F32)
                + jnp.dot(lo, upper, preferred_element_type=F32))
        return jnp.exp(z + lm + rest + c)

    def values(j):
        return v_ref[pl.ds(pl.multiple_of(j * tk, tk), tk), :]

    for d in range(r - 1, -1, -1):
        j = i * r + d
        before = (ci + d * tk) < ri
        z, lm = logits(j)
        lm = jnp.where(before, lm, 0.0)
        w = jnp.where(before, weights(z, lm, c_s[...]), 0.0)
        acc_s[...] += jnp.dot(w.astype(BF16), values(j), preferred_element_type=F32)
        c_s[...] += jnp.sum(lm, axis=1, keepdims=True)

    def single(j):
        z, lm = logits(j)
        w = weights(z, lm, c_s[...])
        acc_s[...] += jnp.dot(w.astype(BF16), values(j), preferred_element_type=F32)
        c_s[...] += jnp.sum(lm, axis=1, keepdims=True)

    def pair(p, carry):
        ja = i * r - 1 - 2 * p
        jb = ja - 1
        za, lma = logits(ja)
        zb, lmb = logits(jb)
        c = c_s[...]
        ta = jnp.sum(lma, axis=1, keepdims=True)
        wa = weights(za, lma, c)
        wb = weights(zb, lmb, c + ta)
        acc_s[...] += (jnp.dot(wa.astype(BF16), values(ja), preferred_element_type=F32)
                       + jnp.dot(wb.astype(BF16), values(jb), preferred_element_type=F32))
        c_s[...] = c + ta + jnp.sum(lmb, axis=1, keepdims=True)
        return carry

    n_full = i * r
    lax.fori_loop(0, n_full // 2, pair, 0)
    if r % 2 == 1:
        @pl.when(n_full % 2 == 1)
        def _():
            single(0)
    o_ref[...] = acc_s[...].astype(o_ref.dtype)


def _sb_attention(qkv, batch, seq, heads):
    tq = _pick(seq, SB_TQ, LANE)
    tk = _pick(tq, SB_TK, LANE)
    nq, nk = seq // tq, seq // tk
    scale = SB_D ** -0.5
    kt = jnp.transpose(qkv.reshape(batch, nk, tk, 3, heads, SB_D)[:, :, :, 1], (0, 3, 1, 4, 2))
    est = 4 * tq * SB_D * 2 + 4 * seq * SB_D * 2 + 2 * tq * SB_D * 4 + 24 * tq * tk * 4
    return pl.pallas_call(
        functools.partial(_sb_kernel, tq=tq, tk=tk, scale=scale),
        grid=(batch, heads, nq),
        in_specs=[pl.BlockSpec((tq, SB_D), lambda b, h, i: (b * nq + i, h)),
                  pl.BlockSpec((None, None, nk, SB_D, tk), lambda b, h, i: (b, h, 0, 0, 0)),
                  pl.BlockSpec((seq, SB_D), lambda b, h, i: (b, 2 * heads + h))],
        out_specs=pl.BlockSpec((tq, SB_D), lambda b, h, i: (b * nq + i, h)),
        out_shape=jax.ShapeDtypeStruct((batch * seq, heads * SB_D), BF16),
        scratch_shapes=[pltpu.VMEM((tq, SB_D), F32), pltpu.VMEM((tq, 1), F32)],
        compiler_params=_cparams(("parallel", "parallel", "arbitrary"), est),
        name="sb_attention",
    )(qkv, kt, qkv)


def _pad_cols(w, n):
    return jnp.pad(w, ((0, 0), (0, n - w.shape[1])))


def _ffn(h, ln_g, w_up, conv_w, w_down, seq):
    xn = _rmsnorm(h, ln_g, BF16)
    conv8 = jnp.pad(conv_w.astype(F32), ((0, SUBLANE - conv_w.shape[0]), (0, 0)))
    act = _ffn_up(xn, w_up.astype(BF16), conv8, seq)
    return _matmul([(act, w_down.astype(BF16))], F32, residual=h)


def _even_layer(h, ln_g, w_in, w_out, q_norm, kv_norm, w_uq, w_ukv,
                a_re, a_im, log_step, b_re, b_im, c_re, c_im, d_skip, w_glu, batch, seq):
    d = h.shape[1]
    q_rank, kv_rank = q_norm.shape[0], kv_norm.shape[0]
    heads = w_uq.shape[1] // (MLA_NOPE + MLA_ROPE)
    s5_width = d_skip.shape[0]
    o_kr = q_rank + kv_rank
    o_s5 = o_kr + MLA_ROPE
    latent_w = -(-(o_kr + LANE) // s5_width) * s5_width if s5_width >= LANE else o_kr + LANE
    w_lat = _pad_cols(w_in[:, :o_s5], latent_w)
    w_cat = jnp.concatenate([w_lat, w_in[:, o_s5:]], axis=1).astype(BF16)
    hn = _rmsnorm(h, ln_g, BF16)
    proj = _matmul([(hn, w_cat)], F32)
    cq = _rmsnorm(proj, q_norm, BF16, col0=0, width=q_rank)
    ckv = _rmsnorm(proj, kv_norm, BF16, col0=q_rank, width=kv_rank)
    w_uq_p = jnp.pad(w_uq.reshape(q_rank, heads, MLA_NOPE + MLA_ROPE),
                     ((0, 0), (0, 0), (0, MLA_QK - MLA_NOPE - MLA_ROPE))).reshape(q_rank, heads * MLA_QK)
    q = _matmul([(cq, w_uq_p.astype(BF16))], F32)
    kv = _matmul([(ckv, w_ukv.astype(BF16))], BF16)
    cosf, sinf = _rope_tables(seq)
    kr = _rope_k(proj, o_kr, cosf, sinf, seq)
    out_a = _mla_attention(q, kv, kr, cosf, sinf, batch, seq, heads)
    params = _s5_params(a_re, a_im, log_step, b_re, b_im, c_re, c_im)
    g = _s5_scan(proj, latent_w, params, d_skip, batch, seq)
    out_b = _s5_glu(g, w_glu.astype(BF16))
    na = heads * MLA_V
    return _matmul([(out_a, w_out[:na].astype(BF16)), (out_b, w_out[na:].astype(BF16))], F32, residual=h)


def _odd_layer(h, ln_g, w_in, w_out, conv_w, a_log, dt_bias, norm_g, batch, seq):
    heads = a_log.shape[0]
    qkv_w = heads * (2 * GDN_DK + GDN_DV)
    o2 = qkv_w + 2 * heads
    o3 = o2 + heads * GDN_DV
    hn = _rmsnorm(h, ln_g, BF16)
    w_main = jnp.concatenate([w_in[:, :qkv_w], w_in[:, o2:o3]], axis=1).astype(BF16)
    w_ab = _pad_cols(w_in[:, qkv_w:o2], LANE).astype(BF16)
    w_sb = w_in[:, o3:].astype(BF16)
    main = _matmul([(hn, w_main)], F32)
    ab = _matmul([(hn, w_ab)], F32)
    sb = _matmul([(hn, w_sb)], BF16)
    conv8 = jnp.pad(conv_w.astype(F32), ((0, SUBLANE - conv_w.shape[0]), (0, 0)))
    qkv = _gdn_pre(main, conv8, batch, seq, heads)
    ab = ab.reshape(batch, seq, LANE)
    a_col = jnp.swapaxes(ab[:, :, :heads], 1, 2)[..., None]
    b_col = jnp.swapaxes(ab[:, :, heads:2 * heads], 1, 2)[..., None]
    out_c = _gdn_core(qkv, a_col, b_col, main, qkv_w, a_log, dt_bias, norm_g, batch, seq, heads)
    out_d = _sb_attention(sb, batch, seq, sb.shape[1] // (3 * SB_D))
    nc = heads * GDN_DV
    return _matmul([(out_c, w_out[:nc].astype(BF16)), (out_d, w_out[nc:].astype(BF16))], F32, residual=h)


def kernel(x, ln_mix, ln_ffn, ln_final, ffn_w_up, ffn_conv, ffn_w_down, ev_w_in, ev_w_out, mla_q_norm,
           mla_kv_norm, mla_w_uq, mla_w_ukv, s5_a_re, s5_a_im, s5_log_step, s5_b_re, s5_b_im, s5_c_re,
           s5_c_im, s5_d, s5_w_glu, od_w_in, od_w_out, gdn_conv, gdn_a_log, gdn_dt_bias, gdn_norm):
    batch, seq, d = x.shape
    h = x.reshape(batch * seq, d).astype(F32)
    for layer in range(ln_mix.shape[0]):
        i = layer // 2
        if layer % 2 == 0:
            h = _even_layer(h, ln_mix[layer], ev_w_in[i], ev_w_out[i], mla_q_norm[i], mla_kv_norm[i],
                            mla_w_uq[i], mla_w_ukv[i], s5_a_re[i], s5_a_im[i], s5_log_step[i], s5_b_re[i],
                            s5_b_im[i], s5_c_re[i], s5_c_im[i], s5_d[i], s5_w_glu[i], batch, seq)
        else:
            h = _odd_layer(h, ln_mix[layer], od_w_in[i], od_w_out[i], gdn_conv[i], gdn_a_log[i],
                           gdn_dt_bias[i], gdn_norm[i], batch, seq)
        h = _ffn(h, ln_ffn[layer], ffn_w_up[layer], ffn_conv[layer], ffn_w_down[layer], seq)
    return _rmsnorm(h, ln_final, x.dtype).reshape(batch, seq, d)
```

```python
import functools
import math

import jax
import jax.numpy as jnp
from jax import lax
from jax.experimental import pallas as pl
from jax.experimental.pallas import tpu as pltpu

F32 = jnp.float32
BF16 = jnp.bfloat16
HI = lax.Precision.HIGHEST

CHUNK = 64
NORM_EPS = 1e-6
MLA_NOPE = 128
MLA_ROPE = 64
MLA_V = 128
MLA_QK = 256
ROPE_BASE = 10000.0
S5_GROUP = 16
S5_STATE = 64
GDN_DK = 128
GDN_DV = 128
SB_D = 128

LANE = 128
SUBLANE = 8
VMEM_LIMIT_MAX = 58 * 1024 * 1024
VMEM_LIMIT_MIN = 32 * 1024 * 1024
MASK_VALUE = -1e30
LOG2E = math.log2(math.e)


def _cparams(sem, vmem_est):
    limit = int(min(max(vmem_est * 5 // 4 + (4 << 20), VMEM_LIMIT_MIN), VMEM_LIMIT_MAX))
    return pltpu.CompilerParams(dimension_semantics=sem, vmem_limit_bytes=limit)


def _pick(n, pref, mult):
    if n <= pref:
        return n
    t = (pref // mult) * mult
    while t >= mult:
        if n % t == 0:
            return t
        t -= mult
    return n


def _rmsnorm_kernel(x_ref, g_ref, o_ref):
    x = x_ref[...].astype(F32)
    ms = jnp.mean(x * x, axis=-1, keepdims=True)
    o_ref[...] = (x * lax.rsqrt(ms + NORM_EPS) * g_ref[...]).astype(o_ref.dtype)


def _rmsnorm(x, g, out_dtype, *, col0=0, width=None):
    m = x.shape[0]
    width = x.shape[1] if width is None else width
    tm = _pick(m, 256, SUBLANE)
    cb = col0 // width
    est = 2 * tm * width * (x.dtype.itemsize + jnp.dtype(out_dtype).itemsize) + 4 * tm * width * 4
    return pl.pallas_call(
        _rmsnorm_kernel,
        grid=(m // tm,),
        in_specs=[pl.BlockSpec((tm, width), lambda i: (i, cb)),
                  pl.BlockSpec((1, width), lambda i: (0, 0))],
        out_specs=pl.BlockSpec((tm, width), lambda i: (i, 0)),
        out_shape=jax.ShapeDtypeStruct((m, width), out_dtype),
        compiler_params=_cparams(("parallel",), est),
        name="rmsnorm",
    )(x, g.reshape(1, width).astype(F32))


def _mm_kernel(*refs, n_pairs, has_res):
    o_ref = refs[-1]
    acc = None
    for p in range(n_pairs):
        a = refs[2 * p][...].astype(BF16)
        d = jnp.dot(a, refs[2 * p + 1][...], preferred_element_type=F32)
        acc = d if acc is None else acc + d
    if has_res:
        acc = acc + refs[2 * n_pairs][...]
    o_ref[...] = acc.astype(o_ref.dtype)


def _mm_tiles(m, ks, n, a_bytes, out_bytes, has_res, budget=40 << 20):
    for bm_p, bn_p in ((1024, 1024), (1024, 512), (512, 512), (512, 256), (256, 256), (256, 128), (128, 128)):
        bm = _pick(m, bm_p, SUBLANE)
        bn = _pick(n, bn_p, LANE)
        est = bm * bn * 4
        for k, ab in zip(ks, a_bytes):
            est += 2 * (bm * k * ab + k * bn * 2)
        est += 2 * bm * bn * (out_bytes + (4 if has_res else 0))
        if est <= budget:
            return bm, bn, est
    return bm, bn, est


def _matmul(pairs, out_dtype, residual=None, *, a_col0=None, b_layer=None):
    m = pairs[0][0].shape[0]
    n = pairs[0][1].shape[-1]
    ks = [b.shape[-2] for _, b in pairs]
    a_col0 = [0] * len(pairs) if a_col0 is None else a_col0
    a_bytes = [a.dtype.itemsize for a, _ in pairs]
    bm, bn, est = _mm_tiles(m, ks, n, a_bytes, jnp.dtype(out_dtype).itemsize, residual is not None)
    in_specs, args = [], []
    for (a, b), k, c0 in zip(pairs, ks, a_col0):
        cb = c0 // k
        in_specs.append(pl.BlockSpec((bm, k), lambda i, j, cb=cb: (i, cb)))
        if b_layer is None:
            in_specs.append(pl.BlockSpec((k, bn), lambda i, j: (0, j)))
        else:
            in_specs.append(pl.BlockSpec((None, k, bn), lambda i, j: (b_layer, 0, j)))
        args += [a, b]
    if residual is not None:
        in_specs.append(pl.BlockSpec((bm, bn), lambda i, j: (i, j)))
        args.append(residual)
    return pl.pallas_call(
        functools.partial(_mm_kernel, n_pairs=len(pairs), has_res=residual is not None),
        grid=(m // bm, n // bn),
        in_specs=in_specs,
        out_specs=pl.BlockSpec((bm, bn), lambda i, j: (i, j)),
        out_shape=jax.ShapeDtypeStruct((m, n), out_dtype),
        compiler_params=_cparams(("parallel", "parallel"), est),
        name="matmul",
    )(*args)


def _rope_tables(seq):
    half = MLA_ROPE // 2
    inv_freq = ROPE_BASE ** (-jnp.arange(half, dtype=F32) / half)
    ang = jnp.arange(seq, dtype=F32)[:, None] * inv_freq[None, :]
    cos, sin = jnp.cos(ang), jnp.sin(ang)
    zeros = jnp.zeros((seq, LANE - MLA_ROPE), F32)
    return (jnp.concatenate([cos, cos, zeros], axis=1),
            jnp.concatenate([-sin, sin, zeros], axis=1))


def _rope_lanes(x, cosf, sinf):
    half = MLA_ROPE // 2
    lane = lax.broadcasted_iota(jnp.int32, x.shape, 1)
    swapped = jnp.where(lane < half, pltpu.roll(x, LANE - half, 1), pltpu.roll(x, half, 1))
    return x * cosf + swapped * sinf


def _rope_kernel(x_ref, cos_ref, sin_ref, o_ref):
    o_ref[...] = _rope_lanes(x_ref[...], cos_ref[...], sin_ref[...]).astype(o_ref.dtype)


def _rope_k(x, col0, cosf, sinf, seq):
    m = x.shape[0]
    tm = _pick(seq, 512, SUBLANE)
    per_seq = seq // tm
    cb = col0 // LANE
    return pl.pallas_call(
        _rope_kernel,
        grid=(m // tm,),
        in_specs=[pl.BlockSpec((tm, LANE), lambda i: (i, cb)),
                  pl.BlockSpec((tm, LANE), lambda i: (i % per_seq, 0)),
                  pl.BlockSpec((tm, LANE), lambda i: (i % per_seq, 0))],
        out_specs=pl.BlockSpec((tm, LANE), lambda i: (i, 0)),
        out_shape=jax.ShapeDtypeStruct((m, LANE), BF16),
        compiler_params=_cparams(("parallel",), 16 * tm * LANE * 4),
        name="rope_k",
    )(x, cosf, sinf)


def _mla_kernel(q_ref, kt_ref, v_ref, cos_ref, sin_ref, o_ref, qs, m_s, l_s, acc_s, *, tq):
    i = pl.program_id(2)
    q = q_ref[...]
    qs[:, :MLA_NOPE] = q[:, :MLA_NOPE].astype(BF16)
    qs[:, MLA_NOPE:] = _rope_lanes(q[:, MLA_NOPE:], cos_ref[...], sin_ref[...]).astype(BF16)

    def tile(j):
        s = jnp.dot(qs[...], kt_ref[j], preferred_element_type=F32)
        return s, v_ref[pl.ds(pl.multiple_of(j * tq, tq), tq), :]

    s, v = tile(i)
    rc = lax.broadcasted_iota(jnp.int32, (tq, tq), 0) // CHUNK
    cc = lax.broadcasted_iota(jnp.int32, (tq, tq), 1) // CHUNK
    s = jnp.where(cc <= rc, s, MASK_VALUE)
    m = jnp.max(s, axis=1, keepdims=True)
    p = jnp.exp2(s - m)
    m_s[...] = m
    l_s[...] = jnp.sum(p, axis=1, keepdims=True)
    acc_s[...] = jnp.dot(p.astype(BF16), v, preferred_element_type=F32)

    def body(j, carry):
        s, v = tile(j)
        m_prev = m_s[...]
        m_new = jnp.maximum(m_prev, jnp.max(s, axis=1, keepdims=True))
        alpha = jnp.exp2(m_prev - m_new)
        p = jnp.exp2(s - m_new)
        l_s[...] = alpha * l_s[...] + jnp.sum(p, axis=1, keepdims=True)
        acc_s[...] = alpha * acc_s[...] + jnp.dot(p.astype(BF16), v, preferred_element_type=F32)
        m_s[...] = m_new
        return carry

    lax.fori_loop(0, i, body, 0)
    o_ref[...] = (acc_s[...] / l_s[...]).astype(o_ref.dtype)


MLA_TILE = 1024


def _mla_attention(q, kv, kr, cosf, sinf, batch, seq, heads):
    tq = _pick(seq, MLA_TILE, CHUNK)
    nq = seq // tq
    kn_t = jnp.transpose(kv.reshape(batch, nq, tq, heads, MLA_QK)[..., :MLA_NOPE], (0, 3, 1, 4, 2))
    kr_t = jnp.transpose(kr.reshape(batch, nq, tq, LANE), (0, 1, 3, 2))
    kt = jnp.concatenate([kn_t, jnp.broadcast_to(kr_t[:, None], (batch, heads, nq, LANE, tq))], axis=3)
    est = (2 * tq * MLA_QK * 4 + 2 * seq * (MLA_QK + LANE) * 2 + 4 * tq * LANE * 4 + 2 * tq * LANE * 2
           + tq * MLA_QK * 2 + 3 * tq * LANE * 4 + 6 * tq * tq * 4)
    return pl.pallas_call(
        functools.partial(_mla_kernel, tq=tq),
        grid=(batch, heads, nq),
        in_specs=[pl.BlockSpec((tq, MLA_QK), lambda b, h, i: (b * nq + i, h)),
                  pl.BlockSpec((None, None, nq, MLA_QK, tq), lambda b, h, i: (b, h, 0, 0, 0)),
                  pl.BlockSpec((seq, MLA_V), lambda b, h, i: (b, 2 * h + 1)),
                  pl.BlockSpec((tq, LANE), lambda b, h, i: (i, 0)),
                  pl.BlockSpec((tq, LANE), lambda b, h, i: (i, 0))],
        out_specs=pl.BlockSpec((tq, MLA_V), lambda b, h, i: (b * nq + i, h)),
        out_shape=jax.ShapeDtypeStruct((batch * seq, heads * MLA_V), BF16),
        scratch_shapes=[pltpu.VMEM((tq, MLA_QK), BF16), pltpu.VMEM((tq, 1), F32),
                        pltpu.VMEM((tq, 1), F32), pltpu.VMEM((tq, MLA_V), F32)],
        compiler_params=_cparams(("parallel", "parallel", "arbitrary"), est),
        name="mla_attention",
    )(q, kt, kv, cosf, sinf)


S5_GROUPS_PER_BLOCK = 8


def _s5_kernel(u_ref, bre_ref, bim_ref, cre_ref, cim_ref, pw_ref, d_ref, o_ref, sre, sim, car, *, tt):
    t = pl.program_id(2)

    @pl.when(t == 0)
    def _():
        car[...] = jnp.zeros_like(car)

    u = u_ref[...]
    ub = u.astype(BF16)
    sre[...] = jnp.dot(ub, bre_ref[0], preferred_element_type=F32)
    sim[...] = jnp.dot(ub, bim_ref[0], preferred_element_type=F32)
    ns = sre.shape[1]
    row = lax.broadcasted_iota(jnp.int32, (SUBLANE, ns), 0)
    steps = [(1, pw_ref[0, 0], pw_ref[0, 1]), (2, pw_ref[0, 2], pw_ref[0, 3]), (4, pw_ref[0, 4], pw_ref[0, 5])]
    pwr, pwi = pw_ref[0, 6], pw_ref[0, 7]

    def body(r, carry):
        cr, ci = carry
        off = pl.multiple_of(r * SUBLANE, SUBLANE)
        xr = sre[pl.ds(off, SUBLANE), :]
        xi = sim[pl.ds(off, SUBLANE), :]
        for k, ar, ai in steps:
            sr = jnp.where(row >= k, pltpu.roll(xr, k, 0), 0.0)
            si = jnp.where(row >= k, pltpu.roll(xi, k, 0), 0.0)
            xr, xi = xr + ar * sr - ai * si, xi + ar * si + ai * sr
        xr, xi = xr + pwr * cr - pwi * ci, xi + pwr * ci + pwi * cr
        sre[pl.ds(off, SUBLANE), :] = xr
        sim[pl.ds(off, SUBLANE), :] = xi
        return xr[SUBLANE - 1:, :], xi[SUBLANE - 1:, :]

    cr, ci = lax.fori_loop(0, tt // SUBLANE, body, (car[0:1, :], car[1:2, :]))
    car[0:1, :] = cr
    car[1:2, :] = ci
    y = (jnp.dot(sre[...].astype(BF16), cre_ref[0], preferred_element_type=F32)
         - jnp.dot(sim[...].astype(BF16), cim_ref[0], preferred_element_type=F32))
    y = y + d_ref[...] * u
    o_ref[...] = jax.nn.gelu(y, approximate=True)


def _s5_params(a_re, a_im, log_step, b_re, b_im, c_re, c_im):
    g, p = a_re.shape
    gb = min(S5_GROUPS_PER_BLOCK, g)
    nb = g // gb
    lam = lax.complex(a_re.astype(F32), a_im.astype(F32))
    step = jnp.exp(log_step.astype(F32))[:, None]
    lam_dt = lam * step
    lam_bar = jnp.exp(lam_dt)
    b_bar = ((lam_bar - 1.0) / lam)[..., None] * lax.complex(b_re.astype(F32), b_im.astype(F32))
    c_c = lax.complex(c_re.astype(F32), c_im.astype(F32))
    eye = jnp.eye(gb, dtype=F32)

    def bd_in(x):
        x = x.reshape(nb, gb, p, S5_GROUP)
        return jnp.einsum('ngpi,gh->ngihp', x, eye).reshape(nb, gb * S5_GROUP, gb * p)

    def bd_out(x):
        x = x.reshape(nb, gb, S5_GROUP, p)
        return jnp.einsum('ngip,gh->ngphi', x, eye).reshape(nb, gb * p, gb * S5_GROUP)

    def rows(z):
        z = z.reshape(nb, 1, gb * p)
        return (jnp.broadcast_to(jnp.real(z), (nb, SUBLANE, gb * p)),
                jnp.broadcast_to(jnp.imag(z), (nb, SUBLANE, gb * p)))

    planes = []
    for k in (1, 2, 4):
        planes += list(rows(jnp.exp(lam_dt * float(k))))
    pw = jnp.exp(lam_dt[None] * jnp.arange(1, SUBLANE + 1, dtype=F32)[:, None, None])
    pw = jnp.moveaxis(pw.reshape(SUBLANE, nb, gb * p), 0, 1)
    planes += [jnp.real(pw), jnp.imag(pw)]
    return (bd_in(jnp.real(b_bar)).astype(BF16), bd_in(jnp.imag(b_bar)).astype(BF16),
            bd_out(jnp.real(c_c)).astype(BF16), bd_out(jnp.imag(c_c)).astype(BF16),
            jnp.stack(planes, axis=1), gb)


def _s5_scan(u_src, col0, params, d_skip, batch, seq):
    bre, bim, cre, cim, pw, gb = params
    nb = bre.shape[0]
    gc = gb * S5_GROUP
    ns = gb * S5_STATE
    tt = _pick(seq, 512, SUBLANE)
    nt = seq // tt
    cb0 = col0 // gc
    est = 4 * tt * gc * 4 + 8 * gc * ns * 4 + 4 * SUBLANE * 8 * ns * 4 + 6 * tt * ns * 4
    return pl.pallas_call(
        functools.partial(_s5_kernel, tt=tt),
        grid=(batch, nb, nt),
        in_specs=[pl.BlockSpec((tt, gc), lambda b, g, t: (b * nt + t, cb0 + g)),
                  pl.BlockSpec((1, gc, ns), lambda b, g, t: (g, 0, 0)),
                  pl.BlockSpec((1, gc, ns), lambda b, g, t: (g, 0, 0)),
                  pl.BlockSpec((1, ns, gc), lambda b, g, t: (g, 0, 0)),
                  pl.BlockSpec((1, ns, gc), lambda b, g, t: (g, 0, 0)),
                  pl.BlockSpec((1, 8, SUBLANE, ns), lambda b, g, t: (g, 0, 0, 0)),
                  pl.BlockSpec((1, gc), lambda b, g, t: (0, g))],
        out_specs=pl.BlockSpec((tt, gc), lambda b, g, t: (b * nt + t, g)),
        out_shape=jax.ShapeDtypeStruct((batch * seq, nb * gc), F32),
        scratch_shapes=[pltpu.VMEM((tt, ns), F32), pltpu.VMEM((tt, ns), F32), pltpu.VMEM((SUBLANE, ns), F32)],
        compiler_params=_cparams(("parallel", "parallel", "arbitrary"), est),
        name="s5_scan",
    )(u_src, bre, bim, cre, cim, pw, d_skip.reshape(1, -1).astype(F32))


S5_BLOCK = 16
S5_GROUPS_PER_STEP = 4


def _s5t_kernel(u_ref, toep_ref, ere_ref, eim_ref, fre_ref, fim_ref, coef_ref, d_ref, o_ref,
                er_s, ei_s, hr_s, hi_s, *, gps, nb, nb_seq):
    tw = S5_BLOCK * S5_GROUP
    npair = gps // 2
    for p in range(npair):
        up = jnp.concatenate([u_ref[2 * p].astype(BF16), u_ref[2 * p + 1].astype(BF16)], axis=1)
        er_s[p] = jnp.dot(up, ere_ref[p], preferred_element_type=F32)
        ei_s[p] = jnp.dot(up, eim_ref[p], preferred_element_type=F32)
    row = lax.broadcasted_iota(jnp.int32, (SUBLANE, LANE), 0)
    groups_per_seq = nb_seq // SUBLANE

    def body(r, carry):
        off = pl.multiple_of(r * SUBLANE, SUBLANE)
        keep = jnp.where((r % groups_per_seq) == 0, 0.0, 1.0)
        new = []
        for p in range(npair):
            xr = er_s[p, pl.ds(off, SUBLANE), :]
            xi = ei_s[p, pl.ds(off, SUBLANE), :]
            for idx, k in enumerate((1, 2, 4)):
                ar, ai = coef_ref[p, 2 * idx], coef_ref[p, 2 * idx + 1]
                sr = jnp.where(row >= k, pltpu.roll(xr, k, 0), 0.0)
                si = jnp.where(row >= k, pltpu.roll(xi, k, 0), 0.0)
                xr, xi = xr + ar * sr - ai * si, xi + ar * si + ai * sr
            cr = jnp.broadcast_to(carry[p][0] * keep, (SUBLANE, LANE))
            ci = jnp.broadcast_to(carry[p][1] * keep, (SUBLANE, LANE))
            pwr, pwi = coef_ref[p, 6], coef_ref[p, 7]
            xr, xi = xr + pwr * cr - pwi * ci, xi + pwr * ci + pwi * cr
            hr_s[p, pl.ds(off, SUBLANE), :] = jnp.where(row == 0, cr, pltpu.roll(xr, 1, 0))
            hi_s[p, pl.ds(off, SUBLANE), :] = jnp.where(row == 0, ci, pltpu.roll(xi, 1, 0))
            new.append((xr[SUBLANE - 1:, :], xi[SUBLANE - 1:, :]))
        return tuple(new)

    zero = jnp.zeros((1, LANE), F32)
    lax.fori_loop(0, nb // SUBLANE, body, tuple((zero, zero) for _ in range(npair)))
    for p in range(npair):
        yc = (jnp.dot(hr_s[p].astype(BF16), fre_ref[p], preferred_element_type=F32)
              + jnp.dot(hi_s[p].astype(BF16), fim_ref[p], preferred_element_type=F32))
        for a in range(2):
            g = 2 * p + a
            u = u_ref[g]
            y = jnp.dot(u.astype(BF16), toep_ref[g], preferred_element_type=F32) + yc[:, a * tw:(a + 1) * tw]
            o_ref[g] = jax.nn.gelu(y + d_ref[g] * u, approximate=True)


def _s5t_params(a_re, a_im, log_step, b_re, b_im, c_re, c_im, d_skip):
    g, p = a_re.shape
    t = S5_BLOCK
    lam = lax.complex(a_re.astype(F32), a_im.astype(F32))
    lam_dt = lam * jnp.exp(log_step.astype(F32))[:, None]
    lam_bar = jnp.exp(lam_dt)
    b_bar = ((lam_bar - 1.0) / lam)[..., None] * lax.complex(b_re.astype(F32), b_im.astype(F32))
    c_c = lax.complex(c_re.astype(F32), c_im.astype(F32))
    tau = jnp.arange(t, dtype=F32)
    apow = jnp.exp(lam_dt[None] * tau[:, None, None])
    apow1 = jnp.exp(lam_dt[None] * (tau + 1.0)[:, None, None])
    kern = jnp.real(jnp.einsum('gip,tgp,gpj->gtij', c_c, apow, b_bar, precision=HI))
    lag = jnp.arange(t)[None, :] - jnp.arange(t)[:, None]
    kst = jnp.where((lag >= 0)[None, :, :, None, None], kern[:, jnp.clip(lag, 0, t - 1)], 0.0)
    toep = jnp.transpose(kst, (0, 1, 4, 2, 3)).reshape(g, t * S5_GROUP, t * S5_GROUP)
    e_c = (apow[::-1][:, :, None, :] * jnp.moveaxis(b_bar, 2, 1)[None]).transpose(1, 0, 2, 3)
    e_c = e_c.reshape(g, t * S5_GROUP, p)
    f_c = (c_c[:, None] * apow1.transpose(1, 0, 2)[:, :, None, :])
    f_c = jnp.moveaxis(f_c.reshape(g, t * S5_GROUP, p), 1, 2)
    eye2 = jnp.eye(2, dtype=F32)

    def pair_in(x):
        return jnp.einsum('narp,ab->narbp', x.reshape(g // 2, 2, t * S5_GROUP, p), eye2).reshape(
            g // 2, 2 * t * S5_GROUP, 2 * p)

    def pair_out(x):
        return jnp.einsum('napc,ab->napbc', x.reshape(g // 2, 2, p, t * S5_GROUP), eye2).reshape(
            g // 2, 2 * p, 2 * t * S5_GROUP)

    def planes(z):
        z = jnp.broadcast_to(z, (g, SUBLANE, p)).reshape(g // 2, 2, SUBLANE, p).transpose(0, 2, 1, 3)
        z = z.reshape(g // 2, SUBLANE, 2 * p)
        return [jnp.real(z), jnp.imag(z)]

    coef = []
    for k in (1, 2, 4):
        coef += planes(jnp.exp(lam_dt * float(t * k))[:, None, :])
    coef += planes(jnp.exp(lam_dt[:, None, :] * (float(t) * jnp.arange(1, SUBLANE + 1, dtype=F32))[None, :, None]))
    dtile = jnp.tile(d_skip.astype(F32).reshape(g, 1, S5_GROUP), (1, 1, t))
    return (toep.astype(BF16), pair_in(jnp.real(e_c)).astype(BF16), pair_in(jnp.imag(e_c)).astype(BF16),
            pair_out(jnp.real(f_c)).astype(BF16), pair_out(-jnp.imag(f_c)).astype(BF16),
            jnp.stack(coef, axis=1), dtile)


def _s5_toeplitz(u, params, batch, seq):
    toep, ere, eim, fre, fim, coef, dtile = params
    g = toep.shape[0]
    t = S5_BLOCK
    tw = t * S5_GROUP
    ns = 2 * S5_STATE
    m = u.shape[0]
    nb = m // t
    gps = S5_GROUPS_PER_STEP if g % S5_GROUPS_PER_STEP == 0 else 2
    npair = gps // 2
    u_blk = u.reshape(nb, t, g, S5_GROUP).transpose(2, 0, 1, 3).reshape(g, nb, tw)
    est = 4 * gps * nb * tw * 4 + 4 * npair * nb * ns * 4 + 16 * gps * tw * tw * 2 + 12 * nb * tw * 4
    out = pl.pallas_call(
        functools.partial(_s5t_kernel, gps=gps, nb=nb, nb_seq=seq // t),
        grid=(g // gps,),
        in_specs=[pl.BlockSpec((gps, nb, tw), lambda i: (i, 0, 0)),
                  pl.BlockSpec((gps, tw, tw), lambda i: (i, 0, 0)),
                  pl.BlockSpec((npair, 2 * tw, ns), lambda i: (i, 0, 0)),
                  pl.BlockSpec((npair, 2 * tw, ns), lambda i: (i, 0, 0)),
                  pl.BlockSpec((npair, ns, 2 * tw), lambda i: (i, 0, 0)),
                  pl.BlockSpec((npair, ns, 2 * tw), lambda i: (i, 0, 0)),
                  pl.BlockSpec((npair, 8, SUBLANE, ns), lambda i: (i, 0, 0, 0)),
                  pl.BlockSpec((gps, 1, tw), lambda i: (i, 0, 0))],
        out_specs=pl.BlockSpec((gps, nb, tw), lambda i: (i, 0, 0)),
        out_shape=jax.ShapeDtypeStruct((g, nb, tw), F32),
        scratch_shapes=[pltpu.VMEM((npair, nb, ns), F32) for _ in range(4)],
        compiler_params=_cparams(("parallel",), est),
        name="s5_blocks",
    )(u_blk, toep, ere, eim, fre, fim, coef, dtile)
    return out.reshape(g, nb, t, S5_GROUP).transpose(1, 2, 0, 3).reshape(m, g * S5_GROUP)


def _glu_kernel(a_ref, w_ref, g_ref, o_ref):
    z = jnp.dot(a_ref[...].astype(BF16), w_ref[...], preferred_element_type=F32)
    o_ref[...] = (g_ref[...] * jax.nn.sigmoid(z)).astype(o_ref.dtype)


def _s5_glu(g, w_glu):
    m, k = g.shape
    bm, bn, est = _mm_tiles(m, [k], k, [4], 2, True)
    return pl.pallas_call(
        _glu_kernel,
        grid=(m // bm, k // bn),
        in_specs=[pl.BlockSpec((bm, k), lambda i, j: (i, 0)),
                  pl.BlockSpec((k, bn), lambda i, j: (0, j)),
                  pl.BlockSpec((bm, bn), lambda i, j: (i, j))],
        out_specs=pl.BlockSpec((bm, bn), lambda i, j: (i, j)),
        out_shape=jax.ShapeDtypeStruct((m, k), BF16),
        compiler_params=_cparams(("parallel", "parallel"), est),
        name="s5_glu",
    )(g, w_glu, g)


def _shift_rows(h, prev, k):
    r = pltpu.roll(h, k, 0)
    p = pltpu.roll(prev, k, 0)
    row = lax.broadcasted_iota(jnp.int32, prev.shape, 0)
    top = jnp.where(row < k, p, r[:SUBLANE])
    return jnp.concatenate([top, r[SUBLANE:]], axis=0)


def _ffn_up_kernel(a_ref, wa_ref, wb_ref, cwa_ref, cwb_ref, o_ref, hbuf, carry, *, tiles_per_seq, nj):
    i, j = pl.program_id(0), pl.program_id(1)
    slot = i % 2
    rows = hbuf.shape[1]

    @pl.when(jnp.logical_and(i == 0, j == 0))
    def _():
        hbuf[...] = jnp.zeros_like(hbuf)
        carry[...] = jnp.zeros_like(carry)

    def epilogue():
        jp = jnp.maximum(j - 1, 0)
        keep = jnp.where((i % tiles_per_seq) == 0, 0.0, 1.0)
        outs = []
        for s, cw_ref in enumerate((cwa_ref, cwb_ref)):
            h = hbuf[s]
            prev = carry[1 - slot, jp, s * SUBLANE:(s + 1) * SUBLANE, :] * keep
            cw = cw_ref[...]
            outs.append(cw[2:3] * h + cw[1:2] * _shift_rows(h, prev, 1) + cw[0:1] * _shift_rows(h, prev, 2))
        o_ref[...] = (jax.nn.silu(outs[0]) * outs[1]).astype(o_ref.dtype)

    @pl.when(j < nj)
    def _():
        a = a_ref[...]
        ha = jnp.dot(a, wa_ref[...].astype(BF16), preferred_element_type=F32)
        hb = jnp.dot(a, wb_ref[...].astype(BF16), preferred_element_type=F32)
        epilogue()
        hbuf[0] = ha
        hbuf[1] = hb
        carry[slot, j, 0:SUBLANE, :] = ha[rows - SUBLANE:, :]
        carry[slot, j, SUBLANE:2 * SUBLANE, :] = hb[rows - SUBLANE:, :]

    @pl.when(j == nj)
    def _():
        epilogue()


def _ffn_up(xn, w_up, layer, conv_w, seq):
    m, k = xn.shape
    f = w_up.shape[2] // 2
    bm = _pick(seq, 1024, SUBLANE)
    bn = _pick(f, 256, LANE)
    nj = f // bn
    est = (2 * bm * k * 2 + 4 * k * bn * 4 + 2 * k * bn * 2 + 2 * bm * bn * 2 + 2 * nj * 16 * bn * 4
           + 2 * bm * bn * 4 + 12 * bm * bn * 4)
    cur = lambda j: jnp.minimum(j, nj - 1)
    prv = lambda j: jnp.maximum(j - 1, 0)
    return pl.pallas_call(
        functools.partial(_ffn_up_kernel, tiles_per_seq=seq // bm, nj=nj),
        grid=(m // bm, nj + 1),
        in_specs=[pl.BlockSpec((bm, k), lambda i, j: (i, 0)),
                  pl.BlockSpec((None, k, bn), lambda i, j: (layer, 0, cur(j))),
                  pl.BlockSpec((None, k, bn), lambda i, j: (layer, 0, cur(j) + nj)),
                  pl.BlockSpec((SUBLANE, bn), lambda i, j: (0, prv(j))),
                  pl.BlockSpec((SUBLANE, bn), lambda i, j: (0, prv(j) + nj))],
        out_specs=pl.BlockSpec((bm, bn), lambda i, j: (i, prv(j))),
        out_shape=jax.ShapeDtypeStruct((m, f), BF16),
        scratch_shapes=[pltpu.VMEM((2, bm, bn), F32), pltpu.VMEM((2, nj, 2 * SUBLANE, bn), F32)],
        compiler_params=_cparams(("arbitrary", "arbitrary"), est),
        name="ffn_up",
    )(xn, w_up, w_up, conv_w, conv_w)


def _gdn_pre_kernel(x_ref, w_ref, o_ref, *, nq, nk):
    j = pl.program_id(1)
    x = x_ref[...]
    w = w_ref[...]
    kw = 4
    row = lax.broadcasted_iota(jnp.int32, x.shape, 0)
    acc = w[kw - 1:kw] * x
    for s in range(1, kw):
        acc = acc + w[kw - 1 - s:kw - s] * jnp.where(row >= s, pltpu.roll(x, s, 0), 0.0)
    y = jax.nn.silu(acc)
    parts = []
    for c in range(x.shape[1] // GDN_DK):
        yh = y[:, c * GDN_DK:(c + 1) * GDN_DK]
        parts.append(yh * lax.rsqrt(jnp.sum(yh * yh, axis=-1, keepdims=True) + NORM_EPS))
    yn = parts[0] if len(parts) == 1 else jnp.concatenate(parts, axis=1)
    is_q = j < nq
    is_k = jnp.logical_and(j >= nq, j < nq + nk)
    o_ref[...] = jnp.where(is_q, yn * (GDN_DK ** -0.5), jnp.where(is_k, yn, y))


def _gdn_pre(src, conv_w, batch, seq, heads):
    width = conv_w.shape[1]
    cb = GDN_DK
    nq = heads * GDN_DK // cb
    est = 4 * seq * cb * 4 + 8 * seq * cb * 4
    return pl.pallas_call(
        functools.partial(_gdn_pre_kernel, nq=nq, nk=nq),
        grid=(batch, width // cb),
        in_specs=[pl.BlockSpec((seq, cb), lambda b, j: (b, j)),
                  pl.BlockSpec((SUBLANE, cb), lambda b, j: (0, j))],
        out_specs=pl.BlockSpec((seq, cb), lambda b, j: (b, j)),
        out_shape=jax.ShapeDtypeStruct((batch * seq, width), F32),
        compiler_params=_cparams(("parallel", "parallel"), est),
        name="gdn_pre",
    )(src, conv_w)


GDN_ROWS = 256
GDN_HEADS_PER_STEP = 2


def _dot_hi(a, b):
    return jnp.dot(a, b, precision=HI, preferred_element_type=F32)


def _bdot(a, b):
    return jnp.dot(a.astype(BF16), b.astype(BF16), preferred_element_type=F32)


def _bdot_nt(a, b):
    return lax.dot_general(a.astype(BF16), b.astype(BF16), (((1,), (1,)), ((), ())), preferred_element_type=F32)


def _bdot_tn(a, b):
    return lax.dot_general(a.astype(BF16), b.astype(BF16), (((0,), (0,)), ((), ())), preferred_element_type=F32)


def _gdn_head(alog, dtb, q, k, v, a_raw, b_raw, gate, ng, s, masks, rows):
    tril_f, same_f, tril, strict, eye_f = masks
    a_raw = a_raw + dtb
    softplus = jnp.maximum(a_raw, 0.0) + jnp.log(1.0 + jnp.exp(-jnp.abs(a_raw)))
    g = -jnp.exp(jnp.zeros_like(a_raw) + alog) * softplus
    beta = jax.nn.sigmoid(b_raw)
    gb = jnp.broadcast_to(g, (rows, LANE))
    gc = _dot_hi(tril_f, gb)
    gl = _dot_hi(same_f, gb)
    gcol = jnp.concatenate([gc] * (rows // LANE), axis=1) if rows > LANE else gc[:, :rows]
    grow = jnp.transpose(gcol)
    decay = jnp.exp(jnp.where(tril, gcol - grow, MASK_VALUE))
    kb = k * beta
    vb = v * beta
    lmat = jnp.where(strict, _bdot_nt(kb, k) * decay, 0.0)
    tinv = eye_f - lmat
    pw = lmat
    for _ in range(int(math.log2(CHUNK)) - 1):
        pw = _bdot(pw, pw)
        tinv = tinv + _bdot(tinv, pw)
    eg = jnp.exp(gc)
    sol = _bdot(tinv, jnp.concatenate([vb, kb * eg], axis=1))
    u_c, w_c = sol[:, :GDN_DV], sol[:, GDN_DV:]
    intra = _bdot_nt(q, k) * decay
    q_dec = q * eg
    k_dec = k * jnp.exp(gl - gc)
    egl = jnp.exp(gl)
    iuw = _bdot(intra, sol)
    o_loc = iuw[:, :GDN_DV]
    q_eff = q_dec - iuw[:, GDN_DV:]
    outs = []
    for c in range(rows // CHUNK):
        lo, hi = c * CHUNK, (c + 1) * CHUNK
        kuw = _bdot_tn(k_dec[lo:hi], sol[lo:hi])
        outs.append(_bdot(q_eff[lo:hi], s) + o_loc[lo:hi])
        s = s * egl[lo:lo + 1, :] + kuw[:, :GDN_DV] - _bdot(kuw[:, GDN_DV:], s)
    o = jnp.concatenate(outs, axis=0)
    o = o * lax.rsqrt(jnp.mean(o * o, axis=-1, keepdims=True) + NORM_EPS) * ng
    return o * jax.nn.silu(gate), s


def _gdn_kernel(alog_ref, dtb_ref, q_ref, k_ref, v_ref, ab_ref, gate_ref, ng_ref, o_ref, state, *,
                rows, hps, heads):
    hb = pl.program_id(1)
    r = pl.program_id(2)
    ab = ab_ref[...]
    ab_lane = lax.broadcasted_iota(jnp.int32, ab.shape, 1)

    @pl.when(r == 0)
    def _():
        state[...] = jnp.zeros_like(state)

    ri = lax.broadcasted_iota(jnp.int32, (rows, rows), 0)
    ci = lax.broadcasted_iota(jnp.int32, (rows, rows), 1)
    same = (ri // CHUNK) == (ci // CHUNK)
    tril = jnp.logical_and(same, ci <= ri)
    strict = jnp.logical_and(same, ci < ri)
    masks = (tril.astype(F32), same.astype(F32), tril, strict, jnp.where(ri == ci, 1.0, 0.0))
    ng = ng_ref[...]
    for i in range(hps):
        cs = slice(i * GDN_DK, (i + 1) * GDN_DK)
        h = hb * hps + i
        a_raw = jnp.sum(jnp.where(ab_lane == h, ab, 0.0), axis=1, keepdims=True)
        b_raw = jnp.sum(jnp.where(ab_lane == heads + h, ab, 0.0), axis=1, keepdims=True)
        o, s = _gdn_head(alog_ref[h], dtb_ref[h], q_ref[:, cs], k_ref[:, cs], v_ref[:, cs], a_raw, b_raw,
                         gate_ref[:, cs], ng, state[i], masks, rows)
        state[i] = s
        o_ref[:, cs] = o.astype(o_ref.dtype)


def _gdn_core(qkv, ab, gate_src, gate_col0, a_log, dt_bias, norm_g, batch, seq, heads):
    rows = _pick(seq, GDN_ROWS, CHUNK)
    nr = seq // rows
    hps = GDN_HEADS_PER_STEP if heads % GDN_HEADS_PER_STEP == 0 else 1
    nhb = heads // hps
    wblk = hps * GDN_DK
    gcb = gate_col0 // wblk
    est = hps * (12 * rows * LANE * 4 + 16 * rows * rows * 4 + 8 * rows * LANE * 4)
    tok = lambda b, h, r: b * nr + r
    return pl.pallas_call(
        functools.partial(_gdn_kernel, rows=rows, hps=hps, heads=heads),
        grid=(batch, nhb, nr),
        in_specs=[pl.BlockSpec(memory_space=pltpu.SMEM),
                  pl.BlockSpec(memory_space=pltpu.SMEM),
                  pl.BlockSpec((rows, wblk), lambda b, h, r: (tok(b, h, r), h)),
                  pl.BlockSpec((rows, wblk), lambda b, h, r: (tok(b, h, r), nhb + h)),
                  pl.BlockSpec((rows, wblk), lambda b, h, r: (tok(b, h, r), 2 * nhb + h)),
                  pl.BlockSpec((rows, LANE), lambda b, h, r: (tok(b, h, r), 0)),
                  pl.BlockSpec((rows, wblk), lambda b, h, r: (tok(b, h, r), gcb + h)),
                  pl.BlockSpec((1, GDN_DV), lambda b, h, r: (0, 0))],
        out_specs=pl.BlockSpec((rows, wblk), lambda b, h, r: (tok(b, h, r), h)),
        out_shape=jax.ShapeDtypeStruct((batch * seq, heads * GDN_DV), BF16),
        scratch_shapes=[pltpu.VMEM((hps, GDN_DK, GDN_DV), F32)],
        compiler_params=_cparams(("parallel", "parallel", "arbitrary"), est),
        name="gdn_core",
    )(a_log.astype(F32), dt_bias.astype(F32), qkv, qkv, qkv, ab, gate_src,
      norm_g.reshape(1, GDN_DV).astype(F32))


SB_TQ = 512
SB_TK = 256


def _sb_kernel(q_ref, kt_ref, v_ref, o_ref, acc_s, c_s, *, tq, tk):
    i = pl.program_id(2)
    r = tq // tk
    q = q_ref[...]
    ur = lax.broadcasted_iota(jnp.int32, (2 * tk, tk), 0) % tk
    uc = lax.broadcasted_iota(jnp.int32, (2 * tk, tk), 1)
    upper2 = jnp.where(ur > uc, 1.0, 0.0).astype(BF16)
    ri = lax.broadcasted_iota(jnp.int32, (tq, tk), 0)
    ci = lax.broadcasted_iota(jnp.int32, (tq, tk), 1)
    acc_s[...] = jnp.zeros_like(acc_s)
    c_s[...] = jnp.zeros_like(c_s)

    def logits(j):
        z = jnp.dot(q, kt_ref[j], preferred_element_type=F32)
        return z, -(jnp.maximum(z, 0.0) + jnp.log(1.0 + jnp.exp2(-jnp.abs(z))) * LOG2E)

    def weights(z, lm, c):
        hi = lm.astype(BF16)
        lo = (lm - hi.astype(F32)).astype(BF16)
        rest = jnp.dot(jnp.concatenate([hi, lo], axis=1), upper2, preferred_element_type=F32)
        return jnp.exp2(z + lm + rest + c)

    def values(j):
        return v_ref[pl.ds(pl.multiple_of(j * tk, tk), tk), :]

    for d in range(r - 1, -1, -1):
        j = i * r + d
        before = (ci + d * tk) < ri
        z, lm = logits(j)
        lm = jnp.where(before, lm, 0.0)
        w = jnp.where(before, weights(z, lm, c_s[...]), 0.0)
        acc_s[...] += jnp.dot(w.astype(BF16), values(j), preferred_element_type=F32)
        c_s[...] += jnp.sum(lm, axis=1, keepdims=True)

    def single(j):
        z, lm = logits(j)
        w = weights(z, lm, c_s[...])
        acc_s[...] += jnp.dot(w.astype(BF16), values(j), preferred_element_type=F32)
        c_s[...] += jnp.sum(lm, axis=1, keepdims=True)

    def pair(p, carry):
        ja = i * r - 1 - 2 * p
        jb = ja - 1
        za, lma = logits(ja)
        zb, lmb = logits(jb)
        c = c_s[...]
        ta = jnp.sum(lma, axis=1, keepdims=True)
        wa = weights(za, lma, c)
        wb = weights(zb, lmb, c + ta)
        acc_s[...] += (jnp.dot(wa.astype(BF16), values(ja), preferred_element_type=F32)
                       + jnp.dot(wb.astype(BF16), values(jb), preferred_element_type=F32))
        c_s[...] = c + ta + jnp.sum(lmb, axis=1, keepdims=True)
        return carry

    n_full = i * r
    lax.fori_loop(0, n_full // 2, pair, 0)
    if r % 2 == 1:
        @pl.when(n_full % 2 == 1)
        def _():
            single(0)
    o_ref[...] = acc_s[...].astype(o_ref.dtype)


def _sb_attention(qkv, batch, seq, heads):
    tq = _pick(seq, SB_TQ, LANE)
    tk = _pick(tq, SB_TK, LANE)
    nq, nk = seq // tq, seq // tk
    kt = jnp.transpose(qkv.reshape(batch, nk, tk, 3, heads, SB_D)[:, :, :, 1], (0, 3, 1, 4, 2))
    est = 4 * tq * SB_D * 2 + 4 * seq * SB_D * 2 + 2 * tq * SB_D * 4 + 24 * tq * tk * 4
    return pl.pallas_call(
        functools.partial(_sb_kernel, tq=tq, tk=tk),
        grid=(batch, heads, nq),
        in_specs=[pl.BlockSpec((tq, SB_D), lambda b, h, i: (b * nq + i, h)),
                  pl.BlockSpec((None, None, nk, SB_D, tk), lambda b, h, i: (b, h, 0, 0, 0)),
                  pl.BlockSpec((seq, SB_D), lambda b, h, i: (b, 2 * heads + h))],
        out_specs=pl.BlockSpec((tq, SB_D), lambda b, h, i: (b * nq + i, h)),
        out_shape=jax.ShapeDtypeStruct((batch * seq, heads * SB_D), BF16),
        scratch_shapes=[pltpu.VMEM((tq, SB_D), F32), pltpu.VMEM((tq, 1), F32)],
        compiler_params=_cparams(("parallel", "parallel", "arbitrary"), est),
        name="sb_attention",
    )(qkv, kt, qkv)


def _pad_cols(w, n):
    return jnp.pad(w, ((0, 0), (0, n - w.shape[1])))


def _ffn(h, ln_g, w_up, layer, conv_w, w_down_bf16, seq):
    xn = _rmsnorm(h, ln_g, BF16)
    conv8 = jnp.pad(conv_w.astype(F32), ((0, SUBLANE - conv_w.shape[0]), (0, 0)))
    act = _ffn_up(xn, w_up, layer, conv8, seq)
    return _matmul([(act, w_down_bf16)], F32, residual=h, b_layer=layer)


def _even_layer(h, ln_g, w_in, w_out, q_norm, kv_norm, w_uq, w_ukv,
                a_re, a_im, log_step, b_re, b_im, c_re, c_im, d_skip, w_glu, batch, seq):
    d = h.shape[1]
    q_rank, kv_rank = q_norm.shape[0], kv_norm.shape[0]
    heads = w_uq.shape[1] // (MLA_NOPE + MLA_ROPE)
    s5_width = d_skip.shape[0]
    o_kr = q_rank + kv_rank
    o_s5 = o_kr + MLA_ROPE
    latent_w = -(-(o_kr + LANE) // s5_width) * s5_width if s5_width >= LANE else o_kr + LANE
    w_lat = _pad_cols(w_in[:, :o_s5], latent_w)
    w_cat = jnp.concatenate([w_lat, w_in[:, o_s5:]], axis=1).astype(BF16)
    hn = _rmsnorm(h, ln_g, BF16)
    proj = _matmul([(hn, w_cat)], F32)
    cq = _rmsnorm(proj, q_norm, BF16, col0=0, width=q_rank)
    ckv = _rmsnorm(proj, kv_norm, BF16, col0=q_rank, width=kv_rank)
    q_scale = (MLA_NOPE + MLA_ROPE) ** -0.5 * LOG2E
    w_uq_p = jnp.pad(w_uq.reshape(q_rank, heads, MLA_NOPE + MLA_ROPE) * q_scale,
                     ((0, 0), (0, 0), (0, MLA_QK - MLA_NOPE - MLA_ROPE))).reshape(q_rank, heads * MLA_QK)
    q = _matmul([(cq, w_uq_p.astype(BF16))], F32)
    kv = _matmul([(ckv, w_ukv.astype(BF16))], BF16)
    cosf, sinf = _rope_tables(seq)
    kr = _rope_k(proj, o_kr, cosf, sinf, seq)
    out_a = _mla_attention(q, kv, kr, cosf, sinf, batch, seq, heads)
    params = _s5t_params(a_re, a_im, log_step, b_re, b_im, c_re, c_im, d_skip)
    g = _s5_toeplitz(proj[:, latent_w:], params, batch, seq)
    out_b = _s5_glu(g, w_glu.astype(BF16))
    na = heads * MLA_V
    return _matmul([(out_a, w_out[:na].astype(BF16)), (out_b, w_out[na:].astype(BF16))], F32, residual=h)


def _odd_layer(h, ln_g, w_in, w_out, conv_w, a_log, dt_bias, norm_g, batch, seq):
    heads = a_log.shape[0]
    qkv_w = heads * (2 * GDN_DK + GDN_DV)
    o2 = qkv_w + 2 * heads
    o3 = o2 + heads * GDN_DV
    hn = _rmsnorm(h, ln_g, BF16)
    w_main = jnp.concatenate([w_in[:, :qkv_w], w_in[:, o2:o3]], axis=1).astype(BF16)
    w_ab = _pad_cols(w_in[:, qkv_w:o2], LANE).astype(BF16)
    sb_heads = (w_in.shape[1] - o3) // (3 * SB_D)
    nsq = sb_heads * SB_D
    w_sb = jnp.concatenate([w_in[:, o3:o3 + nsq] * (SB_D ** -0.5 * LOG2E), w_in[:, o3 + nsq:]], axis=1).astype(BF16)
    main = _matmul([(hn, w_main)], F32)
    ab = _matmul([(hn, w_ab)], F32)
    sb = _matmul([(hn, w_sb)], BF16)
    conv8 = jnp.pad(conv_w.astype(F32), ((0, SUBLANE - conv_w.shape[0]), (0, 0)))
    qkv = _gdn_pre(main, conv8, batch, seq, heads)
    out_c = _gdn_core(qkv, ab, main, qkv_w, a_log, dt_bias, norm_g, batch, seq, heads)
    out_d = _sb_attention(sb, batch, seq, sb_heads)
    nc = heads * GDN_DV
    return _matmul([(out_c, w_out[:nc].astype(BF16)), (out_d, w_out[nc:].astype(BF16))], F32, residual=h)


def kernel(x, ln_mix, ln_ffn, ln_final, ffn_w_up, ffn_conv, ffn_w_down, ev_w_in, ev_w_out, mla_q_norm,
           mla_kv_norm, mla_w_uq, mla_w_ukv, s5_a_re, s5_a_im, s5_log_step, s5_b_re, s5_b_im, s5_c_re,
           s5_c_im, s5_d, s5_w_glu, od_w_in, od_w_out, gdn_conv, gdn_a_log, gdn_dt_bias, gdn_norm):
    batch, seq, d = x.shape
    h = x.reshape(batch * seq, d).astype(F32)
    ffn_w_up = ffn_w_up.astype(F32)
    w_down_bf16 = ffn_w_down.astype(BF16)
    for layer in range(ln_mix.shape[0]):
        i = layer // 2
        if layer % 2 == 0:
            h = _even_layer(h, ln_mix[layer], ev_w_in[i], ev_w_out[i], mla_q_norm[i], mla_kv_norm[i],
                            mla_w_uq[i], mla_w_ukv[i], s5_a_re[i], s5_a_im[i], s5_log_step[i], s5_b_re[i],
                            s5_b_im[i], s5_c_re[i], s5_c_im[i], s5_d[i], s5_w_glu[i], batch, seq)
        else:
            h = _odd_layer(h, ln_mix[layer], od_w_in[i], od_w_out[i], gdn_conv[i], gdn_a_log[i],
                           gdn_dt_bias[i], gdn_norm[i], batch, seq)
        h = _ffn(h, ln_ffn[layer], ffn_w_up, layer, ffn_conv[layer], w_down_bf16, seq)
    return _rmsnorm(h, ln_final, x.dtype).reshape(batch, seq, d)
```

```python
import functools
import math

import jax
import jax.numpy as jnp
from jax import lax
from jax.experimental import pallas as pl
from jax.experimental.pallas import tpu as pltpu

F32 = jnp.float32
BF16 = jnp.bfloat16
HI = lax.Precision.HIGHEST

CHUNK = 64
NORM_EPS = 1e-6
MLA_NOPE = 128
MLA_ROPE = 64
MLA_V = 128
MLA_QK = 256
ROPE_BASE = 10000.0
S5_GROUP = 16
S5_STATE = 64
GDN_DK = 128
GDN_DV = 128
SB_D = 128

LANE = 128
SUBLANE = 8
VMEM_LIMIT_MAX = 58 * 1024 * 1024
VMEM_LIMIT_MIN = 32 * 1024 * 1024
MASK_VALUE = -1e30
LOG2E = math.log2(math.e)


def _cparams(sem, vmem_est):
    limit = int(min(max(vmem_est * 5 // 4 + (4 << 20), VMEM_LIMIT_MIN), VMEM_LIMIT_MAX))
    return pltpu.CompilerParams(dimension_semantics=sem, vmem_limit_bytes=limit)


def _pick(n, pref, mult):
    if n <= pref:
        return n
    t = (pref // mult) * mult
    while t >= mult:
        if n % t == 0:
            return t
        t -= mult
    return n


def _rmsnorm_kernel(x_ref, g_ref, o_ref):
    x = x_ref[...].astype(F32)
    ms = jnp.mean(x * x, axis=-1, keepdims=True)
    o_ref[...] = (x * lax.rsqrt(ms + NORM_EPS) * g_ref[...]).astype(o_ref.dtype)


def _rmsnorm(x, g, out_dtype, *, col0=0, width=None):
    m = x.shape[0]
    width = x.shape[1] if width is None else width
    tm = _pick(m, 256, SUBLANE)
    cb = col0 // width
    est = 2 * tm * width * (x.dtype.itemsize + jnp.dtype(out_dtype).itemsize) + 4 * tm * width * 4
    return pl.pallas_call(
        _rmsnorm_kernel,
        grid=(m // tm,),
        in_specs=[pl.BlockSpec((tm, width), lambda i: (i, cb)),
                  pl.BlockSpec((1, width), lambda i: (0, 0))],
        out_specs=pl.BlockSpec((tm, width), lambda i: (i, 0)),
        out_shape=jax.ShapeDtypeStruct((m, width), out_dtype),
        compiler_params=_cparams(("parallel",), est),
        name="rmsnorm",
    )(x, g.reshape(1, width).astype(F32))


def _mm_kernel(*refs, n_pairs, has_res):
    o_ref = refs[-1]
    acc = None
    for p in range(n_pairs):
        a = refs[2 * p][...].astype(BF16)
        d = jnp.dot(a, refs[2 * p + 1][...], preferred_element_type=F32)
        acc = d if acc is None else acc + d
    if has_res:
        acc = acc + refs[2 * n_pairs][...]
    o_ref[...] = acc.astype(o_ref.dtype)


def _mm_tiles(m, ks, n, a_bytes, out_bytes, has_res, budget=52 << 20):
    for bm_p, bn_p in ((1024, 1024), (1024, 512), (512, 512), (512, 256), (256, 256), (256, 128), (128, 128)):
        bm = _pick(m, bm_p, SUBLANE)
        bn = _pick(n, bn_p, LANE)
        est = bm * bn * 4
        for k, ab in zip(ks, a_bytes):
            est += 2 * (bm * k * ab + k * bn * 2)
        est += 2 * bm * bn * (out_bytes + (4 if has_res else 0))
        if est <= budget:
            return bm, bn, est
    return bm, bn, est


def _matmul(pairs, out_dtype, residual=None, *, a_col0=None, b_layer=None):
    m = pairs[0][0].shape[0]
    n = pairs[0][1].shape[-1]
    ks = [b.shape[-2] for _, b in pairs]
    a_col0 = [0] * len(pairs) if a_col0 is None else a_col0
    a_bytes = [a.dtype.itemsize for a, _ in pairs]
    bm, bn, est = _mm_tiles(m, ks, n, a_bytes, jnp.dtype(out_dtype).itemsize, residual is not None)
    in_specs, args = [], []
    for (a, b), k, c0 in zip(pairs, ks, a_col0):
        cb = c0 // k
        in_specs.append(pl.BlockSpec((bm, k), lambda i, j, cb=cb: (i, cb)))
        if b_layer is None:
            in_specs.append(pl.BlockSpec((k, bn), lambda i, j: (0, j)))
        else:
            in_specs.append(pl.BlockSpec((None, k, bn), lambda i, j: (b_layer, 0, j)))
        args += [a, b]
    if residual is not None:
        in_specs.append(pl.BlockSpec((bm, bn), lambda i, j: (i, j)))
        args.append(residual)
    return pl.pallas_call(
        functools.partial(_mm_kernel, n_pairs=len(pairs), has_res=residual is not None),
        grid=(m // bm, n // bn),
        in_specs=in_specs,
        out_specs=pl.BlockSpec((bm, bn), lambda i, j: (i, j)),
        out_shape=jax.ShapeDtypeStruct((m, n), out_dtype),
        compiler_params=_cparams(("parallel", "parallel"), est),
        name="matmul",
    )(*args)


def _rope_tables(seq):
    half = MLA_ROPE // 2
    inv_freq = ROPE_BASE ** (-jnp.arange(half, dtype=F32) / half)
    ang = jnp.arange(seq, dtype=F32)[:, None] * inv_freq[None, :]
    cos, sin = jnp.cos(ang), jnp.sin(ang)
    zeros = jnp.zeros((seq, LANE - MLA_ROPE), F32)
    return (jnp.concatenate([cos, cos, zeros], axis=1),
            jnp.concatenate([-sin, sin, zeros], axis=1))


def _rope_lanes(x, cosf, sinf):
    half = MLA_ROPE // 2
    lane = lax.broadcasted_iota(jnp.int32, x.shape, 1)
    swapped = jnp.where(lane < half, pltpu.roll(x, LANE - half, 1), pltpu.roll(x, half, 1))
    return x * cosf + swapped * sinf


def _rope_kernel(x_ref, cos_ref, sin_ref, o_ref):
    o_ref[...] = _rope_lanes(x_ref[...], cos_ref[...], sin_ref[...]).astype(o_ref.dtype)


def _rope_k(x, col0, cosf, sinf, seq):
    m = x.shape[0]
    tm = _pick(seq, 512, SUBLANE)
    per_seq = seq // tm
    cb = col0 // LANE
    return pl.pallas_call(
        _rope_kernel,
        grid=(m // tm,),
        in_specs=[pl.BlockSpec((tm, LANE), lambda i: (i, cb)),
                  pl.BlockSpec((tm, LANE), lambda i: (i % per_seq, 0)),
                  pl.BlockSpec((tm, LANE), lambda i: (i % per_seq, 0))],
        out_specs=pl.BlockSpec((tm, LANE), lambda i: (i, 0)),
        out_shape=jax.ShapeDtypeStruct((m, LANE), BF16),
        compiler_params=_cparams(("parallel",), 16 * tm * LANE * 4),
        name="rope_k",
    )(x, cosf, sinf)


def _mla_kernel(q_ref, kt_ref, v_ref, cos_ref, sin_ref, o_ref, qs, m_s, l_s, acc_s, *, tq):
    i = pl.program_id(2)
    q = q_ref[...]
    qs[:, :MLA_NOPE] = q[:, :MLA_NOPE].astype(BF16)
    qs[:, MLA_NOPE:] = _rope_lanes(q[:, MLA_NOPE:], cos_ref[...], sin_ref[...]).astype(BF16)

    def tile(j):
        s = jnp.dot(qs[...], kt_ref[j], preferred_element_type=F32)
        return s, v_ref[pl.ds(pl.multiple_of(j * tq, tq), tq), :]

    s, v = tile(i)
    rc = lax.broadcasted_iota(jnp.int32, (tq, tq), 0) // CHUNK
    cc = lax.broadcasted_iota(jnp.int32, (tq, tq), 1) // CHUNK
    s = jnp.where(cc <= rc, s, MASK_VALUE)
    m = jnp.max(s, axis=1, keepdims=True)
    p = jnp.exp2(s - m)
    m_s[...] = m
    l_s[...] = jnp.sum(p, axis=1, keepdims=True)
    acc_s[...] = jnp.dot(p.astype(BF16), v, preferred_element_type=F32)

    def body(j, carry):
        s, v = tile(j)
        m_prev = m_s[...]
        m_new = jnp.maximum(m_prev, jnp.max(s, axis=1, keepdims=True))
        alpha = jnp.exp2(m_prev - m_new)
        p = jnp.exp2(s - m_new)
        l_s[...] = alpha * l_s[...] + jnp.sum(p, axis=1, keepdims=True)
        acc_s[...] = alpha * acc_s[...] + jnp.dot(p.astype(BF16), v, preferred_element_type=F32)
        m_s[...] = m_new
        return carry

    lax.fori_loop(0, i, body, 0)
    o_ref[...] = (acc_s[...] / l_s[...]).astype(o_ref.dtype)


MLA_TILE = 1024


def _mla_attention(q, kv, kr, cosf, sinf, batch, seq, heads):
    tq = _pick(seq, MLA_TILE, CHUNK)
    nq = seq // tq
    kn_t = jnp.transpose(kv.reshape(batch, nq, tq, heads, MLA_QK)[..., :MLA_NOPE], (0, 3, 1, 4, 2))
    kr_t = jnp.transpose(kr.reshape(batch, nq, tq, LANE), (0, 1, 3, 2))
    kt = jnp.concatenate([kn_t, jnp.broadcast_to(kr_t[:, None], (batch, heads, nq, LANE, tq))], axis=3)
    est = (2 * tq * MLA_QK * 4 + 2 * seq * (MLA_QK + LANE) * 2 + 4 * tq * LANE * 4 + 2 * tq * LANE * 2
           + tq * MLA_QK * 2 + 3 * tq * LANE * 4 + 6 * tq * tq * 4)
    return pl.pallas_call(
        functools.partial(_mla_kernel, tq=tq),
        grid=(batch, heads, nq),
        in_specs=[pl.BlockSpec((tq, MLA_QK), lambda b, h, i: (b * nq + i, h)),
                  pl.BlockSpec((None, None, nq, MLA_QK, tq), lambda b, h, i: (b, h, 0, 0, 0)),
                  pl.BlockSpec((seq, MLA_V), lambda b, h, i: (b, 2 * h + 1)),
                  pl.BlockSpec((tq, LANE), lambda b, h, i: (i, 0)),
                  pl.BlockSpec((tq, LANE), lambda b, h, i: (i, 0))],
        out_specs=pl.BlockSpec((tq, MLA_V), lambda b, h, i: (b * nq + i, h)),
        out_shape=jax.ShapeDtypeStruct((batch * seq, heads * MLA_V), BF16),
        scratch_shapes=[pltpu.VMEM((tq, MLA_QK), BF16), pltpu.VMEM((tq, 1), F32),
                        pltpu.VMEM((tq, 1), F32), pltpu.VMEM((tq, MLA_V), F32)],
        compiler_params=_cparams(("parallel", "parallel", "arbitrary"), est),
        name="mla_attention",
    )(q, kt, kv, cosf, sinf)


S5_GROUPS_PER_BLOCK = 8


def _s5_kernel(u_ref, bre_ref, bim_ref, cre_ref, cim_ref, pw_ref, d_ref, o_ref, sre, sim, car, *, tt):
    t = pl.program_id(2)

    @pl.when(t == 0)
    def _():
        car[...] = jnp.zeros_like(car)

    u = u_ref[...]
    ub = u.astype(BF16)
    sre[...] = jnp.dot(ub, bre_ref[0], preferred_element_type=F32)
    sim[...] = jnp.dot(ub, bim_ref[0], preferred_element_type=F32)
    ns = sre.shape[1]
    row = lax.broadcasted_iota(jnp.int32, (SUBLANE, ns), 0)
    steps = [(1, pw_ref[0, 0], pw_ref[0, 1]), (2, pw_ref[0, 2], pw_ref[0, 3]), (4, pw_ref[0, 4], pw_ref[0, 5])]
    pwr, pwi = pw_ref[0, 6], pw_ref[0, 7]

    def body(r, carry):
        cr, ci = carry
        off = pl.multiple_of(r * SUBLANE, SUBLANE)
        xr = sre[pl.ds(off, SUBLANE), :]
        xi = sim[pl.ds(off, SUBLANE), :]
        for k, ar, ai in steps:
            sr = jnp.where(row >= k, pltpu.roll(xr, k, 0), 0.0)
            si = jnp.where(row >= k, pltpu.roll(xi, k, 0), 0.0)
            xr, xi = xr + ar * sr - ai * si, xi + ar * si + ai * sr
        xr, xi = xr + pwr * cr - pwi * ci, xi + pwr * ci + pwi * cr
        sre[pl.ds(off, SUBLANE), :] = xr
        sim[pl.ds(off, SUBLANE), :] = xi
        return xr[SUBLANE - 1:, :], xi[SUBLANE - 1:, :]

    cr, ci = lax.fori_loop(0, tt // SUBLANE, body, (car[0:1, :], car[1:2, :]))
    car[0:1, :] = cr
    car[1:2, :] = ci
    y = (jnp.dot(sre[...].astype(BF16), cre_ref[0], preferred_element_type=F32)
         - jnp.dot(sim[...].astype(BF16), cim_ref[0], preferred_element_type=F32))
    y = y + d_ref[...] * u
    o_ref[...] = jax.nn.gelu(y, approximate=True)


def _s5_params(a_re, a_im, log_step, b_re, b_im, c_re, c_im):
    g, p = a_re.shape
    gb = min(S5_GROUPS_PER_BLOCK, g)
    nb = g // gb
    lam = lax.complex(a_re.astype(F32), a_im.astype(F32))
    step = jnp.exp(log_step.astype(F32))[:, None]
    lam_dt = lam * step
    lam_bar = jnp.exp(lam_dt)
    b_bar = ((lam_bar - 1.0) / lam)[..., None] * lax.complex(b_re.astype(F32), b_im.astype(F32))
    c_c = lax.complex(c_re.astype(F32), c_im.astype(F32))
    eye = jnp.eye(gb, dtype=F32)

    def bd_in(x):
        x = x.reshape(nb, gb, p, S5_GROUP)
        return jnp.einsum('ngpi,gh->ngihp', x, eye).reshape(nb, gb * S5_GROUP, gb * p)

    def bd_out(x):
        x = x.reshape(nb, gb, S5_GROUP, p)
        return jnp.einsum('ngip,gh->ngphi', x, eye).reshape(nb, gb * p, gb * S5_GROUP)

    def rows(z):
        z = z.reshape(nb, 1, gb * p)
        return (jnp.broadcast_to(jnp.real(z), (nb, SUBLANE, gb * p)),
                jnp.broadcast_to(jnp.imag(z), (nb, SUBLANE, gb * p)))

    planes = []
    for k in (1, 2, 4):
        planes += list(rows(jnp.exp(lam_dt * float(k))))
    pw = jnp.exp(lam_dt[None] * jnp.arange(1, SUBLANE + 1, dtype=F32)[:, None, None])
    pw = jnp.moveaxis(pw.reshape(SUBLANE, nb, gb * p), 0, 1)
    planes += [jnp.real(pw), jnp.imag(pw)]
    return (bd_in(jnp.real(b_bar)).astype(BF16), bd_in(jnp.imag(b_bar)).astype(BF16),
            bd_out(jnp.real(c_c)).astype(BF16), bd_out(jnp.imag(c_c)).astype(BF16),
            jnp.stack(planes, axis=1), gb)


def _s5_scan(u_src, col0, params, d_skip, batch, seq):
    bre, bim, cre, cim, pw, gb = params
    nb = bre.shape[0]
    gc = gb * S5_GROUP
    ns = gb * S5_STATE
    tt = _pick(seq, 512, SUBLANE)
    nt = seq // tt
    cb0 = col0 // gc
    est = 4 * tt * gc * 4 + 8 * gc * ns * 4 + 4 * SUBLANE * 8 * ns * 4 + 6 * tt * ns * 4
    return pl.pallas_call(
        functools.partial(_s5_kernel, tt=tt),
        grid=(batch, nb, nt),
        in_specs=[pl.BlockSpec((tt, gc), lambda b, g, t: (b * nt + t, cb0 + g)),
                  pl.BlockSpec((1, gc, ns), lambda b, g, t: (g, 0, 0)),
                  pl.BlockSpec((1, gc, ns), lambda b, g, t: (g, 0, 0)),
                  pl.BlockSpec((1, ns, gc), lambda b, g, t: (g, 0, 0)),
                  pl.BlockSpec((1, ns, gc), lambda b, g, t: (g, 0, 0)),
                  pl.BlockSpec((1, 8, SUBLANE, ns), lambda b, g, t: (g, 0, 0, 0)),
                  pl.BlockSpec((1, gc), lambda b, g, t: (0, g))],
        out_specs=pl.BlockSpec((tt, gc), lambda b, g, t: (b * nt + t, g)),
        out_shape=jax.ShapeDtypeStruct((batch * seq, nb * gc), F32),
        scratch_shapes=[pltpu.VMEM((tt, ns), F32), pltpu.VMEM((tt, ns), F32), pltpu.VMEM((SUBLANE, ns), F32)],
        compiler_params=_cparams(("parallel", "parallel", "arbitrary"), est),
        name="s5_scan",
    )(u_src, bre, bim, cre, cim, pw, d_skip.reshape(1, -1).astype(F32))


S5_BLOCK = 16
S5_GROUPS_PER_STEP = 4


def _s5t_kernel(x_ref, toep_ref, ere_ref, eim_ref, fre_ref, fim_ref, coef_ref, d_ref, o_ref,
                u_ref, y_ref, er_s, ei_s, hr_s, hi_s, *, gps, nb, nb_seq):
    t_blk = S5_BLOCK
    tw = t_blk * S5_GROUP
    npair = gps // 2
    per_slab = LANE // S5_GROUP
    steps = [x_ref[pl.ds(t, nb, stride=t_blk), :] for t in range(t_blk)]
    win = lax.broadcasted_iota(jnp.int32, (nb, LANE), 1) // S5_GROUP
    in_win = [win == w for w in range(per_slab)]

    def place(pieces):
        out = None
        for w, (arr, src) in enumerate(pieces):
            sh = ((w - src) % per_slab) * S5_GROUP
            r = arr if sh == 0 else pltpu.roll(arr, sh, 1)
            out = r if out is None else jnp.where(in_win[w], r, out)
        return out

    for g in range(gps):
        for hf in range(tw // LANE):
            u_ref[g, :, hf * LANE:(hf + 1) * LANE] = place([(steps[hf * per_slab + tt], g) for tt in range(per_slab)])
    for p in range(npair):
        up = jnp.concatenate([u_ref[2 * p].astype(BF16), u_ref[2 * p + 1].astype(BF16)], axis=1)
        er_s[p] = jnp.dot(up, ere_ref[p], preferred_element_type=F32)
        ei_s[p] = jnp.dot(up, eim_ref[p], preferred_element_type=F32)
    row = lax.broadcasted_iota(jnp.int32, (SUBLANE, LANE), 0)
    groups_per_seq = nb_seq // SUBLANE

    def body(r, carry):
        off = pl.multiple_of(r * SUBLANE, SUBLANE)
        keep = jnp.where((r % groups_per_seq) == 0, 0.0, 1.0)
        new = []
        for p in range(npair):
            xr = er_s[p, pl.ds(off, SUBLANE), :]
            xi = ei_s[p, pl.ds(off, SUBLANE), :]
            for idx, k in enumerate((1, 2, 4)):
                ar, ai = coef_ref[p, 2 * idx], coef_ref[p, 2 * idx + 1]
                sr = jnp.where(row >= k, pltpu.roll(xr, k, 0), 0.0)
                si = jnp.where(row >= k, pltpu.roll(xi, k, 0), 0.0)
                xr, xi = xr + ar * sr - ai * si, xi + ar * si + ai * sr
            cr = jnp.broadcast_to(carry[p][0] * keep, (SUBLANE, LANE))
            ci = jnp.broadcast_to(carry[p][1] * keep, (SUBLANE, LANE))
            pwr, pwi = coef_ref[p, 6], coef_ref[p, 7]
            xr, xi = xr + pwr * cr - pwi * ci, xi + pwr * ci + pwi * cr
            hr_s[p, pl.ds(off, SUBLANE), :] = jnp.where(row == 0, cr, pltpu.roll(xr, 1, 0))
            hi_s[p, pl.ds(off, SUBLANE), :] = jnp.where(row == 0, ci, pltpu.roll(xi, 1, 0))
            new.append((xr[SUBLANE - 1:, :], xi[SUBLANE - 1:, :]))
        return tuple(new)

    zero = jnp.zeros((1, LANE), F32)
    lax.fori_loop(0, nb // SUBLANE, body, tuple((zero, zero) for _ in range(npair)))
    for p in range(npair):
        yc = (jnp.dot(hr_s[p].astype(BF16), fre_ref[p], preferred_element_type=F32)
              + jnp.dot(hi_s[p].astype(BF16), fim_ref[p], preferred_element_type=F32))
        for a in range(2):
            g = 2 * p + a
            u = u_ref[g]
            y = jnp.dot(u.astype(BF16), toep_ref[g], preferred_element_type=F32) + yc[:, a * tw:(a + 1) * tw]
            y_ref[g] = jax.nn.gelu(y + d_ref[g] * u, approximate=True)
    for t in range(t_blk):
        hf, tt = divmod(t, per_slab)
        o_ref[pl.ds(t, nb, stride=t_blk), :] = place(
            [(y_ref[g, :, hf * LANE:(hf + 1) * LANE], tt) for g in range(gps)])


def _s5t_params(a_re, a_im, log_step, b_re, b_im, c_re, c_im, d_skip):
    g, p = a_re.shape
    t = S5_BLOCK
    lam = lax.complex(a_re.astype(F32), a_im.astype(F32))
    lam_dt = lam * jnp.exp(log_step.astype(F32))[:, None]
    lam_bar = jnp.exp(lam_dt)
    b_bar = ((lam_bar - 1.0) / lam)[..., None] * lax.complex(b_re.astype(F32), b_im.astype(F32))
    c_c = lax.complex(c_re.astype(F32), c_im.astype(F32))
    tau = jnp.arange(t, dtype=F32)
    apow = jnp.exp(lam_dt[None] * tau[:, None, None])
    apow1 = jnp.exp(lam_dt[None] * (tau + 1.0)[:, None, None])
    kern = jnp.real(jnp.einsum('gip,tgp,gpj->gtij', c_c, apow, b_bar, precision=HI))
    lag = jnp.arange(t)[None, :] - jnp.arange(t)[:, None]
    kst = jnp.where((lag >= 0)[None, :, :, None, None], kern[:, jnp.clip(lag, 0, t - 1)], 0.0)
    toep = jnp.transpose(kst, (0, 1, 4, 2, 3)).reshape(g, t * S5_GROUP, t * S5_GROUP)
    e_c = (apow[::-1][:, :, None, :] * jnp.moveaxis(b_bar, 2, 1)[None]).transpose(1, 0, 2, 3)
    e_c = e_c.reshape(g, t * S5_GROUP, p)
    f_c = (c_c[:, None] * apow1.transpose(1, 0, 2)[:, :, None, :])
    f_c = jnp.moveaxis(f_c.reshape(g, t * S5_GROUP, p), 1, 2)
    eye2 = jnp.eye(2, dtype=F32)

    def pair_in(x):
        return jnp.einsum('narp,ab->narbp', x.reshape(g // 2, 2, t * S5_GROUP, p), eye2).reshape(
            g // 2, 2 * t * S5_GROUP, 2 * p)

    def pair_out(x):
        return jnp.einsum('napc,ab->napbc', x.reshape(g // 2, 2, p, t * S5_GROUP), eye2).reshape(
            g // 2, 2 * p, 2 * t * S5_GROUP)

    def planes(z):
        z = jnp.broadcast_to(z, (g, SUBLANE, p)).reshape(g // 2, 2, SUBLANE, p).transpose(0, 2, 1, 3)
        z = z.reshape(g // 2, SUBLANE, 2 * p)
        return [jnp.real(z), jnp.imag(z)]

    coef = []
    for k in (1, 2, 4):
        coef += planes(jnp.exp(lam_dt * float(t * k))[:, None, :])
    coef += planes(jnp.exp(lam_dt[:, None, :] * (float(t) * jnp.arange(1, SUBLANE + 1, dtype=F32))[None, :, None]))
    dtile = jnp.tile(d_skip.astype(F32).reshape(g, 1, S5_GROUP), (1, 1, t))
    return (toep.astype(BF16), pair_in(jnp.real(e_c)).astype(BF16), pair_in(jnp.imag(e_c)).astype(BF16),
            pair_out(jnp.real(f_c)).astype(BF16), pair_out(-jnp.imag(f_c)).astype(BF16),
            jnp.stack(coef, axis=1), dtile)


def _s5_toeplitz(u_src, col0, params, batch, seq):
    toep, ere, eim, fre, fim, coef, dtile = params
    g = toep.shape[0]
    t = S5_BLOCK
    tw = t * S5_GROUP
    ns = 2 * S5_STATE
    m = u_src.shape[0]
    nb = m // t
    gps = LANE // S5_GROUP
    npair = gps // 2
    cb0 = col0 // LANE
    est = 4 * m * LANE * 4 + 2 * gps * nb * tw * 4 + 4 * npair * nb * ns * 4 + 16 * gps * tw * tw * 2 + 12 * nb * tw * 4
    return pl.pallas_call(
        functools.partial(_s5t_kernel, gps=gps, nb=nb, nb_seq=seq // t),
        grid=(g // gps,),
        in_specs=[pl.BlockSpec((m, LANE), lambda i: (0, cb0 + i)),
                  pl.BlockSpec((gps, tw, tw), lambda i: (i, 0, 0)),
                  pl.BlockSpec((npair, 2 * tw, ns), lambda i: (i, 0, 0)),
                  pl.BlockSpec((npair, 2 * tw, ns), lambda i: (i, 0, 0)),
                  pl.BlockSpec((npair, ns, 2 * tw), lambda i: (i, 0, 0)),
                  pl.BlockSpec((npair, ns, 2 * tw), lambda i: (i, 0, 0)),
                  pl.BlockSpec((npair, 8, SUBLANE, ns), lambda i: (i, 0, 0, 0)),
                  pl.BlockSpec((gps, 1, tw), lambda i: (i, 0, 0))],
        out_specs=pl.BlockSpec((m, LANE), lambda i: (0, i)),
        out_shape=jax.ShapeDtypeStruct((m, g * S5_GROUP), F32),
        scratch_shapes=([pltpu.VMEM((gps, nb, tw), F32) for _ in range(2)]
                        + [pltpu.VMEM((npair, nb, ns), F32) for _ in range(4)]),
        compiler_params=_cparams(("parallel",), est),
        name="s5_blocks",
    )(u_src, toep, ere, eim, fre, fim, coef, dtile)


def _glu_kernel(a_ref, w_ref, g_ref, o_ref):
    z = jnp.dot(a_ref[...].astype(BF16), w_ref[...], preferred_element_type=F32)
    o_ref[...] = (g_ref[...] * jax.nn.sigmoid(z)).astype(o_ref.dtype)


def _s5_glu(g, w_glu):
    m, k = g.shape
    bm, bn, est = _mm_tiles(m, [k], k, [4], 2, True)
    return pl.pallas_call(
        _glu_kernel,
        grid=(m // bm, k // bn),
        in_specs=[pl.BlockSpec((bm, k), lambda i, j: (i, 0)),
                  pl.BlockSpec((k, bn), lambda i, j: (0, j)),
                  pl.BlockSpec((bm, bn), lambda i, j: (i, j))],
        out_specs=pl.BlockSpec((bm, bn), lambda i, j: (i, j)),
        out_shape=jax.ShapeDtypeStruct((m, k), BF16),
        compiler_params=_cparams(("parallel", "parallel"), est),
        name="s5_glu",
    )(g, w_glu, g)


def _shift_rows(h, prev, k):
    r = pltpu.roll(h, k, 0)
    p = pltpu.roll(prev, k, 0)
    row = lax.broadcasted_iota(jnp.int32, prev.shape, 0)
    top = jnp.where(row < k, p, r[:SUBLANE])
    return jnp.concatenate([top, r[SUBLANE:]], axis=0)


def _ffn_up_kernel(a_ref, wa_ref, wb_ref, cwa_ref, cwb_ref, o_ref, hbuf, carry, *, tiles_per_seq, nj):
    i, j = pl.program_id(0), pl.program_id(1)
    slot = i % 2
    rows = hbuf.shape[1]

    @pl.when(jnp.logical_and(i == 0, j == 0))
    def _():
        hbuf[...] = jnp.zeros_like(hbuf)
        carry[...] = jnp.zeros_like(carry)

    def epilogue():
        jp = jnp.maximum(j - 1, 0)
        keep = jnp.where((i % tiles_per_seq) == 0, 0.0, 1.0)
        outs = []
        for s, cw_ref in enumerate((cwa_ref, cwb_ref)):
            h = hbuf[s]
            prev = carry[1 - slot, jp, s * SUBLANE:(s + 1) * SUBLANE, :] * keep
            cw = cw_ref[...]
            outs.append(cw[2:3] * h + cw[1:2] * _shift_rows(h, prev, 1) + cw[0:1] * _shift_rows(h, prev, 2))
        o_ref[...] = (jax.nn.silu(outs[0]) * outs[1]).astype(o_ref.dtype)

    @pl.when(j < nj)
    def _():
        epilogue()
        a = a_ref[...]
        ha = jnp.dot(a, wa_ref[...].astype(BF16), preferred_element_type=F32)
        hb = jnp.dot(a, wb_ref[...].astype(BF16), preferred_element_type=F32)
        hbuf[0] = ha
        hbuf[1] = hb
        carry[slot, j, 0:SUBLANE, :] = ha[rows - SUBLANE:, :]
        carry[slot, j, SUBLANE:2 * SUBLANE, :] = hb[rows - SUBLANE:, :]

    @pl.when(j == nj)
    def _():
        epilogue()


def _ffn_up(xn, w_up, layer, conv_w, seq):
    m, k = xn.shape
    f = w_up.shape[2] // 2
    bm = _pick(seq, 1024, SUBLANE)
    bn = _pick(f, 256, LANE)
    nj = f // bn
    est = (2 * bm * k * 2 + 4 * k * bn * 4 + 2 * k * bn * 2 + 2 * bm * bn * 2 + 2 * nj * 16 * bn * 4
           + 2 * bm * bn * 4 + 12 * bm * bn * 4)
    cur = lambda j: jnp.minimum(j, nj - 1)
    prv = lambda j: jnp.maximum(j - 1, 0)
    return pl.pallas_call(
        functools.partial(_ffn_up_kernel, tiles_per_seq=seq // bm, nj=nj),
        grid=(m // bm, nj + 1),
        in_specs=[pl.BlockSpec((bm, k), lambda i, j: (i, 0)),
                  pl.BlockSpec((None, k, bn), lambda i, j: (layer, 0, cur(j))),
                  pl.BlockSpec((None, k, bn), lambda i, j: (layer, 0, cur(j) + nj)),
                  pl.BlockSpec((SUBLANE, bn), lambda i, j: (0, prv(j))),
                  pl.BlockSpec((SUBLANE, bn), lambda i, j: (0, prv(j) + nj))],
        out_specs=pl.BlockSpec((bm, bn), lambda i, j: (i, prv(j))),
        out_shape=jax.ShapeDtypeStruct((m, f), BF16),
        scratch_shapes=[pltpu.VMEM((2, bm, bn), F32), pltpu.VMEM((2, nj, 2 * SUBLANE, bn), F32)],
        compiler_params=_cparams(("arbitrary", "arbitrary"), est),
        name="ffn_up",
    )(xn, w_up, w_up, conv_w, conv_w)


def _gdn_pre_kernel(x_ref, w_ref, o_ref, *, nq, nk):
    j = pl.program_id(1)
    x = x_ref[...]
    w = w_ref[...]
    kw = 4
    row = lax.broadcasted_iota(jnp.int32, x.shape, 0)
    acc = w[kw - 1:kw] * x
    for s in range(1, kw):
        acc = acc + w[kw - 1 - s:kw - s] * jnp.where(row >= s, pltpu.roll(x, s, 0), 0.0)
    y = jax.nn.silu(acc)
    parts = []
    for c in range(x.shape[1] // GDN_DK):
        yh = y[:, c * GDN_DK:(c + 1) * GDN_DK]
        parts.append(yh * lax.rsqrt(jnp.sum(yh * yh, axis=-1, keepdims=True) + NORM_EPS))
    yn = parts[0] if len(parts) == 1 else jnp.concatenate(parts, axis=1)
    is_q = j < nq
    is_k = jnp.logical_and(j >= nq, j < nq + nk)
    o_ref[...] = jnp.where(is_q, yn * (GDN_DK ** -0.5), jnp.where(is_k, yn, y))


def _gdn_pre(src, conv_w, batch, seq, heads):
    width = conv_w.shape[1]
    cb = GDN_DK
    nq = heads * GDN_DK // cb
    est = 4 * seq * cb * 4 + 8 * seq * cb * 4
    return pl.pallas_call(
        functools.partial(_gdn_pre_kernel, nq=nq, nk=nq),
        grid=(batch, width // cb),
        in_specs=[pl.BlockSpec((seq, cb), lambda b, j: (b, j)),
                  pl.BlockSpec((SUBLANE, cb), lambda b, j: (0, j))],
        out_specs=pl.BlockSpec((seq, cb), lambda b, j: (b, j)),
        out_shape=jax.ShapeDtypeStruct((batch * seq, width), F32),
        compiler_params=_cparams(("parallel", "parallel"), est),
        name="gdn_pre",
    )(src, conv_w)


GDN_ROWS = 256
GDN_HEADS_PER_STEP = 2


def _dot_hi(a, b):
    return jnp.dot(a, b, precision=HI, preferred_element_type=F32)


def _bdot(a, b):
    return jnp.dot(a.astype(BF16), b.astype(BF16), preferred_element_type=F32)


def _bdot_nt(a, b):
    return lax.dot_general(a.astype(BF16), b.astype(BF16), (((1,), (1,)), ((), ())), preferred_element_type=F32)


def _bdot_tn(a, b):
    return lax.dot_general(a.astype(BF16), b.astype(BF16), (((0,), (0,)), ((), ())), preferred_element_type=F32)


def _gdn_head(alog, dtb, q, k, v, a_raw, b_raw, gate, ng, s, masks, rows):
    tril_f, same_f, tril, strict, eye_f = masks
    a_raw = a_raw + dtb
    softplus = jnp.maximum(a_raw, 0.0) + jnp.log(1.0 + jnp.exp(-jnp.abs(a_raw)))
    g = -jnp.exp(jnp.zeros_like(a_raw) + alog) * softplus
    beta = jax.nn.sigmoid(b_raw)
    gb = jnp.broadcast_to(g, (rows, LANE))
    gc = _dot_hi(tril_f, gb)
    gl = _dot_hi(same_f, gb)
    gcol = jnp.concatenate([gc] * (rows // LANE), axis=1) if rows > LANE else gc[:, :rows]
    grow = jnp.transpose(gcol)
    decay = jnp.exp(jnp.where(tril, gcol - grow, MASK_VALUE))
    kb = k * beta
    vb = v * beta
    lmat = jnp.where(strict, _bdot_nt(kb, k) * decay, 0.0)
    tinv = eye_f - lmat
    pw = lmat
    for _ in range(int(math.log2(CHUNK)) - 1):
        pw = _bdot(pw, pw)
        tinv = tinv + _bdot(tinv, pw)
    eg = jnp.exp(gc)
    sol = _bdot(tinv, jnp.concatenate([vb, kb * eg], axis=1))
    u_c, w_c = sol[:, :GDN_DV], sol[:, GDN_DV:]
    intra = _bdot_nt(q, k) * decay
    q_dec = q * eg
    k_dec = k * jnp.exp(gl - gc)
    egl = jnp.exp(gl)
    iuw = _bdot(intra, sol)
    o_loc = iuw[:, :GDN_DV]
    q_eff = q_dec - iuw[:, GDN_DV:]
    outs = []
    for c in range(rows // CHUNK):
        lo, hi = c * CHUNK, (c + 1) * CHUNK
        kuw = _bdot_tn(k_dec[lo:hi], sol[lo:hi])
        outs.append(_bdot(q_eff[lo:hi], s) + o_loc[lo:hi])
        s = s * egl[lo:lo + 1, :] + kuw[:, :GDN_DV] - _bdot(kuw[:, GDN_DV:], s)
    o = jnp.concatenate(outs, axis=0)
    o = o * lax.rsqrt(jnp.mean(o * o, axis=-1, keepdims=True) + NORM_EPS) * ng
    return o * jax.nn.silu(gate), s


def _gdn_kernel(alog_ref, dtb_ref, q_ref, k_ref, v_ref, ab_ref, gate_ref, ng_ref, o_ref, state, *,
                rows, hps, heads):
    hb = pl.program_id(1)
    r = pl.program_id(2)
    ab = ab_ref[...]
    ab_lane = lax.broadcasted_iota(jnp.int32, ab.shape, 1)

    @pl.when(r == 0)
    def _():
        state[...] = jnp.zeros_like(state)

    ri = lax.broadcasted_iota(jnp.int32, (rows, rows), 0)
    ci = lax.broadcasted_iota(jnp.int32, (rows, rows), 1)
    same = (ri // CHUNK) == (ci // CHUNK)
    tril = jnp.logical_and(same, ci <= ri)
    strict = jnp.logical_and(same, ci < ri)
    masks = (tril.astype(F32), same.astype(F32), tril, strict, jnp.where(ri == ci, 1.0, 0.0))
    ng = ng_ref[...]
    for i in range(hps):
        cs = slice(i * GDN_DK, (i + 1) * GDN_DK)
        h = hb * hps + i
        a_raw = jnp.sum(jnp.where(ab_lane == h, ab, 0.0), axis=1, keepdims=True)
        b_raw = jnp.sum(jnp.where(ab_lane == heads + h, ab, 0.0), axis=1, keepdims=True)
        o, s = _gdn_head(alog_ref[h], dtb_ref[h], q_ref[:, cs], k_ref[:, cs], v_ref[:, cs], a_raw, b_raw,
                         gate_ref[:, cs], ng, state[i], masks, rows)
        state[i] = s
        o_ref[:, cs] = o.astype(o_ref.dtype)


def _gdn_core(qkv, ab, gate_src, gate_col0, a_log, dt_bias, norm_g, batch, seq, heads):
    rows = _pick(seq, GDN_ROWS, CHUNK)
    nr = seq // rows
    hps = GDN_HEADS_PER_STEP if heads % GDN_HEADS_PER_STEP == 0 else 1
    nhb = heads // hps
    wblk = hps * GDN_DK
    gcb = gate_col0 // wblk
    est = hps * (12 * rows * LANE * 4 + 16 * rows * rows * 4 + 8 * rows * LANE * 4)
    tok = lambda b, h, r: b * nr + r
    return pl.pallas_call(
        functools.partial(_gdn_kernel, rows=rows, hps=hps, heads=heads),
        grid=(batch, nhb, nr),
        in_specs=[pl.BlockSpec(memory_space=pltpu.SMEM),
                  pl.BlockSpec(memory_space=pltpu.SMEM),
                  pl.BlockSpec((rows, wblk), lambda b, h, r: (tok(b, h, r), h)),
                  pl.BlockSpec((rows, wblk), lambda b, h, r: (tok(b, h, r), nhb + h)),
                  pl.BlockSpec((rows, wblk), lambda b, h, r: (tok(b, h, r), 2 * nhb + h)),
                  pl.BlockSpec((rows, LANE), lambda b, h, r: (tok(b, h, r), 0)),
                  pl.BlockSpec((rows, wblk), lambda b, h, r: (tok(b, h, r), gcb + h)),
                  pl.BlockSpec((1, GDN_DV), lambda b, h, r: (0, 0))],
        out_specs=pl.BlockSpec((rows, wblk), lambda b, h, r: (tok(b, h, r), h)),
        out_shape=jax.ShapeDtypeStruct((batch * seq, heads * GDN_DV), BF16),
        scratch_shapes=[pltpu.VMEM((hps, GDN_DK, GDN_DV), F32)],
        compiler_params=_cparams(("parallel", "parallel", "arbitrary"), est),
        name="gdn_core",
    )(a_log.astype(F32), dt_bias.astype(F32), qkv, qkv, qkv, ab, gate_src,
      norm_g.reshape(1, GDN_DV).astype(F32))


SB_TQ = 512
SB_TK = 256


def _sb_kernel(q_ref, kt_ref, v_ref, o_ref, acc_s, c_s, *, tq, tk):
    i = pl.program_id(2)
    r = tq // tk
    q = q_ref[...]
    ur = lax.broadcasted_iota(jnp.int32, (2 * tk, tk), 0) % tk
    uc = lax.broadcasted_iota(jnp.int32, (2 * tk, tk), 1)
    upper2 = jnp.where(ur > uc, 1.0, 0.0).astype(BF16)
    ri = lax.broadcasted_iota(jnp.int32, (tq, tk), 0)
    ci = lax.broadcasted_iota(jnp.int32, (tq, tk), 1)
    acc_s[...] = jnp.zeros_like(acc_s)
    c_s[...] = jnp.zeros_like(c_s)

    def logits(j):
        z = jnp.dot(q, kt_ref[j], preferred_element_type=F32)
        return z, -(jnp.maximum(z, 0.0) + jnp.log(1.0 + jnp.exp2(-jnp.abs(z))) * LOG2E)

    def weights(z, lm, c):
        hi = lm.astype(BF16)
        lo = (lm - hi.astype(F32)).astype(BF16)
        rest = jnp.dot(jnp.concatenate([hi, lo], axis=1), upper2, preferred_element_type=F32)
        return jnp.exp2(z + lm + rest + c)

    def values(j):
        return v_ref[pl.ds(pl.multiple_of(j * tk, tk), tk), :]

    for d in range(r - 1, -1, -1):
        j = i * r + d
        before = (ci + d * tk) < ri
        z, lm = logits(j)
        lm = jnp.where(before, lm, 0.0)
        w = jnp.where(before, weights(z, lm, c_s[...]), 0.0)
        acc_s[...] += jnp.dot(w.astype(BF16), values(j), preferred_element_type=F32)
        c_s[...] += jnp.sum(lm, axis=1, keepdims=True)

    def single(j):
        z, lm = logits(j)
        w = weights(z, lm, c_s[...])
        acc_s[...] += jnp.dot(w.astype(BF16), values(j), preferred_element_type=F32)
        c_s[...] += jnp.sum(lm, axis=1, keepdims=True)

    def pair(p, carry):
        ja = i * r - 1 - 2 * p
        jb = ja - 1
        za, lma = logits(ja)
        zb, lmb = logits(jb)
        c = c_s[...]
        ta = jnp.sum(lma, axis=1, keepdims=True)
        wa = weights(za, lma, c)
        wb = weights(zb, lmb, c + ta)
        acc_s[...] += (jnp.dot(wa.astype(BF16), values(ja), preferred_element_type=F32)
                       + jnp.dot(wb.astype(BF16), values(jb), preferred_element_type=F32))
        c_s[...] = c + ta + jnp.sum(lmb, axis=1, keepdims=True)
        return carry

    n_full = i * r
    lax.fori_loop(0, n_full // 2, pair, 0)
    if r % 2 == 1:
        @pl.when(n_full % 2 == 1)
        def _():
            single(0)
    o_ref[...] = acc_s[...].astype(o_ref.dtype)


def _sb_attention(qkv, batch, seq, heads):
    tq = _pick(seq, SB_TQ, LANE)
    tk = _pick(tq, SB_TK, LANE)
    nq, nk = seq // tq, seq // tk
    kt = jnp.transpose(qkv.reshape(batch, nk, tk, 3, heads, SB_D)[:, :, :, 1], (0, 3, 1, 4, 2))
    est = 4 * tq * SB_D * 2 + 4 * seq * SB_D * 2 + 2 * tq * SB_D * 4 + 24 * tq * tk * 4
    return pl.pallas_call(
        functools.partial(_sb_kernel, tq=tq, tk=tk),
        grid=(batch, heads, nq),
        in_specs=[pl.BlockSpec((tq, SB_D), lambda b, h, i: (b * nq + i, h)),
                  pl.BlockSpec((None, None, nk, SB_D, tk), lambda b, h, i: (b, h, 0, 0, 0)),
                  pl.BlockSpec((seq, SB_D), lambda b, h, i: (b, 2 * heads + h))],
        out_specs=pl.BlockSpec((tq, SB_D), lambda b, h, i: (b * nq + i, h)),
        out_shape=jax.ShapeDtypeStruct((batch * seq, heads * SB_D), BF16),
        scratch_shapes=[pltpu.VMEM((tq, SB_D), F32), pltpu.VMEM((tq, 1), F32)],
        compiler_params=_cparams(("parallel", "parallel", "arbitrary"), est),
        name="sb_attention",
    )(qkv, kt, qkv)


def _pad_cols(w, n):
    return jnp.pad(w, ((0, 0), (0, n - w.shape[1])))


def _ffn(h, ln_g, w_up, layer, conv_w, w_down_bf16, seq):
    xn = _rmsnorm(h, ln_g, BF16)
    conv8 = jnp.pad(conv_w.astype(F32), ((0, SUBLANE - conv_w.shape[0]), (0, 0)))
    act = _ffn_up(xn, w_up, layer, conv8, seq)
    return _matmul([(act, w_down_bf16)], F32, residual=h, b_layer=layer)


def _even_layer(h, ln_g, w_in, w_out, q_norm, kv_norm, w_uq, w_ukv,
                a_re, a_im, log_step, b_re, b_im, c_re, c_im, d_skip, w_glu, batch, seq):
    d = h.shape[1]
    q_rank, kv_rank = q_norm.shape[0], kv_norm.shape[0]
    heads = w_uq.shape[1] // (MLA_NOPE + MLA_ROPE)
    s5_width = d_skip.shape[0]
    o_kr = q_rank + kv_rank
    o_s5 = o_kr + MLA_ROPE
    latent_w = -(-(o_kr + LANE) // s5_width) * s5_width if s5_width >= LANE else o_kr + LANE
    w_lat = _pad_cols(w_in[:, :o_s5], latent_w)
    w_cat = jnp.concatenate([w_lat, w_in[:, o_s5:]], axis=1).astype(BF16)
    hn = _rmsnorm(h, ln_g, BF16)
    proj = _matmul([(hn, w_cat)], F32)
    cq = _rmsnorm(proj, q_norm, BF16, col0=0, width=q_rank)
    ckv = _rmsnorm(proj, kv_norm, BF16, col0=q_rank, width=kv_rank)
    q_scale = (MLA_NOPE + MLA_ROPE) ** -0.5 * LOG2E
    w_uq_p = jnp.pad(w_uq.reshape(q_rank, heads, MLA_NOPE + MLA_ROPE) * q_scale,
                     ((0, 0), (0, 0), (0, MLA_QK - MLA_NOPE - MLA_ROPE))).reshape(q_rank, heads * MLA_QK)
    q = _matmul([(cq, w_uq_p.astype(BF16))], F32)
    kv = _matmul([(ckv, w_ukv.astype(BF16))], BF16)
    cosf, sinf = _rope_tables(seq)
    kr = _rope_k(proj, o_kr, cosf, sinf, seq)
    out_a = _mla_attention(q, kv, kr, cosf, sinf, batch, seq, heads)
    params = _s5t_params(a_re, a_im, log_step, b_re, b_im, c_re, c_im, d_skip)
    g = _s5_toeplitz(proj, latent_w, params, batch, seq)
    out_b = _s5_glu(g, w_glu.astype(BF16))
    na = heads * MLA_V
    return _matmul([(out_a, w_out[:na].astype(BF16)), (out_b, w_out[na:].astype(BF16))], F32, residual=h)


def _odd_layer(h, ln_g, w_in, w_out, conv_w, a_log, dt_bias, norm_g, batch, seq):
    heads = a_log.shape[0]
    qkv_w = heads * (2 * GDN_DK + GDN_DV)
    o2 = qkv_w + 2 * heads
    o3 = o2 + heads * GDN_DV
    hn = _rmsnorm(h, ln_g, BF16)
    w_main = jnp.concatenate([w_in[:, :qkv_w], w_in[:, o2:o3]], axis=1).astype(BF16)
    w_ab = _pad_cols(w_in[:, qkv_w:o2], LANE).astype(BF16)
    sb_heads = (w_in.shape[1] - o3) // (3 * SB_D)
    nsq = sb_heads * SB_D
    w_sb = jnp.concatenate([w_in[:, o3:o3 + nsq] * (SB_D ** -0.5 * LOG2E), w_in[:, o3 + nsq:]], axis=1).astype(BF16)
    main = _matmul([(hn, w_main)], F32)
    ab = _matmul([(hn, w_ab)], F32)
    sb = _matmul([(hn, w_sb)], BF16)
    conv8 = jnp.pad(conv_w.astype(F32), ((0, SUBLANE - conv_w.shape[0]), (0, 0)))
    qkv = _gdn_pre(main, conv8, batch, seq, heads)
    out_c = _gdn_core(qkv, ab, main, qkv_w, a_log, dt_bias, norm_g, batch, seq, heads)
    out_d = _sb_attention(sb, batch, seq, sb_heads)
    nc = heads * GDN_DV
    return _matmul([(out_c, w_out[:nc].astype(BF16)), (out_d, w_out[nc:].astype(BF16))], F32, residual=h)


def kernel(x, ln_mix, ln_ffn, ln_final, ffn_w_up, ffn_conv, ffn_w_down, ev_w_in, ev_w_out, mla_q_norm,
           mla_kv_norm, mla_w_uq, mla_w_ukv, s5_a_re, s5_a_im, s5_log_step, s5_b_re, s5_b_im, s5_c_re,
           s5_c_im, s5_d, s5_w_glu, od_w_in, od_w_out, gdn_conv, gdn_a_log, gdn_dt_bias, gdn_norm):
    batch, seq, d = x.shape
    h = x.reshape(batch * seq, d).astype(F32)
    ffn_w_up = ffn_w_up.astype(F32)
    w_down_bf16 = ffn_w_down.astype(BF16)
    for layer in range(ln_mix.shape[0]):
        i = layer // 2
        if layer % 2 == 0:
            h = _even_layer(h, ln_mix[layer], ev_w_in[i], ev_w_out[i], mla_q_norm[i], mla_kv_norm[i],
                            mla_w_uq[i], mla_w_ukv[i], s5_a_re[i], s5_a_im[i], s5_log_step[i], s5_b_re[i],
                            s5_b_im[i], s5_c_re[i], s5_c_im[i], s5_d[i], s5_w_glu[i], batch, seq)
        else:
            h = _odd_layer(h, ln_mix[layer], od_w_in[i], od_w_out[i], gdn_conv[i], gdn_a_log[i],
                           gdn_dt_bias[i], gdn_norm[i], batch, seq)
        h = _ffn(h, ln_ffn[layer], ffn_w_up, layer, ffn_conv[layer], w_down_bf16, seq)
    return _rmsnorm(h, ln_final, x.dtype).reshape(batch, seq, d)
```

```python
import functools
import math

import jax
import jax.numpy as jnp
from jax import lax
from jax.experimental import pallas as pl
from jax.experimental.pallas import tpu as pltpu

F32 = jnp.float32
BF16 = jnp.bfloat16
HI = lax.Precision.HIGHEST

CHUNK = 64
NORM_EPS = 1e-6
MLA_NOPE = 128
MLA_ROPE = 64
MLA_V = 128
MLA_QK = 256
ROPE_BASE = 10000.0
S5_GROUP = 16
S5_STATE = 64
GDN_DK = 128
GDN_DV = 128
SB_D = 128

LANE = 128
SUBLANE = 8
VMEM_LIMIT_MAX = 58 * 1024 * 1024
VMEM_LIMIT_MIN = 32 * 1024 * 1024
MASK_VALUE = -1e30
LOG2E = math.log2(math.e)


def _cparams(sem, vmem_est):
    limit = int(min(max(vmem_est * 5 // 4 + (4 << 20), VMEM_LIMIT_MIN), VMEM_LIMIT_MAX))
    return pltpu.CompilerParams(dimension_semantics=sem, vmem_limit_bytes=limit)


def _pick(n, pref, mult):
    if n <= pref:
        return n
    t = (pref // mult) * mult
    while t >= mult:
        if n % t == 0:
            return t
        t -= mult
    return n


def _rmsnorm_kernel(x_ref, g_ref, o_ref):
    x = x_ref[...].astype(F32)
    ms = jnp.mean(x * x, axis=-1, keepdims=True)
    o_ref[...] = (x * lax.rsqrt(ms + NORM_EPS) * g_ref[...]).astype(o_ref.dtype)


def _rmsnorm(x, g, out_dtype, *, col0=0, width=None):
    m = x.shape[0]
    width = x.shape[1] if width is None else width
    tm = _pick(m, 256, SUBLANE)
    cb = col0 // width
    est = 2 * tm * width * (x.dtype.itemsize + jnp.dtype(out_dtype).itemsize) + 4 * tm * width * 4
    return pl.pallas_call(
        _rmsnorm_kernel,
        grid=(m // tm,),
        in_specs=[pl.BlockSpec((tm, width), lambda i: (i, cb)),
                  pl.BlockSpec((1, width), lambda i: (0, 0))],
        out_specs=pl.BlockSpec((tm, width), lambda i: (i, 0)),
        out_shape=jax.ShapeDtypeStruct((m, width), out_dtype),
        compiler_params=_cparams(("parallel",), est),
        name="rmsnorm",
    )(x, g.reshape(1, width).astype(F32))


def _mm_kernel(*refs, n_pairs, has_res):
    o_ref = refs[-1]
    acc = None
    for p in range(n_pairs):
        a = refs[2 * p][...].astype(BF16)
        d = jnp.dot(a, refs[2 * p + 1][...], preferred_element_type=F32)
        acc = d if acc is None else acc + d
    if has_res:
        acc = acc + refs[2 * n_pairs][...]
    o_ref[...] = acc.astype(o_ref.dtype)


def _mm_tiles(m, ks, n, a_bytes, out_bytes, has_res, budget=52 << 20):
    for bm_p, bn_p in ((1024, 1024), (1024, 512), (512, 512), (512, 256), (256, 256), (256, 128), (128, 128)):
        bm = _pick(m, bm_p, SUBLANE)
        bn = _pick(n, bn_p, LANE)
        est = bm * bn * 4
        for k, ab in zip(ks, a_bytes):
            est += 2 * (bm * k * ab + k * bn * 2)
        est += 2 * bm * bn * (out_bytes + (4 if has_res else 0))
        if est <= budget:
            return bm, bn, est
    return bm, bn, est


def _matmul(pairs, out_dtype, residual=None, *, a_col0=None, b_layer=None):
    m = pairs[0][0].shape[0]
    n = pairs[0][1].shape[-1]
    ks = [b.shape[-2] for _, b in pairs]
    a_col0 = [0] * len(pairs) if a_col0 is None else a_col0
    a_bytes = [a.dtype.itemsize for a, _ in pairs]
    bm, bn, est = _mm_tiles(m, ks, n, a_bytes, jnp.dtype(out_dtype).itemsize, residual is not None)
    in_specs, args = [], []
    for (a, b), k, c0 in zip(pairs, ks, a_col0):
        cb = c0 // k
        in_specs.append(pl.BlockSpec((bm, k), lambda i, j, cb=cb: (i, cb)))
        if b_layer is None:
            in_specs.append(pl.BlockSpec((k, bn), lambda i, j: (0, j)))
        else:
            in_specs.append(pl.BlockSpec((None, k, bn), lambda i, j: (b_layer, 0, j)))
        args += [a, b]
    if residual is not None:
        in_specs.append(pl.BlockSpec((bm, bn), lambda i, j: (i, j)))
        args.append(residual)
    return pl.pallas_call(
        functools.partial(_mm_kernel, n_pairs=len(pairs), has_res=residual is not None),
        grid=(m // bm, n // bn),
        in_specs=in_specs,
        out_specs=pl.BlockSpec((bm, bn), lambda i, j: (i, j)),
        out_shape=jax.ShapeDtypeStruct((m, n), out_dtype),
        compiler_params=_cparams(("parallel", "parallel"), est),
        name="matmul",
    )(*args)


def _mm_t_kernel(a_ref, b_ref, o_ref, acc_ref, *, tk):
    acc_ref[...] = jnp.dot(a_ref[...].astype(BF16), b_ref[...], preferred_element_type=F32)
    res_t = jnp.transpose(acc_ref[...])
    for c in range(o_ref.shape[0]):
        o_ref[c] = res_t[:, c * tk:(c + 1) * tk].astype(o_ref.dtype)


def _matmul_keys_t(a, b, tk):
    m, k = a.shape
    n = b.shape[1]
    bm = _pick(m, 1024, tk)
    bn = _pick(n, 512, LANE)
    est = 2 * (bm * k * a.dtype.itemsize + k * bn * 2 + bm * bn * 2) + 3 * bm * bn * 4
    return pl.pallas_call(
        functools.partial(_mm_t_kernel, tk=tk),
        grid=(m // bm, n // bn),
        in_specs=[pl.BlockSpec((bm, k), lambda i, j: (i, 0)),
                  pl.BlockSpec((k, bn), lambda i, j: (0, j))],
        out_specs=pl.BlockSpec((bm // tk, bn, tk), lambda i, j: (i, j, 0)),
        out_shape=jax.ShapeDtypeStruct((m // tk, n, tk), BF16),
        scratch_shapes=[pltpu.VMEM((bm, bn), F32)],
        compiler_params=_cparams(("parallel", "parallel"), est),
        name="matmul_keys_t",
    )(a, b)


def _rope_tables(seq):
    half = MLA_ROPE // 2
    inv_freq = ROPE_BASE ** (-jnp.arange(half, dtype=F32) / half)
    ang = jnp.arange(seq, dtype=F32)[:, None] * inv_freq[None, :]
    cos, sin = jnp.cos(ang), jnp.sin(ang)
    zeros = jnp.zeros((seq, LANE - MLA_ROPE), F32)
    return (jnp.concatenate([cos, cos, zeros], axis=1),
            jnp.concatenate([-sin, sin, zeros], axis=1))


def _rope_lanes(x, cosf, sinf):
    half = MLA_ROPE // 2
    lane = lax.broadcasted_iota(jnp.int32, x.shape, 1)
    swapped = jnp.where(lane < half, pltpu.roll(x, LANE - half, 1), pltpu.roll(x, half, 1))
    return x * cosf + swapped * sinf


def _rope_kernel(x_ref, cos_ref, sin_ref, o_ref):
    o_ref[0] = jnp.transpose(_rope_lanes(x_ref[...], cos_ref[...], sin_ref[...])).astype(o_ref.dtype)


def _rope_k(x, col0, cosf, sinf, seq, tq):
    m = x.shape[0]
    per_seq = seq // tq
    cb = col0 // LANE
    return pl.pallas_call(
        _rope_kernel,
        grid=(m // tq,),
        in_specs=[pl.BlockSpec((tq, LANE), lambda i: (i, cb)),
                  pl.BlockSpec((tq, LANE), lambda i: (i % per_seq, 0)),
                  pl.BlockSpec((tq, LANE), lambda i: (i % per_seq, 0))],
        out_specs=pl.BlockSpec((1, LANE, tq), lambda i: (i, 0, 0)),
        out_shape=jax.ShapeDtypeStruct((m // tq, LANE, tq), BF16),
        compiler_params=_cparams(("parallel",), 16 * tq * LANE * 4),
        name="rope_k",
    )(x, cosf, sinf)


def _mla_kernel(q_ref, knt_ref, krt_ref, v_ref, cos_ref, sin_ref, o_ref, qs, m_s, l_s, acc_s, *, tq):
    i = pl.program_id(2)
    q = q_ref[...]
    qs[:, :MLA_NOPE] = q[:, :MLA_NOPE].astype(BF16)
    qs[:, MLA_NOPE:] = _rope_lanes(q[:, MLA_NOPE:], cos_ref[...], sin_ref[...]).astype(BF16)

    def tile(j):
        kt = jnp.concatenate([knt_ref[j], krt_ref[j]], axis=0)
        s = jnp.dot(qs[...], kt, preferred_element_type=F32)
        return s, v_ref[pl.ds(pl.multiple_of(j * tq, tq), tq), :]

    s, v = tile(i)
    rc = lax.broadcasted_iota(jnp.int32, (tq, tq), 0) // CHUNK
    cc = lax.broadcasted_iota(jnp.int32, (tq, tq), 1) // CHUNK
    s = jnp.where(cc <= rc, s, MASK_VALUE)
    m = jnp.max(s, axis=1, keepdims=True)
    p = jnp.exp2(s - m)
    m_s[...] = m
    l_s[...] = jnp.sum(p, axis=1, keepdims=True)
    acc_s[...] = jnp.dot(p.astype(BF16), v, preferred_element_type=F32)

    def body(j, carry):
        s, v = tile(j)
        m_prev = m_s[...]
        m_new = jnp.maximum(m_prev, jnp.max(s, axis=1, keepdims=True))
        alpha = jnp.exp2(m_prev - m_new)
        p = jnp.exp2(s - m_new)
        l_s[...] = alpha * l_s[...] + jnp.sum(p, axis=1, keepdims=True)
        acc_s[...] = alpha * acc_s[...] + jnp.dot(p.astype(BF16), v, preferred_element_type=F32)
        m_s[...] = m_new
        return carry

    lax.fori_loop(0, i, body, 0)
    o_ref[...] = (acc_s[...] / l_s[...]).astype(o_ref.dtype)


MLA_TILE = 1024


def _mla_tile(seq):
    return _pick(seq, MLA_TILE, CHUNK)


def _mla_attention(q, knt, krt, v, cosf, sinf, batch, seq, heads):
    tq = _mla_tile(seq)
    nq = seq // tq
    est = (2 * tq * MLA_QK * 4 + 2 * seq * (MLA_QK + LANE) * 2 + 4 * tq * LANE * 4 + 2 * tq * LANE * 2
           + tq * MLA_QK * 2 + 3 * tq * LANE * 4 + 6 * tq * tq * 4)
    return pl.pallas_call(
        functools.partial(_mla_kernel, tq=tq),
        grid=(batch, heads, nq),
        in_specs=[pl.BlockSpec((tq, MLA_QK), lambda b, h, i: (b * nq + i, h)),
                  pl.BlockSpec((nq, MLA_NOPE, tq), lambda b, h, i: (b, h, 0)),
                  pl.BlockSpec((nq, LANE, tq), lambda b, h, i: (b, 0, 0)),
                  pl.BlockSpec((seq, MLA_V), lambda b, h, i: (b, h)),
                  pl.BlockSpec((tq, LANE), lambda b, h, i: (i, 0)),
                  pl.BlockSpec((tq, LANE), lambda b, h, i: (i, 0))],
        out_specs=pl.BlockSpec((tq, MLA_V), lambda b, h, i: (b * nq + i, h)),
        out_shape=jax.ShapeDtypeStruct((batch * seq, heads * MLA_V), BF16),
        scratch_shapes=[pltpu.VMEM((tq, MLA_QK), BF16), pltpu.VMEM((tq, 1), F32),
                        pltpu.VMEM((tq, 1), F32), pltpu.VMEM((tq, MLA_V), F32)],
        compiler_params=_cparams(("parallel", "parallel", "arbitrary"), est),
        name="mla_attention",
    )(q, knt, krt, v, cosf, sinf)


S5_GROUPS_PER_BLOCK = 8


def _s5_kernel(u_ref, bre_ref, bim_ref, cre_ref, cim_ref, pw_ref, d_ref, o_ref, sre, sim, car, *, tt):
    t = pl.program_id(2)

    @pl.when(t == 0)
    def _():
        car[...] = jnp.zeros_like(car)

    u = u_ref[...]
    ub = u.astype(BF16)
    sre[...] = jnp.dot(ub, bre_ref[0], preferred_element_type=F32)
    sim[...] = jnp.dot(ub, bim_ref[0], preferred_element_type=F32)
    ns = sre.shape[1]
    row = lax.broadcasted_iota(jnp.int32, (SUBLANE, ns), 0)
    steps = [(1, pw_ref[0, 0], pw_ref[0, 1]), (2, pw_ref[0, 2], pw_ref[0, 3]), (4, pw_ref[0, 4], pw_ref[0, 5])]
    pwr, pwi = pw_ref[0, 6], pw_ref[0, 7]

    def body(r, carry):
        cr, ci = carry
        off = pl.multiple_of(r * SUBLANE, SUBLANE)
        xr = sre[pl.ds(off, SUBLANE), :]
        xi = sim[pl.ds(off, SUBLANE), :]
        for k, ar, ai in steps:
            sr = jnp.where(row >= k, pltpu.roll(xr, k, 0), 0.0)
            si = jnp.where(row >= k, pltpu.roll(xi, k, 0), 0.0)
            xr, xi = xr + ar * sr - ai * si, xi + ar * si + ai * sr
        xr, xi = xr + pwr * cr - pwi * ci, xi + pwr * ci + pwi * cr
        sre[pl.ds(off, SUBLANE), :] = xr
        sim[pl.ds(off, SUBLANE), :] = xi
        return xr[SUBLANE - 1:, :], xi[SUBLANE - 1:, :]

    cr, ci = lax.fori_loop(0, tt // SUBLANE, body, (car[0:1, :], car[1:2, :]))
    car[0:1, :] = cr
    car[1:2, :] = ci
    y = (jnp.dot(sre[...].astype(BF16), cre_ref[0], preferred_element_type=F32)
         - jnp.dot(sim[...].astype(BF16), cim_ref[0], preferred_element_type=F32))
    y = y + d_ref[...] * u
    o_ref[...] = jax.nn.gelu(y, approximate=True)


def _s5_params(a_re, a_im, log_step, b_re, b_im, c_re, c_im):
    g, p = a_re.shape
    gb = min(S5_GROUPS_PER_BLOCK, g)
    nb = g // gb
    lam = lax.complex(a_re.astype(F32), a_im.astype(F32))
    step = jnp.exp(log_step.astype(F32))[:, None]
    lam_dt = lam * step
    lam_bar = jnp.exp(lam_dt)
    b_bar = ((lam_bar - 1.0) / lam)[..., None] * lax.complex(b_re.astype(F32), b_im.astype(F32))
    c_c = lax.complex(c_re.astype(F32), c_im.astype(F32))
    eye = jnp.eye(gb, dtype=F32)

    def bd_in(x):
        x = x.reshape(nb, gb, p, S5_GROUP)
        return jnp.einsum('ngpi,gh->ngihp', x, eye).reshape(nb, gb * S5_GROUP, gb * p)

    def bd_out(x):
        x = x.reshape(nb, gb, S5_GROUP, p)
        return jnp.einsum('ngip,gh->ngphi', x, eye).reshape(nb, gb * p, gb * S5_GROUP)

    def rows(z):
        z = z.reshape(nb, 1, gb * p)
        return (jnp.broadcast_to(jnp.real(z), (nb, SUBLANE, gb * p)),
                jnp.broadcast_to(jnp.imag(z), (nb, SUBLANE, gb * p)))

    planes = []
    for k in (1, 2, 4):
        planes += list(rows(jnp.exp(lam_dt * float(k))))
    pw = jnp.exp(lam_dt[None] * jnp.arange(1, SUBLANE + 1, dtype=F32)[:, None, None])
    pw = jnp.moveaxis(pw.reshape(SUBLANE, nb, gb * p), 0, 1)
    planes += [jnp.real(pw), jnp.imag(pw)]
    return (bd_in(jnp.real(b_bar)).astype(BF16), bd_in(jnp.imag(b_bar)).astype(BF16),
            bd_out(jnp.real(c_c)).astype(BF16), bd_out(jnp.imag(c_c)).astype(BF16),
            jnp.stack(planes, axis=1), gb)


def _s5_scan(u_src, col0, params, d_skip, batch, seq):
    bre, bim, cre, cim, pw, gb = params
    nb = bre.shape[0]
    gc = gb * S5_GROUP
    ns = gb * S5_STATE
    tt = _pick(seq, 512, SUBLANE)
    nt = seq // tt
    cb0 = col0 // gc
    est = 4 * tt * gc * 4 + 8 * gc * ns * 4 + 4 * SUBLANE * 8 * ns * 4 + 6 * tt * ns * 4
    return pl.pallas_call(
        functools.partial(_s5_kernel, tt=tt),
        grid=(batch, nb, nt),
        in_specs=[pl.BlockSpec((tt, gc), lambda b, g, t: (b * nt + t, cb0 + g)),
                  pl.BlockSpec((1, gc, ns), lambda b, g, t: (g, 0, 0)),
                  pl.BlockSpec((1, gc, ns), lambda b, g, t: (g, 0, 0)),
                  pl.BlockSpec((1, ns, gc), lambda b, g, t: (g, 0, 0)),
                  pl.BlockSpec((1, ns, gc), lambda b, g, t: (g, 0, 0)),
                  pl.BlockSpec((1, 8, SUBLANE, ns), lambda b, g, t: (g, 0, 0, 0)),
                  pl.BlockSpec((1, gc), lambda b, g, t: (0, g))],
        out_specs=pl.BlockSpec((tt, gc), lambda b, g, t: (b * nt + t, g)),
        out_shape=jax.ShapeDtypeStruct((batch * seq, nb * gc), F32),
        scratch_shapes=[pltpu.VMEM((tt, ns), F32), pltpu.VMEM((tt, ns), F32), pltpu.VMEM((SUBLANE, ns), F32)],
        compiler_params=_cparams(("parallel", "parallel", "arbitrary"), est),
        name="s5_scan",
    )(u_src, bre, bim, cre, cim, pw, d_skip.reshape(1, -1).astype(F32))


S5_BLOCK = 16
S5_GROUPS_PER_STEP = 4


def _s5t_kernel(x_ref, toep_ref, ere_ref, eim_ref, fre_ref, fim_ref, coef_ref, d_ref, o_ref,
                u_ref, y_ref, er_s, ei_s, hr_s, hi_s, *, gps, nb, nb_seq):
    t_blk = S5_BLOCK
    tw = t_blk * S5_GROUP
    npair = gps // 2
    per_slab = LANE // S5_GROUP
    steps = [x_ref[pl.ds(t, nb, stride=t_blk), :] for t in range(t_blk)]
    win = lax.broadcasted_iota(jnp.int32, (nb, LANE), 1) // S5_GROUP
    in_win = [win == w for w in range(per_slab)]

    def place(pieces):
        out = None
        for w, (arr, src) in enumerate(pieces):
            sh = ((w - src) % per_slab) * S5_GROUP
            r = arr if sh == 0 else pltpu.roll(arr, sh, 1)
            out = r if out is None else jnp.where(in_win[w], r, out)
        return out

    for g in range(gps):
        for hf in range(tw // LANE):
            u_ref[g, :, hf * LANE:(hf + 1) * LANE] = place([(steps[hf * per_slab + tt], g) for tt in range(per_slab)])
    for p in range(npair):
        up = jnp.concatenate([u_ref[2 * p].astype(BF16), u_ref[2 * p + 1].astype(BF16)], axis=1)
        er_s[p] = jnp.dot(up, ere_ref[p], preferred_element_type=F32)
        ei_s[p] = jnp.dot(up, eim_ref[p], preferred_element_type=F32)
    row = lax.broadcasted_iota(jnp.int32, (SUBLANE, LANE), 0)
    groups_per_seq = nb_seq // SUBLANE

    def body(r, carry):
        off = pl.multiple_of(r * SUBLANE, SUBLANE)
        keep = jnp.where((r % groups_per_seq) == 0, 0.0, 1.0)
        new = []
        for p in range(npair):
            xr = er_s[p, pl.ds(off, SUBLANE), :]
            xi = ei_s[p, pl.ds(off, SUBLANE), :]
            for idx, k in enumerate((1, 2, 4)):
                ar, ai = coef_ref[p, 2 * idx], coef_ref[p, 2 * idx + 1]
                sr = jnp.where(row >= k, pltpu.roll(xr, k, 0), 0.0)
                si = jnp.where(row >= k, pltpu.roll(xi, k, 0), 0.0)
                xr, xi = xr + ar * sr - ai * si, xi + ar * si + ai * sr
            cr = jnp.broadcast_to(carry[p][0] * keep, (SUBLANE, LANE))
            ci = jnp.broadcast_to(carry[p][1] * keep, (SUBLANE, LANE))
            pwr, pwi = coef_ref[p, 6], coef_ref[p, 7]
            xr, xi = xr + pwr * cr - pwi * ci, xi + pwr * ci + pwi * cr
            hr_s[p, pl.ds(off, SUBLANE), :] = jnp.where(row == 0, cr, pltpu.roll(xr, 1, 0))
            hi_s[p, pl.ds(off, SUBLANE), :] = jnp.where(row == 0, ci, pltpu.roll(xi, 1, 0))
            new.append((xr[SUBLANE - 1:, :], xi[SUBLANE - 1:, :]))
        return tuple(new)

    zero = jnp.zeros((1, LANE), F32)
    lax.fori_loop(0, nb // SUBLANE, body, tuple((zero, zero) for _ in range(npair)))
    for p in range(npair):
        yc = (jnp.dot(hr_s[p].astype(BF16), fre_ref[p], preferred_element_type=F32)
              + jnp.dot(hi_s[p].astype(BF16), fim_ref[p], preferred_element_type=F32))
        for a in range(2):
            g = 2 * p + a
            u = u_ref[g]
            y = jnp.dot(u.astype(BF16), toep_ref[g], preferred_element_type=F32) + yc[:, a * tw:(a + 1) * tw]
            y_ref[g] = jax.nn.gelu(y + d_ref[g] * u, approximate=True)
    for t in range(t_blk):
        hf, tt = divmod(t, per_slab)
        o_ref[pl.ds(t, nb, stride=t_blk), :] = place(
            [(y_ref[g, :, hf * LANE:(hf + 1) * LANE], tt) for g in range(gps)])


def _s5t_params(a_re, a_im, log_step, b_re, b_im, c_re, c_im, d_skip):
    g, p = a_re.shape
    t = S5_BLOCK
    lam = lax.complex(a_re.astype(F32), a_im.astype(F32))
    lam_dt = lam * jnp.exp(log_step.astype(F32))[:, None]
    lam_bar = jnp.exp(lam_dt)
    b_bar = ((lam_bar - 1.0) / lam)[..., None] * lax.complex(b_re.astype(F32), b_im.astype(F32))
    c_c = lax.complex(c_re.astype(F32), c_im.astype(F32))
    tau = jnp.arange(t, dtype=F32)
    apow = jnp.exp(lam_dt[None] * tau[:, None, None])
    apow1 = jnp.exp(lam_dt[None] * (tau + 1.0)[:, None, None])
    kern = jnp.real(jnp.einsum('gip,tgp,gpj->gtij', c_c, apow, b_bar, precision=HI))
    lag = jnp.arange(t)[None, :] - jnp.arange(t)[:, None]
    shift = (lag[None] == jnp.arange(t)[:, None, None]).astype(F32)
    toep = jnp.einsum('gtij,tsu->gsjui', kern, shift, precision=HI)
    toep = toep.reshape(g, t * S5_GROUP, t * S5_GROUP)
    e_c = (apow[::-1][:, :, None, :] * jnp.moveaxis(b_bar, 2, 1)[None]).transpose(1, 0, 2, 3)
    e_c = e_c.reshape(g, t * S5_GROUP, p)
    f_c = (c_c[:, None] * apow1.transpose(1, 0, 2)[:, :, None, :])
    f_c = jnp.moveaxis(f_c.reshape(g, t * S5_GROUP, p), 1, 2)
    eye2 = jnp.eye(2, dtype=F32)

    def pair_in(x):
        return jnp.einsum('narp,ab->narbp', x.reshape(g // 2, 2, t * S5_GROUP, p), eye2).reshape(
            g // 2, 2 * t * S5_GROUP, 2 * p)

    def pair_out(x):
        return jnp.einsum('napc,ab->napbc', x.reshape(g // 2, 2, p, t * S5_GROUP), eye2).reshape(
            g // 2, 2 * p, 2 * t * S5_GROUP)

    def planes(z):
        z = jnp.broadcast_to(z, (g, SUBLANE, p)).reshape(g // 2, 2, SUBLANE, p).transpose(0, 2, 1, 3)
        z = z.reshape(g // 2, SUBLANE, 2 * p)
        return [jnp.real(z), jnp.imag(z)]

    coef = []
    for k in (1, 2, 4):
        coef += planes(jnp.exp(lam_dt * float(t * k))[:, None, :])
    coef += planes(jnp.exp(lam_dt[:, None, :] * (float(t) * jnp.arange(1, SUBLANE + 1, dtype=F32))[None, :, None]))
    dtile = jnp.tile(d_skip.astype(F32).reshape(g, 1, S5_GROUP), (1, 1, t))
    return (toep.astype(BF16), pair_in(jnp.real(e_c)).astype(BF16), pair_in(jnp.imag(e_c)).astype(BF16),
            pair_out(jnp.real(f_c)).astype(BF16), pair_out(-jnp.imag(f_c)).astype(BF16),
            jnp.stack(coef, axis=1), dtile)


def _s5_toeplitz(u_src, col0, params, batch, seq):
    toep, ere, eim, fre, fim, coef, dtile = params
    g = toep.shape[0]
    t = S5_BLOCK
    tw = t * S5_GROUP
    ns = 2 * S5_STATE
    m = u_src.shape[0]
    nb = m // t
    gps = LANE // S5_GROUP
    npair = gps // 2
    cb0 = col0 // LANE
    est = 4 * m * LANE * 4 + 2 * gps * nb * tw * 4 + 4 * npair * nb * ns * 4 + 16 * gps * tw * tw * 2 + 12 * nb * tw * 4
    return pl.pallas_call(
        functools.partial(_s5t_kernel, gps=gps, nb=nb, nb_seq=seq // t),
        grid=(g // gps,),
        in_specs=[pl.BlockSpec((m, LANE), lambda i: (0, cb0 + i)),
                  pl.BlockSpec((gps, tw, tw), lambda i: (i, 0, 0)),
                  pl.BlockSpec((npair, 2 * tw, ns), lambda i: (i, 0, 0)),
                  pl.BlockSpec((npair, 2 * tw, ns), lambda i: (i, 0, 0)),
                  pl.BlockSpec((npair, ns, 2 * tw), lambda i: (i, 0, 0)),
                  pl.BlockSpec((npair, ns, 2 * tw), lambda i: (i, 0, 0)),
                  pl.BlockSpec((npair, 8, SUBLANE, ns), lambda i: (i, 0, 0, 0)),
                  pl.BlockSpec((gps, 1, tw), lambda i: (i, 0, 0))],
        out_specs=pl.BlockSpec((m, LANE), lambda i: (0, i)),
        out_shape=jax.ShapeDtypeStruct((m, g * S5_GROUP), F32),
        scratch_shapes=([pltpu.VMEM((gps, nb, tw), F32) for _ in range(2)]
                        + [pltpu.VMEM((npair, nb, ns), F32) for _ in range(4)]),
        compiler_params=_cparams(("parallel",), est),
        name="s5_blocks",
    )(u_src, toep, ere, eim, fre, fim, coef, dtile)


def _glu_kernel(a_ref, w_ref, g_ref, o_ref):
    z = jnp.dot(a_ref[...].astype(BF16), w_ref[...], preferred_element_type=F32)
    o_ref[...] = (g_ref[...] * jax.nn.sigmoid(z)).astype(o_ref.dtype)


def _s5_glu(g, w_glu):
    m, k = g.shape
    bm, bn, est = _mm_tiles(m, [k], k, [4], 2, True)
    return pl.pallas_call(
        _glu_kernel,
        grid=(m // bm, k // bn),
        in_specs=[pl.BlockSpec((bm, k), lambda i, j: (i, 0)),
                  pl.BlockSpec((k, bn), lambda i, j: (0, j)),
                  pl.BlockSpec((bm, bn), lambda i, j: (i, j))],
        out_specs=pl.BlockSpec((bm, bn), lambda i, j: (i, j)),
        out_shape=jax.ShapeDtypeStruct((m, k), BF16),
        compiler_params=_cparams(("parallel", "parallel"), est),
        name="s5_glu",
    )(g, w_glu, g)


def _shift_rows(h, prev, k):
    r = pltpu.roll(h, k, 0)
    p = pltpu.roll(prev, k, 0)
    row = lax.broadcasted_iota(jnp.int32, prev.shape, 0)
    top = jnp.where(row < k, p, r[:SUBLANE])
    return jnp.concatenate([top, r[SUBLANE:]], axis=0)


def _ffn_up_kernel(a_ref, wa_ref, wb_ref, cwa_ref, cwb_ref, o_ref, hbuf, carry, *, tiles_per_seq, nj):
    i, j = pl.program_id(0), pl.program_id(1)
    slot = i % 2
    rows = hbuf.shape[1]

    @pl.when(jnp.logical_and(i == 0, j == 0))
    def _():
        hbuf[...] = jnp.zeros_like(hbuf)
        carry[...] = jnp.zeros_like(carry)

    def epilogue():
        jp = jnp.maximum(j - 1, 0)
        keep = jnp.where((i % tiles_per_seq) == 0, 0.0, 1.0)
        outs = []
        for s, cw_ref in enumerate((cwa_ref, cwb_ref)):
            h = hbuf[s]
            prev = carry[1 - slot, jp, s * SUBLANE:(s + 1) * SUBLANE, :] * keep
            cw = cw_ref[...]
            outs.append(cw[2:3] * h + cw[1:2] * _shift_rows(h, prev, 1) + cw[0:1] * _shift_rows(h, prev, 2))
        o_ref[...] = (jax.nn.silu(outs[0]) * outs[1]).astype(o_ref.dtype)

    @pl.when(j < nj)
    def _():
        epilogue()
        a = a_ref[...]
        ha = jnp.dot(a, wa_ref[...].astype(BF16), preferred_element_type=F32)
        hb = jnp.dot(a, wb_ref[...].astype(BF16), preferred_element_type=F32)
        hbuf[0] = ha
        hbuf[1] = hb
        carry[slot, j, 0:SUBLANE, :] = ha[rows - SUBLANE:, :]
        carry[slot, j, SUBLANE:2 * SUBLANE, :] = hb[rows - SUBLANE:, :]

    @pl.when(j == nj)
    def _():
        epilogue()


def _ffn_up(xn, w_up, layer, conv_w, seq):
    m, k = xn.shape
    f = w_up.shape[2] // 2
    bm = _pick(seq, 1024, SUBLANE)
    bn = _pick(f, 256, LANE)
    nj = f // bn
    est = (2 * bm * k * 2 + 4 * k * bn * 4 + 2 * k * bn * 2 + 2 * bm * bn * 2 + 2 * nj * 16 * bn * 4
           + 2 * bm * bn * 4 + 12 * bm * bn * 4)
    cur = lambda j: jnp.minimum(j, nj - 1)
    prv = lambda j: jnp.maximum(j - 1, 0)
    return pl.pallas_call(
        functools.partial(_ffn_up_kernel, tiles_per_seq=seq // bm, nj=nj),
        grid=(m // bm, nj + 1),
        in_specs=[pl.BlockSpec((bm, k), lambda i, j: (i, 0)),
                  pl.BlockSpec((None, k, bn), lambda i, j: (layer, 0, cur(j))),
                  pl.BlockSpec((None, k, bn), lambda i, j: (layer, 0, cur(j) + nj)),
                  pl.BlockSpec((SUBLANE, bn), lambda i, j: (0, prv(j))),
                  pl.BlockSpec((SUBLANE, bn), lambda i, j: (0, prv(j) + nj))],
        out_specs=pl.BlockSpec((bm, bn), lambda i, j: (i, prv(j))),
        out_shape=jax.ShapeDtypeStruct((m, f), BF16),
        scratch_shapes=[pltpu.VMEM((2, bm, bn), F32), pltpu.VMEM((2, nj, 2 * SUBLANE, bn), F32)],
        compiler_params=_cparams(("arbitrary", "arbitrary"), est),
        name="ffn_up",
    )(xn, w_up, w_up, conv_w, conv_w)


def _gdn_pre_kernel(x_ref, w_ref, o_ref, *, nq, nk):
    j = pl.program_id(1)
    x = x_ref[...]
    w = w_ref[...]
    kw = 4
    row = lax.broadcasted_iota(jnp.int32, x.shape, 0)
    acc = w[kw - 1:kw] * x
    for s in range(1, kw):
        acc = acc + w[kw - 1 - s:kw - s] * jnp.where(row >= s, pltpu.roll(x, s, 0), 0.0)
    y = jax.nn.silu(acc)
    parts = []
    for c in range(x.shape[1] // GDN_DK):
        yh = y[:, c * GDN_DK:(c + 1) * GDN_DK]
        parts.append(yh * lax.rsqrt(jnp.sum(yh * yh, axis=-1, keepdims=True) + NORM_EPS))
    yn = parts[0] if len(parts) == 1 else jnp.concatenate(parts, axis=1)
    is_q = j < nq
    is_k = jnp.logical_and(j >= nq, j < nq + nk)
    o_ref[...] = jnp.where(is_q, yn * (GDN_DK ** -0.5), jnp.where(is_k, yn, y))


def _gdn_pre(src, conv_w, batch, seq, heads):
    width = conv_w.shape[1]
    cb = GDN_DK
    nq = heads * GDN_DK // cb
    est = 4 * seq * cb * 4 + 8 * seq * cb * 4
    return pl.pallas_call(
        functools.partial(_gdn_pre_kernel, nq=nq, nk=nq),
        grid=(batch, width // cb),
        in_specs=[pl.BlockSpec((seq, cb), lambda b, j: (b, j)),
                  pl.BlockSpec((SUBLANE, cb), lambda b, j: (0, j))],
        out_specs=pl.BlockSpec((seq, cb), lambda b, j: (b, j)),
        out_shape=jax.ShapeDtypeStruct((batch * seq, width), F32),
        compiler_params=_cparams(("parallel", "parallel"), est),
        name="gdn_pre",
    )(src, conv_w)


GDN_ROWS = 256
GDN_HEADS_PER_STEP = 2


def _dot_hi(a, b):
    return jnp.dot(a, b, precision=HI, preferred_element_type=F32)


def _bdot(a, b):
    return jnp.dot(a.astype(BF16), b.astype(BF16), preferred_element_type=F32)


def _bdot_nt(a, b):
    return lax.dot_general(a.astype(BF16), b.astype(BF16), (((1,), (1,)), ((), ())), preferred_element_type=F32)


def _bdot_tn(a, b):
    return lax.dot_general(a.astype(BF16), b.astype(BF16), (((0,), (0,)), ((), ())), preferred_element_type=F32)


def _gdn_head(alog, dtb, q, k, v, a_raw, b_raw, gate, ng, s, masks, rows):
    same, tril, tril_t, strict, eye, eye_f = masks
    a_raw = a_raw + dtb
    softplus = jnp.maximum(a_raw, 0.0) + jnp.log(1.0 + jnp.exp(-jnp.abs(a_raw)))
    g = -jnp.exp(jnp.zeros_like(a_raw) + alog) * softplus
    beta = jax.nn.sigmoid(b_raw)
    g_cols = jnp.broadcast_to(g, (rows, rows))
    g_rows = jnp.broadcast_to(jnp.sum(jnp.where(eye, g_cols, 0.0), axis=0, keepdims=True), (rows, rows))
    gc_col = jnp.sum(jnp.where(tril, g_rows, 0.0), axis=1, keepdims=True)
    gl_col = jnp.sum(jnp.where(same, g_rows, 0.0), axis=1, keepdims=True)
    gc_row = jnp.sum(jnp.where(tril_t, g_cols, 0.0), axis=0, keepdims=True)
    gc = jnp.broadcast_to(gc_col, (rows, LANE))
    gl = jnp.broadcast_to(gl_col, (rows, LANE))
    decay = jnp.exp(jnp.where(tril, gc_col - gc_row, MASK_VALUE))
    kb = k * beta
    vb = v * beta
    lmat = jnp.where(strict, _bdot_nt(kb, k) * decay, 0.0)
    tinv = eye_f - lmat
    pw = lmat
    for _ in range(int(math.log2(CHUNK)) - 1):
        pw = _bdot(pw, pw)
        tinv = tinv + _bdot(tinv, pw)
    eg = jnp.exp(gc)
    sol = _bdot(tinv, jnp.concatenate([vb, kb * eg], axis=1))
    u_c, w_c = sol[:, :GDN_DV], sol[:, GDN_DV:]
    intra = _bdot_nt(q, k) * decay
    q_dec = q * eg
    k_dec = k * jnp.exp(gl - gc)
    egl = jnp.exp(gl)
    iuw = _bdot(intra, sol)
    o_loc = iuw[:, :GDN_DV]
    q_eff = q_dec - iuw[:, GDN_DV:]
    outs = []
    for c in range(rows // CHUNK):
        lo, hi = c * CHUNK, (c + 1) * CHUNK
        kuw = _bdot_tn(k_dec[lo:hi], sol[lo:hi])
        outs.append(_bdot(q_eff[lo:hi], s) + o_loc[lo:hi])
        s = s * egl[lo:lo + 1, :] + kuw[:, :GDN_DV] - _bdot(kuw[:, GDN_DV:], s)
    o = jnp.concatenate(outs, axis=0)
    o = o * lax.rsqrt(jnp.mean(o * o, axis=-1, keepdims=True) + NORM_EPS) * ng
    return o * jax.nn.silu(gate), s


def _gdn_kernel(alog_ref, dtb_ref, q_ref, k_ref, v_ref, ab_ref, gate_ref, ng_ref, o_ref, state, *,
                rows, hps, heads):
    hb = pl.program_id(1)
    r = pl.program_id(2)
    ab = ab_ref[...]
    ab_lane = lax.broadcasted_iota(jnp.int32, ab.shape, 1)

    @pl.when(r == 0)
    def _():
        state[...] = jnp.zeros_like(state)

    ri = lax.broadcasted_iota(jnp.int32, (rows, rows), 0)
    ci = lax.broadcasted_iota(jnp.int32, (rows, rows), 1)
    same = (ri // CHUNK) == (ci // CHUNK)
    tril = jnp.logical_and(same, ci <= ri)
    tril_t = jnp.logical_and(same, ri <= ci)
    strict = jnp.logical_and(same, ci < ri)
    eye = ri == ci
    masks = (same, tril, tril_t, strict, eye, jnp.where(eye, 1.0, 0.0))
    ng = ng_ref[...]
    for i in range(hps):
        cs = slice(i * GDN_DK, (i + 1) * GDN_DK)
        h = hb * hps + i
        a_raw = jnp.sum(jnp.where(ab_lane == h, ab, 0.0), axis=1, keepdims=True)
        b_raw = jnp.sum(jnp.where(ab_lane == heads + h, ab, 0.0), axis=1, keepdims=True)
        o, s = _gdn_head(alog_ref[h], dtb_ref[h], q_ref[:, cs], k_ref[:, cs], v_ref[:, cs], a_raw, b_raw,
                         gate_ref[:, cs], ng, state[i], masks, rows)
        state[i] = s
        o_ref[:, cs] = o.astype(o_ref.dtype)


def _gdn_core(qkv, ab, gate_src, gate_col0, a_log, dt_bias, norm_g, batch, seq, heads):
    rows = _pick(seq, GDN_ROWS, CHUNK)
    nr = seq // rows
    hps = GDN_HEADS_PER_STEP if heads % GDN_HEADS_PER_STEP == 0 else 1
    nhb = heads // hps
    wblk = hps * GDN_DK
    gcb = gate_col0 // wblk
    est = hps * (12 * rows * LANE * 4 + 16 * rows * rows * 4 + 8 * rows * LANE * 4)
    tok = lambda b, h, r: b * nr + r
    return pl.pallas_call(
        functools.partial(_gdn_kernel, rows=rows, hps=hps, heads=heads),
        grid=(batch, nhb, nr),
        in_specs=[pl.BlockSpec(memory_space=pltpu.SMEM),
                  pl.BlockSpec(memory_space=pltpu.SMEM),
                  pl.BlockSpec((rows, wblk), lambda b, h, r: (tok(b, h, r), h)),
                  pl.BlockSpec((rows, wblk), lambda b, h, r: (tok(b, h, r), nhb + h)),
                  pl.BlockSpec((rows, wblk), lambda b, h, r: (tok(b, h, r), 2 * nhb + h)),
                  pl.BlockSpec((rows, LANE), lambda b, h, r: (tok(b, h, r), 0)),
                  pl.BlockSpec((rows, wblk), lambda b, h, r: (tok(b, h, r), gcb + h)),
                  pl.BlockSpec((1, GDN_DV), lambda b, h, r: (0, 0))],
        out_specs=pl.BlockSpec((rows, wblk), lambda b, h, r: (tok(b, h, r), h)),
        out_shape=jax.ShapeDtypeStruct((batch * seq, heads * GDN_DV), BF16),
        scratch_shapes=[pltpu.VMEM((hps, GDN_DK, GDN_DV), F32)],
        compiler_params=_cparams(("parallel", "parallel", "arbitrary"), est),
        name="gdn_core",
    )(a_log.astype(F32), dt_bias.astype(F32), qkv, qkv, qkv, ab, gate_src,
      norm_g.reshape(1, GDN_DV).astype(F32))


SB_TQ = 512
SB_TK = 256
SB_HEADS_PER_STEP = 2


def _sb_kernel(q_ref, kt_ref, v_ref, o_ref, acc_s, c_s, *, tq, tk, hps):
    i = pl.program_id(2)
    r = tq // tk
    ur = lax.broadcasted_iota(jnp.int32, (2 * tk, tk), 0) % tk
    uc = lax.broadcasted_iota(jnp.int32, (2 * tk, tk), 1)
    upper2 = jnp.where(ur > uc, 1.0, 0.0).astype(BF16)
    ri = lax.broadcasted_iota(jnp.int32, (tq, tk), 0)
    ci = lax.broadcasted_iota(jnp.int32, (tq, tk), 1)
    acc_s[...] = jnp.zeros_like(acc_s)
    c_s[...] = jnp.zeros_like(c_s)

    def cols(hh):
        return slice(hh * SB_D, (hh + 1) * SB_D)

    def logits(hh, j):
        z = jnp.dot(q_ref[:, cols(hh)], kt_ref[j, cols(hh), :], preferred_element_type=F32)
        return z, -(jnp.maximum(z, 0.0) + jnp.log(1.0 + jnp.exp2(-jnp.abs(z))) * LOG2E)

    def weights(z, lm, c):
        hi = lm.astype(BF16)
        lo = (lm - hi.astype(F32)).astype(BF16)
        rest = jnp.dot(jnp.concatenate([hi, lo], axis=1), upper2, preferred_element_type=F32)
        return jnp.exp2(z + lm + rest + c)

    def values(hh, j):
        return v_ref[pl.ds(pl.multiple_of(j * tk, tk), tk), cols(hh)]

    for d in range(r - 1, -1, -1):
        j = i * r + d
        before = (ci + d * tk) < ri
        for hh in range(hps):
            z, lm = logits(hh, j)
            lm = jnp.where(before, lm, 0.0)
            w = jnp.where(before, weights(z, lm, c_s[hh]), 0.0)
            acc_s[:, cols(hh)] += jnp.dot(w.astype(BF16), values(hh, j), preferred_element_type=F32)
            c_s[hh] += jnp.sum(lm, axis=1, keepdims=True)

    def single(j):
        for hh in range(hps):
            z, lm = logits(hh, j)
            w = weights(z, lm, c_s[hh])
            acc_s[:, cols(hh)] += jnp.dot(w.astype(BF16), values(hh, j), preferred_element_type=F32)
            c_s[hh] += jnp.sum(lm, axis=1, keepdims=True)

    def pair(p, carry):
        ja = i * r - 1 - 2 * p
        jb = ja - 1
        for hh in range(hps):
            za, lma = logits(hh, ja)
            zb, lmb = logits(hh, jb)
            c = c_s[hh]
            ta = jnp.sum(lma, axis=1, keepdims=True)
            wa = weights(za, lma, c)
            wb = weights(zb, lmb, c + ta)
            acc_s[:, cols(hh)] += (jnp.dot(wa.astype(BF16), values(hh, ja), preferred_element_type=F32)
                                   + jnp.dot(wb.astype(BF16), values(hh, jb), preferred_element_type=F32))
            c_s[hh] = c + ta + jnp.sum(lmb, axis=1, keepdims=True)
        return carry

    n_full = i * r
    lax.fori_loop(0, n_full // 2, pair, 0)
    if r % 2 == 1:
        @pl.when(n_full % 2 == 1)
        def _():
            single(0)
    o_ref[...] = acc_s[...].astype(o_ref.dtype)


def _sb_tiles(seq):
    tq = _pick(seq, SB_TQ, LANE)
    return tq, _pick(tq, SB_TK, LANE)


def _sb_attention(qv, kt, batch, seq, heads):
    tq, tk = _sb_tiles(seq)
    nq, nk = seq // tq, seq // tk
    hps = SB_HEADS_PER_STEP if heads % SB_HEADS_PER_STEP == 0 else 1
    nhb = heads // hps
    wblk = hps * SB_D
    est = hps * (4 * tq * SB_D * 2 + 4 * seq * SB_D * 2 + 2 * tq * SB_D * 4 + 24 * tq * tk * 4)
    return pl.pallas_call(
        functools.partial(_sb_kernel, tq=tq, tk=tk, hps=hps),
        grid=(batch, nhb, nq),
        in_specs=[pl.BlockSpec((tq, wblk), lambda b, h, i: (b * nq + i, h)),
                  pl.BlockSpec((nk, wblk, tk), lambda b, h, i: (b, h, 0)),
                  pl.BlockSpec((seq, wblk), lambda b, h, i: (b, nhb + h))],
        out_specs=pl.BlockSpec((tq, wblk), lambda b, h, i: (b * nq + i, h)),
        out_shape=jax.ShapeDtypeStruct((batch * seq, heads * SB_D), BF16),
        scratch_shapes=[pltpu.VMEM((tq, wblk), F32), pltpu.VMEM((hps, tq, 1), F32)],
        compiler_params=_cparams(("parallel", "parallel", "arbitrary"), est),
        name="sb_attention",
    )(qv, kt, qv)


def _pad_cols(w, n):
    return jnp.pad(w, ((0, 0), (0, n - w.shape[1])))


def _ffn(h, ln_g, w_up, layer, conv_w, w_down_bf16, seq):
    xn = _rmsnorm(h, ln_g, BF16)
    conv8 = jnp.pad(conv_w.astype(F32), ((0, SUBLANE - conv_w.shape[0]), (0, 0)))
    act = _ffn_up(xn, w_up, layer, conv8, seq)
    return _matmul([(act, w_down_bf16)], F32, residual=h, b_layer=layer)


def _even_layer(h, ln_g, w_in, w_out, q_norm, kv_norm, w_uq, w_ukv,
                a_re, a_im, log_step, b_re, b_im, c_re, c_im, d_skip, w_glu, batch, seq):
    d = h.shape[1]
    q_rank, kv_rank = q_norm.shape[0], kv_norm.shape[0]
    heads = w_uq.shape[1] // (MLA_NOPE + MLA_ROPE)
    s5_width = d_skip.shape[0]
    o_kr = q_rank + kv_rank
    o_s5 = o_kr + MLA_ROPE
    latent_w = -(-(o_kr + LANE) // s5_width) * s5_width if s5_width >= LANE else o_kr + LANE
    w_lat = _pad_cols(w_in[:, :o_s5], latent_w)
    w_cat = jnp.concatenate([w_lat, w_in[:, o_s5:]], axis=1).astype(BF16)
    hn = _rmsnorm(h, ln_g, BF16)
    proj = _matmul([(hn, w_cat)], F32)
    cq = _rmsnorm(proj, q_norm, BF16, col0=0, width=q_rank)
    ckv = _rmsnorm(proj, kv_norm, BF16, col0=q_rank, width=kv_rank)
    q_scale = (MLA_NOPE + MLA_ROPE) ** -0.5 * LOG2E
    w_uq_p = jnp.pad(w_uq.reshape(q_rank, heads, MLA_NOPE + MLA_ROPE) * q_scale,
                     ((0, 0), (0, 0), (0, MLA_QK - MLA_NOPE - MLA_ROPE))).reshape(q_rank, heads * MLA_QK)
    q = _matmul([(cq, w_uq_p.astype(BF16))], F32)
    tq = _mla_tile(seq)
    w_kv = w_ukv.reshape(kv_rank, heads, MLA_NOPE + MLA_V)
    w_kn = w_kv[:, :, :MLA_NOPE].reshape(kv_rank, heads * MLA_NOPE).astype(BF16)
    w_v = w_kv[:, :, MLA_NOPE:].reshape(kv_rank, heads * MLA_V).astype(BF16)
    knt = _matmul_keys_t(ckv, w_kn, tq)
    v = _matmul([(ckv, w_v)], BF16)
    cosf, sinf = _rope_tables(seq)
    krt = _rope_k(proj, o_kr, cosf, sinf, seq, tq)
    out_a = _mla_attention(q, knt, krt, v, cosf, sinf, batch, seq, heads)
    params = _s5t_params(a_re, a_im, log_step, b_re, b_im, c_re, c_im, d_skip)
    g = _s5_toeplitz(proj, latent_w, params, batch, seq)
    out_b = _s5_glu(g, w_glu.astype(BF16))
    na = heads * MLA_V
    return _matmul([(out_a, w_out[:na].astype(BF16)), (out_b, w_out[na:].astype(BF16))], F32, residual=h)


def _odd_layer(h, ln_g, w_in, w_out, conv_w, a_log, dt_bias, norm_g, batch, seq):
    heads = a_log.shape[0]
    qkv_w = heads * (2 * GDN_DK + GDN_DV)
    o2 = qkv_w + 2 * heads
    o3 = o2 + heads * GDN_DV
    hn = _rmsnorm(h, ln_g, BF16)
    w_main = jnp.concatenate([w_in[:, :qkv_w], w_in[:, o2:o3]], axis=1).astype(BF16)
    w_ab = _pad_cols(w_in[:, qkv_w:o2], LANE).astype(BF16)
    sb_heads = (w_in.shape[1] - o3) // (3 * SB_D)
    nsq = sb_heads * SB_D
    w_sb_qv = jnp.concatenate([w_in[:, o3:o3 + nsq] * (SB_D ** -0.5 * LOG2E), w_in[:, o3 + 2 * nsq:]],
                              axis=1).astype(BF16)
    w_sb_k = w_in[:, o3 + nsq:o3 + 2 * nsq].astype(BF16)
    main = _matmul([(hn, w_main)], F32)
    ab = _matmul([(hn, w_ab)], F32)
    sb_qv = _matmul([(hn, w_sb_qv)], BF16)
    sb_kt = _matmul_keys_t(hn, w_sb_k, _sb_tiles(seq)[1])
    conv8 = jnp.pad(conv_w.astype(F32), ((0, SUBLANE - conv_w.shape[0]), (0, 0)))
    qkv = _gdn_pre(main, conv8, batch, seq, heads)
    out_c = _gdn_core(qkv, ab, main, qkv_w, a_log, dt_bias, norm_g, batch, seq, heads)
    out_d = _sb_attention(sb_qv, sb_kt, batch, seq, sb_heads)
    nc = heads * GDN_DV
    return _matmul([(out_c, w_out[:nc].astype(BF16)), (out_d, w_out[nc:].astype(BF16))], F32, residual=h)


def kernel(x, ln_mix, ln_ffn, ln_final, ffn_w_up, ffn_conv, ffn_w_down, ev_w_in, ev_w_out, mla_q_norm,
           mla_kv_norm, mla_w_uq, mla_w_ukv, s5_a_re, s5_a_im, s5_log_step, s5_b_re, s5_b_im, s5_c_re,
           s5_c_im, s5_d, s5_w_glu, od_w_in, od_w_out, gdn_conv, gdn_a_log, gdn_dt_bias, gdn_norm):
    batch, seq, d = x.shape
    h = x.reshape(batch * seq, d).astype(F32)
    ffn_w_up = ffn_w_up.astype(F32)
    w_down_bf16 = ffn_w_down.astype(BF16)
    for layer in range(ln_mix.shape[0]):
        i = layer // 2
        if layer % 2 == 0:
            h = _even_layer(h, ln_mix[layer], ev_w_in[i], ev_w_out[i], mla_q_norm[i], mla_kv_norm[i],
                            mla_w_uq[i], mla_w_ukv[i], s5_a_re[i], s5_a_im[i], s5_log_step[i], s5_b_re[i],
                            s5_b_im[i], s5_c_re[i], s5_c_im[i], s5_d[i], s5_w_glu[i], batch, seq)
        else:
            h = _odd_layer(h, ln_mix[layer], od_w_in[i], od_w_out[i], gdn_conv[i], gdn_a_log[i],
                           gdn_dt_bias[i], gdn_norm[i], batch, seq)
        h = _ffn(h, ln_ffn[layer], ffn_w_up, layer, ffn_conv[layer], w_down_bf16, seq)
    return _rmsnorm(h, ln_final, x.dtype).reshape(batch, seq, d)
```

```python
import functools
import math

import jax
import jax.numpy as jnp
from jax import lax
from jax.experimental import pallas as pl
from jax.experimental.pallas import tpu as pltpu

F32 = jnp.float32
BF16 = jnp.bfloat16
HI = lax.Precision.HIGHEST

CHUNK = 64
NORM_EPS = 1e-6
MLA_NOPE = 128
MLA_ROPE = 64
MLA_V = 128
MLA_QK = 256
ROPE_BASE = 10000.0
S5_GROUP = 16
S5_STATE = 64
GDN_DK = 128
GDN_DV = 128
SB_D = 128

LANE = 128
SUBLANE = 8
VMEM_LIMIT_MAX = 58 * 1024 * 1024
VMEM_LIMIT_MIN = 32 * 1024 * 1024
MASK_VALUE = -1e30
LOG2E = math.log2(math.e)


def _cparams(sem, vmem_est):
    limit = int(min(max(vmem_est * 5 // 4 + (4 << 20), VMEM_LIMIT_MIN), VMEM_LIMIT_MAX))
    return pltpu.CompilerParams(dimension_semantics=sem, vmem_limit_bytes=limit)


def _pick(n, pref, mult):
    if n <= pref:
        return n
    t = (pref // mult) * mult
    while t >= mult:
        if n % t == 0:
            return t
        t -= mult
    return n


def _rmsnorm_kernel(x_ref, g_ref, o_ref):
    x = x_ref[...].astype(F32)
    ms = jnp.mean(x * x, axis=-1, keepdims=True)
    o_ref[...] = (x * lax.rsqrt(ms + NORM_EPS) * g_ref[...]).astype(o_ref.dtype)


def _rmsnorm(x, g, out_dtype, *, col0=0, width=None):
    m = x.shape[0]
    width = x.shape[1] if width is None else width
    tm = _pick(m, 256, SUBLANE)
    cb = col0 // width
    est = 2 * tm * width * (x.dtype.itemsize + jnp.dtype(out_dtype).itemsize) + 4 * tm * width * 4
    return pl.pallas_call(
        _rmsnorm_kernel,
        grid=(m // tm,),
        in_specs=[pl.BlockSpec((tm, width), lambda i: (i, cb)),
                  pl.BlockSpec((1, width), lambda i: (0, 0))],
        out_specs=pl.BlockSpec((tm, width), lambda i: (i, 0)),
        out_shape=jax.ShapeDtypeStruct((m, width), out_dtype),
        compiler_params=_cparams(("parallel",), est),
        name="rmsnorm",
    )(x, g.reshape(1, width).astype(F32))


def _mm_kernel(*refs, n_pairs, has_res, b_nk):
    o_ref = refs[-1]
    acc = None
    for p in range(n_pairs):
        a = refs[2 * p][...].astype(BF16)
        b = refs[2 * p + 1][...].astype(BF16)
        if b_nk:
            d = lax.dot_general(a, b, (((1,), (1,)), ((), ())), preferred_element_type=F32)
        else:
            d = jnp.dot(a, b, preferred_element_type=F32)
        acc = d if acc is None else acc + d
    if has_res:
        acc = acc + refs[2 * n_pairs][...]
    o_ref[...] = acc.astype(o_ref.dtype)


def _mm_tiles(m, ks, n, a_bytes, out_bytes, has_res, budget=52 << 20, b_bytes=2):
    for bm_p, bn_p in ((1024, 1024), (1024, 512), (512, 512), (512, 256), (256, 256), (256, 128), (128, 128)):
        bm = _pick(m, bm_p, SUBLANE)
        bn = _pick(n, bn_p, LANE)
        est = bm * bn * 4
        for k, ab in zip(ks, a_bytes):
            est += 2 * (bm * k * ab + k * bn * b_bytes) + (k * bn * 2 if b_bytes > 2 else 0)
        est += 2 * bm * bn * (out_bytes + (4 if has_res else 0))
        if est <= budget:
            return bm, bn, est
    return bm, bn, est


def _matmul(pairs, out_dtype, residual=None, *, a_col0=None, b_layer=None, b_row0=None, b_nk=False):
    m = pairs[0][0].shape[0]
    n = pairs[0][1].shape[0] if b_nk else pairs[0][1].shape[-1]
    if b_row0 is not None:
        ks = [a.shape[1] for a, _ in pairs]
    else:
        ks = [b.shape[1] if b_nk else b.shape[-2] for _, b in pairs]
        b_row0 = [0] * len(pairs)
    a_col0 = [0] * len(pairs) if a_col0 is None else a_col0
    a_bytes = [a.dtype.itemsize for a, _ in pairs]
    bm, bn, est = _mm_tiles(m, ks, n, a_bytes, jnp.dtype(out_dtype).itemsize, residual is not None,
                            b_bytes=pairs[0][1].dtype.itemsize)
    in_specs, args = [], []
    for (a, b), k, c0, r0 in zip(pairs, ks, a_col0, b_row0):
        cb = c0 // k
        rb = r0 // k
        in_specs.append(pl.BlockSpec((bm, k), lambda i, j, cb=cb: (i, cb)))
        if b_nk:
            in_specs.append(pl.BlockSpec((bn, k), lambda i, j: (j, 0)))
        elif b_layer is None:
            in_specs.append(pl.BlockSpec((k, bn), lambda i, j: (0, j)))
        else:
            in_specs.append(pl.BlockSpec((None, k, bn), lambda i, j, rb=rb: (b_layer, rb, j)))
        args += [a, b]
    if residual is not None:
        in_specs.append(pl.BlockSpec((bm, bn), lambda i, j: (i, j)))
        args.append(residual)
    return pl.pallas_call(
        functools.partial(_mm_kernel, n_pairs=len(pairs), has_res=residual is not None, b_nk=b_nk),
        grid=(m // bm, n // bn),
        in_specs=in_specs,
        out_specs=pl.BlockSpec((bm, bn), lambda i, j: (i, j)),
        out_shape=jax.ShapeDtypeStruct((m, n), out_dtype),
        compiler_params=_cparams(("parallel", "parallel"), est),
        name="matmul",
    )(*args)


def _mm_t_kernel(a_ref, b_ref, o_ref, acc_ref, *, tk, b_nk):
    a = a_ref[...].astype(BF16)
    if b_nk:
        acc_ref[...] = lax.dot_general(a, b_ref[...], (((1,), (1,)), ((), ())), preferred_element_type=F32)
    else:
        acc_ref[...] = jnp.dot(a, b_ref[...], preferred_element_type=F32)
    res_t = jnp.transpose(acc_ref[...])
    for c in range(o_ref.shape[0]):
        o_ref[c] = res_t[:, c * tk:(c + 1) * tk].astype(o_ref.dtype)


def _matmul_keys_t(a, b, tk, *, b_nk=False):
    m, k = a.shape
    n = b.shape[0] if b_nk else b.shape[1]
    bm = _pick(m, 1024, tk)
    bn = _pick(n, 512, LANE)
    est = 2 * (bm * k * a.dtype.itemsize + k * bn * 2 + bm * bn * 2) + 3 * bm * bn * 4
    return pl.pallas_call(
        functools.partial(_mm_t_kernel, tk=tk, b_nk=b_nk),
        grid=(m // bm, n // bn),
        in_specs=[pl.BlockSpec((bm, k), lambda i, j: (i, 0)),
                  pl.BlockSpec((bn, k), lambda i, j: (j, 0)) if b_nk else
                  pl.BlockSpec((k, bn), lambda i, j: (0, j))],
        out_specs=pl.BlockSpec((bm // tk, bn, tk), lambda i, j: (i, j, 0)),
        out_shape=jax.ShapeDtypeStruct((m // tk, n, tk), BF16),
        scratch_shapes=[pltpu.VMEM((bm, bn), F32)],
        compiler_params=_cparams(("parallel", "parallel"), est),
        name="matmul_keys_t",
    )(a, b)


def _rope_tables(seq):
    half = MLA_ROPE // 2
    inv_freq = ROPE_BASE ** (-jnp.arange(half, dtype=F32) / half)
    ang = jnp.arange(seq, dtype=F32)[:, None] * inv_freq[None, :]
    cos, sin = jnp.cos(ang), jnp.sin(ang)
    zeros = jnp.zeros((seq, LANE - MLA_ROPE), F32)
    return (jnp.concatenate([cos, cos, zeros], axis=1),
            jnp.concatenate([-sin, sin, zeros], axis=1))


def _rope_lanes(x, cosf, sinf):
    half = MLA_ROPE // 2
    lane = lax.broadcasted_iota(jnp.int32, x.shape, 1)
    swapped = jnp.where(lane < half, pltpu.roll(x, LANE - half, 1), pltpu.roll(x, half, 1))
    return x * cosf + swapped * sinf


def _rope_kernel(x_ref, cos_ref, sin_ref, o_ref):
    o_ref[0] = jnp.transpose(_rope_lanes(x_ref[...], cos_ref[...], sin_ref[...])).astype(o_ref.dtype)


def _rope_k(x, col0, cosf, sinf, seq, tq):
    m = x.shape[0]
    per_seq = seq // tq
    cb = col0 // LANE
    return pl.pallas_call(
        _rope_kernel,
        grid=(m // tq,),
        in_specs=[pl.BlockSpec((tq, LANE), lambda i: (i, cb)),
                  pl.BlockSpec((tq, LANE), lambda i: (i % per_seq, 0)),
                  pl.BlockSpec((tq, LANE), lambda i: (i % per_seq, 0))],
        out_specs=pl.BlockSpec((1, LANE, tq), lambda i: (i, 0, 0)),
        out_shape=jax.ShapeDtypeStruct((m // tq, LANE, tq), BF16),
        compiler_params=_cparams(("parallel",), 16 * tq * LANE * 4),
        name="rope_k",
    )(x, cosf, sinf)


def _mla_kernel(q_ref, knt_ref, krt_ref, v_ref, cos_ref, sin_ref, o_ref, qs, m_s, l_s, acc_s, *, tq):
    i = pl.program_id(2)
    q = q_ref[...]
    qs[:, :MLA_NOPE] = q[:, :MLA_NOPE].astype(BF16)
    qs[:, MLA_NOPE:] = _rope_lanes(q[:, MLA_NOPE:], cos_ref[...], sin_ref[...]).astype(BF16)

    def tile(j):
        kt = jnp.concatenate([knt_ref[j], krt_ref[j]], axis=0)
        s = jnp.dot(qs[...], kt, preferred_element_type=F32)
        return s, v_ref[pl.ds(pl.multiple_of(j * tq, tq), tq), :]

    s, v = tile(i)
    rc = lax.broadcasted_iota(jnp.int32, (tq, tq), 0) // CHUNK
    cc = lax.broadcasted_iota(jnp.int32, (tq, tq), 1) // CHUNK
    s = jnp.where(cc <= rc, s, MASK_VALUE)
    m = jnp.max(s, axis=1, keepdims=True)
    p = jnp.exp2(s - m)
    m_s[...] = m
    l_s[...] = jnp.sum(p, axis=1, keepdims=True)
    acc_s[...] = jnp.dot(p.astype(BF16), v, preferred_element_type=F32)

    def body(j, carry):
        s, v = tile(j)
        m_prev = m_s[...]
        m_new = jnp.maximum(m_prev, jnp.max(s, axis=1, keepdims=True))
        alpha = jnp.exp2(m_prev - m_new)
        p = jnp.exp2(s - m_new)
        l_s[...] = alpha * l_s[...] + jnp.sum(p, axis=1, keepdims=True)
        acc_s[...] = alpha * acc_s[...] + jnp.dot(p.astype(BF16), v, preferred_element_type=F32)
        m_s[...] = m_new
        return carry

    lax.fori_loop(0, i, body, 0)
    o_ref[...] = (acc_s[...] / l_s[...]).astype(o_ref.dtype)


MLA_TILE = 1024


def _mla_tile(seq):
    return _pick(seq, MLA_TILE, CHUNK)


def _mla_attention(q, knt, krt, v, cosf, sinf, batch, seq, heads):
    tq = _mla_tile(seq)
    nq = seq // tq
    est = (2 * tq * MLA_QK * 4 + 2 * seq * (MLA_QK + LANE) * 2 + 4 * tq * LANE * 4 + 2 * tq * LANE * 2
           + tq * MLA_QK * 2 + 3 * tq * LANE * 4 + 6 * tq * tq * 4)
    return pl.pallas_call(
        functools.partial(_mla_kernel, tq=tq),
        grid=(batch, heads, nq),
        in_specs=[pl.BlockSpec((tq, MLA_QK), lambda b, h, i: (b * nq + i, h)),
                  pl.BlockSpec((nq, MLA_NOPE, tq), lambda b, h, i: (b, h, 0)),
                  pl.BlockSpec((nq, LANE, tq), lambda b, h, i: (b, 0, 0)),
                  pl.BlockSpec((seq, MLA_V), lambda b, h, i: (b, h)),
                  pl.BlockSpec((tq, LANE), lambda b, h, i: (i, 0)),
                  pl.BlockSpec((tq, LANE), lambda b, h, i: (i, 0))],
        out_specs=pl.BlockSpec((tq, MLA_V), lambda b, h, i: (b * nq + i, h)),
        out_shape=jax.ShapeDtypeStruct((batch * seq, heads * MLA_V), BF16),
        scratch_shapes=[pltpu.VMEM((tq, MLA_QK), BF16), pltpu.VMEM((tq, 1), F32),
                        pltpu.VMEM((tq, 1), F32), pltpu.VMEM((tq, MLA_V), F32)],
        compiler_params=_cparams(("parallel", "parallel", "arbitrary"), est),
        name="mla_attention",
    )(q, knt, krt, v, cosf, sinf)


S5_GROUPS_PER_BLOCK = 8


def _s5_kernel(u_ref, bre_ref, bim_ref, cre_ref, cim_ref, pw_ref, d_ref, o_ref, sre, sim, car, *, tt):
    t = pl.program_id(2)

    @pl.when(t == 0)
    def _():
        car[...] = jnp.zeros_like(car)

    u = u_ref[...]
    ub = u.astype(BF16)
    sre[...] = jnp.dot(ub, bre_ref[0], preferred_element_type=F32)
    sim[...] = jnp.dot(ub, bim_ref[0], preferred_element_type=F32)
    ns = sre.shape[1]
    row = lax.broadcasted_iota(jnp.int32, (SUBLANE, ns), 0)
    steps = [(1, pw_ref[0, 0], pw_ref[0, 1]), (2, pw_ref[0, 2], pw_ref[0, 3]), (4, pw_ref[0, 4], pw_ref[0, 5])]
    pwr, pwi = pw_ref[0, 6], pw_ref[0, 7]

    def body(r, carry):
        cr, ci = carry
        off = pl.multiple_of(r * SUBLANE, SUBLANE)
        xr = sre[pl.ds(off, SUBLANE), :]
        xi = sim[pl.ds(off, SUBLANE), :]
        for k, ar, ai in steps:
            sr = jnp.where(row >= k, pltpu.roll(xr, k, 0), 0.0)
            si = jnp.where(row >= k, pltpu.roll(xi, k, 0), 0.0)
            xr, xi = xr + ar * sr - ai * si, xi + ar * si + ai * sr
        xr, xi = xr + pwr * cr - pwi * ci, xi + pwr * ci + pwi * cr
        sre[pl.ds(off, SUBLANE), :] = xr
        sim[pl.ds(off, SUBLANE), :] = xi
        return xr[SUBLANE - 1:, :], xi[SUBLANE - 1:, :]

    cr, ci = lax.fori_loop(0, tt // SUBLANE, body, (car[0:1, :], car[1:2, :]))
    car[0:1, :] = cr
    car[1:2, :] = ci
    y = (jnp.dot(sre[...].astype(BF16), cre_ref[0], preferred_element_type=F32)
         - jnp.dot(sim[...].astype(BF16), cim_ref[0], preferred_element_type=F32))
    y = y + d_ref[...] * u
    o_ref[...] = jax.nn.gelu(y, approximate=True)


def _s5_params(a_re, a_im, log_step, b_re, b_im, c_re, c_im):
    g, p = a_re.shape
    gb = min(S5_GROUPS_PER_BLOCK, g)
    nb = g // gb
    lam = lax.complex(a_re.astype(F32), a_im.astype(F32))
    step = jnp.exp(log_step.astype(F32))[:, None]
    lam_dt = lam * step
    lam_bar = jnp.exp(lam_dt)
    b_bar = ((lam_bar - 1.0) / lam)[..., None] * lax.complex(b_re.astype(F32), b_im.astype(F32))
    c_c = lax.complex(c_re.astype(F32), c_im.astype(F32))
    eye = jnp.eye(gb, dtype=F32)

    def bd_in(x):
        x = x.reshape(nb, gb, p, S5_GROUP)
        return jnp.einsum('ngpi,gh->ngihp', x, eye).reshape(nb, gb * S5_GROUP, gb * p)

    def bd_out(x):
        x = x.reshape(nb, gb, S5_GROUP, p)
        return jnp.einsum('ngip,gh->ngphi', x, eye).reshape(nb, gb * p, gb * S5_GROUP)

    def rows(z):
        z = z.reshape(nb, 1, gb * p)
        return (jnp.broadcast_to(jnp.real(z), (nb, SUBLANE, gb * p)),
                jnp.broadcast_to(jnp.imag(z), (nb, SUBLANE, gb * p)))

    planes = []
    for k in (1, 2, 4):
        planes += list(rows(jnp.exp(lam_dt * float(k))))
    pw = jnp.exp(lam_dt[None] * jnp.arange(1, SUBLANE + 1, dtype=F32)[:, None, None])
    pw = jnp.moveaxis(pw.reshape(SUBLANE, nb, gb * p), 0, 1)
    planes += [jnp.real(pw), jnp.imag(pw)]
    return (bd_in(jnp.real(b_bar)).astype(BF16), bd_in(jnp.imag(b_bar)).astype(BF16),
            bd_out(jnp.real(c_c)).astype(BF16), bd_out(jnp.imag(c_c)).astype(BF16),
            jnp.stack(planes, axis=1), gb)


def _s5_scan(u_src, col0, params, d_skip, batch, seq):
    bre, bim, cre, cim, pw, gb = params
    nb = bre.shape[0]
    gc = gb * S5_GROUP
    ns = gb * S5_STATE
    tt = _pick(seq, 512, SUBLANE)
    nt = seq // tt
    cb0 = col0 // gc
    est = 4 * tt * gc * 4 + 8 * gc * ns * 4 + 4 * SUBLANE * 8 * ns * 4 + 6 * tt * ns * 4
    return pl.pallas_call(
        functools.partial(_s5_kernel, tt=tt),
        grid=(batch, nb, nt),
        in_specs=[pl.BlockSpec((tt, gc), lambda b, g, t: (b * nt + t, cb0 + g)),
                  pl.BlockSpec((1, gc, ns), lambda b, g, t: (g, 0, 0)),
                  pl.BlockSpec((1, gc, ns), lambda b, g, t: (g, 0, 0)),
                  pl.BlockSpec((1, ns, gc), lambda b, g, t: (g, 0, 0)),
                  pl.BlockSpec((1, ns, gc), lambda b, g, t: (g, 0, 0)),
                  pl.BlockSpec((1, 8, SUBLANE, ns), lambda b, g, t: (g, 0, 0, 0)),
                  pl.BlockSpec((1, gc), lambda b, g, t: (0, g))],
        out_specs=pl.BlockSpec((tt, gc), lambda b, g, t: (b * nt + t, g)),
        out_shape=jax.ShapeDtypeStruct((batch * seq, nb * gc), F32),
        scratch_shapes=[pltpu.VMEM((tt, ns), F32), pltpu.VMEM((tt, ns), F32), pltpu.VMEM((SUBLANE, ns), F32)],
        compiler_params=_cparams(("parallel", "parallel", "arbitrary"), est),
        name="s5_scan",
    )(u_src, bre, bim, cre, cim, pw, d_skip.reshape(1, -1).astype(F32))


S5_BLOCK = 16
S5_GROUPS_PER_STEP = 4


def _s5t_kernel(x_ref, toep_ref, ere_ref, eim_ref, fre_ref, fim_ref, coef_ref, d_ref, o_ref,
                u_ref, y_ref, er_s, ei_s, hr_s, hi_s, *, gps, nb, nb_seq):
    t_blk = S5_BLOCK
    tw = t_blk * S5_GROUP
    npair = gps // 2
    per_slab = LANE // S5_GROUP
    steps = [x_ref[pl.ds(t, nb, stride=t_blk), :] for t in range(t_blk)]
    win = lax.broadcasted_iota(jnp.int32, (nb, LANE), 1) // S5_GROUP
    in_win = [win == w for w in range(per_slab)]

    def place(pieces):
        out = None
        for w, (arr, src) in enumerate(pieces):
            sh = ((w - src) % per_slab) * S5_GROUP
            r = arr if sh == 0 else pltpu.roll(arr, sh, 1)
            out = r if out is None else jnp.where(in_win[w], r, out)
        return out

    for g in range(gps):
        for hf in range(tw // LANE):
            u_ref[g, :, hf * LANE:(hf + 1) * LANE] = place([(steps[hf * per_slab + tt], g) for tt in range(per_slab)])
    for p in range(npair):
        up = jnp.concatenate([u_ref[2 * p].astype(BF16), u_ref[2 * p + 1].astype(BF16)], axis=1)
        er_s[p] = jnp.dot(up, ere_ref[p], preferred_element_type=F32)
        ei_s[p] = jnp.dot(up, eim_ref[p], preferred_element_type=F32)
    row = lax.broadcasted_iota(jnp.int32, (SUBLANE, LANE), 0)
    groups_per_seq = nb_seq // SUBLANE

    def body(r, carry):
        off = pl.multiple_of(r * SUBLANE, SUBLANE)
        keep = jnp.where((r % groups_per_seq) == 0, 0.0, 1.0)
        new = []
        for p in range(npair):
            xr = er_s[p, pl.ds(off, SUBLANE), :]
            xi = ei_s[p, pl.ds(off, SUBLANE), :]
            for idx, k in enumerate((1, 2, 4)):
                ar, ai = coef_ref[p, 2 * idx], coef_ref[p, 2 * idx + 1]
                sr = jnp.where(row >= k, pltpu.roll(xr, k, 0), 0.0)
                si = jnp.where(row >= k, pltpu.roll(xi, k, 0), 0.0)
                xr, xi = xr + ar * sr - ai * si, xi + ar * si + ai * sr
            cr = jnp.broadcast_to(carry[p][0] * keep, (SUBLANE, LANE))
            ci = jnp.broadcast_to(carry[p][1] * keep, (SUBLANE, LANE))
            pwr, pwi = coef_ref[p, 6], coef_ref[p, 7]
            xr, xi = xr + pwr * cr - pwi * ci, xi + pwr * ci + pwi * cr
            hr_s[p, pl.ds(off, SUBLANE), :] = jnp.where(row == 0, cr, pltpu.roll(xr, 1, 0))
            hi_s[p, pl.ds(off, SUBLANE), :] = jnp.where(row == 0, ci, pltpu.roll(xi, 1, 0))
            new.append((xr[SUBLANE - 1:, :], xi[SUBLANE - 1:, :]))
        return tuple(new)

    zero = jnp.zeros((1, LANE), F32)
    lax.fori_loop(0, nb // SUBLANE, body, tuple((zero, zero) for _ in range(npair)))
    for p in range(npair):
        yc = (jnp.dot(hr_s[p].astype(BF16), fre_ref[p], preferred_element_type=F32)
              + jnp.dot(hi_s[p].astype(BF16), fim_ref[p], preferred_element_type=F32))
        for a in range(2):
            g = 2 * p + a
            u = u_ref[g]
            y = jnp.dot(u.astype(BF16), toep_ref[g], preferred_element_type=F32) + yc[:, a * tw:(a + 1) * tw]
            y_ref[g] = jax.nn.gelu(y + d_ref[g] * u, approximate=True)
    for t in range(t_blk):
        hf, tt = divmod(t, per_slab)
        o_ref[pl.ds(t, nb, stride=t_blk), :] = place(
            [(y_ref[g, :, hf * LANE:(hf + 1) * LANE], tt) for g in range(gps)])


def _s5t_params(a_re, a_im, log_step, b_re, b_im, c_re, c_im, d_skip):
    g, p = a_re.shape
    t = S5_BLOCK
    lam = lax.complex(a_re.astype(F32), a_im.astype(F32))
    lam_dt = lam * jnp.exp(log_step.astype(F32))[:, None]
    lam_bar = jnp.exp(lam_dt)
    b_bar = ((lam_bar - 1.0) / lam)[..., None] * lax.complex(b_re.astype(F32), b_im.astype(F32))
    c_c = lax.complex(c_re.astype(F32), c_im.astype(F32))
    tau = jnp.arange(t, dtype=F32)
    apow = jnp.exp(lam_dt[None] * tau[:, None, None])
    apow1 = jnp.exp(lam_dt[None] * (tau + 1.0)[:, None, None])
    kern = jnp.real(jnp.einsum('gip,tgp,gpj->gtij', c_c, apow, b_bar, precision=HI))
    lag = jnp.arange(t)[None, :] - jnp.arange(t)[:, None]
    shift = (lag[None] == jnp.arange(t)[:, None, None]).astype(F32)
    toep = jnp.einsum('gtij,tsu->gsjui', kern, shift, precision=HI)
    toep = toep.reshape(g, t * S5_GROUP, t * S5_GROUP)
    e_c = (apow[::-1][:, :, None, :] * jnp.moveaxis(b_bar, 2, 1)[None]).transpose(1, 0, 2, 3)
    e_c = e_c.reshape(g, t * S5_GROUP, p)
    f_c = (c_c[:, None] * apow1.transpose(1, 0, 2)[:, :, None, :])
    f_c = jnp.moveaxis(f_c.reshape(g, t * S5_GROUP, p), 1, 2)
    eye2 = jnp.eye(2, dtype=F32)

    def pair_in(x):
        return jnp.einsum('narp,ab->narbp', x.reshape(g // 2, 2, t * S5_GROUP, p), eye2).reshape(
            g // 2, 2 * t * S5_GROUP, 2 * p)

    def pair_out(x):
        return jnp.einsum('napc,ab->napbc', x.reshape(g // 2, 2, p, t * S5_GROUP), eye2).reshape(
            g // 2, 2 * p, 2 * t * S5_GROUP)

    def planes(z):
        z = jnp.broadcast_to(z, (g, SUBLANE, p)).reshape(g // 2, 2, SUBLANE, p).transpose(0, 2, 1, 3)
        z = z.reshape(g // 2, SUBLANE, 2 * p)
        return [jnp.real(z), jnp.imag(z)]

    coef = []
    for k in (1, 2, 4):
        coef += planes(jnp.exp(lam_dt * float(t * k))[:, None, :])
    coef += planes(jnp.exp(lam_dt[:, None, :] * (float(t) * jnp.arange(1, SUBLANE + 1, dtype=F32))[None, :, None]))
    dtile = jnp.tile(d_skip.astype(F32).reshape(g, 1, S5_GROUP), (1, 1, t))
    return (toep.astype(BF16), pair_in(jnp.real(e_c)).astype(BF16), pair_in(jnp.imag(e_c)).astype(BF16),
            pair_out(jnp.real(f_c)).astype(BF16), pair_out(-jnp.imag(f_c)).astype(BF16),
            jnp.stack(coef, axis=1), dtile)


def _s5_toeplitz(u_src, col0, params, batch, seq):
    toep, ere, eim, fre, fim, coef, dtile = params
    g = toep.shape[0]
    t = S5_BLOCK
    tw = t * S5_GROUP
    ns = 2 * S5_STATE
    m = u_src.shape[0]
    nb = m // t
    gps = LANE // S5_GROUP
    npair = gps // 2
    cb0 = col0 // LANE
    est = 4 * m * LANE * 4 + 2 * gps * nb * tw * 4 + 4 * npair * nb * ns * 4 + 16 * gps * tw * tw * 2 + 12 * nb * tw * 4
    return pl.pallas_call(
        functools.partial(_s5t_kernel, gps=gps, nb=nb, nb_seq=seq // t),
        grid=(g // gps,),
        in_specs=[pl.BlockSpec((m, LANE), lambda i: (0, cb0 + i)),
                  pl.BlockSpec((gps, tw, tw), lambda i: (i, 0, 0)),
                  pl.BlockSpec((npair, 2 * tw, ns), lambda i: (i, 0, 0)),
                  pl.BlockSpec((npair, 2 * tw, ns), lambda i: (i, 0, 0)),
                  pl.BlockSpec((npair, ns, 2 * tw), lambda i: (i, 0, 0)),
                  pl.BlockSpec((npair, ns, 2 * tw), lambda i: (i, 0, 0)),
                  pl.BlockSpec((npair, 8, SUBLANE, ns), lambda i: (i, 0, 0, 0)),
                  pl.BlockSpec((gps, 1, tw), lambda i: (i, 0, 0))],
        out_specs=pl.BlockSpec((m, LANE), lambda i: (0, i)),
        out_shape=jax.ShapeDtypeStruct((m, g * S5_GROUP), F32),
        scratch_shapes=([pltpu.VMEM((gps, nb, tw), F32) for _ in range(2)]
                        + [pltpu.VMEM((npair, nb, ns), F32) for _ in range(4)]),
        compiler_params=_cparams(("parallel",), est),
        name="s5_blocks",
    )(u_src, toep, ere, eim, fre, fim, coef, dtile)


def _glu_kernel(a_ref, w_ref, g_ref, o_ref):
    z = jnp.dot(a_ref[...].astype(BF16), w_ref[...], preferred_element_type=F32)
    o_ref[...] = (g_ref[...] * jax.nn.sigmoid(z)).astype(o_ref.dtype)


def _s5_glu(g, w_glu):
    m, k = g.shape
    bm, bn, est = _mm_tiles(m, [k], k, [4], 2, True)
    return pl.pallas_call(
        _glu_kernel,
        grid=(m // bm, k // bn),
        in_specs=[pl.BlockSpec((bm, k), lambda i, j: (i, 0)),
                  pl.BlockSpec((k, bn), lambda i, j: (0, j)),
                  pl.BlockSpec((bm, bn), lambda i, j: (i, j))],
        out_specs=pl.BlockSpec((bm, bn), lambda i, j: (i, j)),
        out_shape=jax.ShapeDtypeStruct((m, k), BF16),
        compiler_params=_cparams(("parallel", "parallel"), est),
        name="s5_glu",
    )(g, w_glu, g)


def _shift_rows(h, prev, k):
    r = pltpu.roll(h, k, 0)
    p = pltpu.roll(prev, k, 0)
    row = lax.broadcasted_iota(jnp.int32, prev.shape, 0)
    top = jnp.where(row < k, p, r[:SUBLANE])
    return jnp.concatenate([top, r[SUBLANE:]], axis=0)


def _ffn_up_kernel(a_ref, wa_ref, wb_ref, cwa_ref, cwb_ref, o_ref, hbuf, carry, *, tiles_per_seq, nj):
    i, j = pl.program_id(0), pl.program_id(1)
    slot = i % 2
    rows = hbuf.shape[1]

    @pl.when(jnp.logical_and(i == 0, j == 0))
    def _():
        hbuf[...] = jnp.zeros_like(hbuf)
        carry[...] = jnp.zeros_like(carry)

    def epilogue():
        jp = jnp.maximum(j - 1, 0)
        keep = jnp.where((i % tiles_per_seq) == 0, 0.0, 1.0)
        outs = []
        for s, cw_ref in enumerate((cwa_ref, cwb_ref)):
            h = hbuf[s]
            prev = carry[1 - slot, jp, s * SUBLANE:(s + 1) * SUBLANE, :] * keep
            cw = cw_ref[...]
            outs.append(cw[2:3] * h + cw[1:2] * _shift_rows(h, prev, 1) + cw[0:1] * _shift_rows(h, prev, 2))
        o_ref[...] = (jax.nn.silu(outs[0]) * outs[1]).astype(o_ref.dtype)

    @pl.when(j < nj)
    def _():
        epilogue()
        a = a_ref[...]
        ha = jnp.dot(a, wa_ref[...].astype(BF16), preferred_element_type=F32)
        hb = jnp.dot(a, wb_ref[...].astype(BF16), preferred_element_type=F32)
        hbuf[0] = ha
        hbuf[1] = hb
        carry[slot, j, 0:SUBLANE, :] = ha[rows - SUBLANE:, :]
        carry[slot, j, SUBLANE:2 * SUBLANE, :] = hb[rows - SUBLANE:, :]

    @pl.when(j == nj)
    def _():
        epilogue()


def _ffn_up(xn, w_up, layer, conv_w, seq):
    m, k = xn.shape
    f = w_up.shape[2] // 2
    bm = _pick(seq, 1024, SUBLANE)
    bn = _pick(f, 256, LANE)
    nj = f // bn
    est = (2 * bm * k * 2 + 4 * k * bn * 4 + 2 * k * bn * 2 + 2 * bm * bn * 2 + 2 * nj * 16 * bn * 4
           + 2 * bm * bn * 4 + 12 * bm * bn * 4)
    cur = lambda j: jnp.minimum(j, nj - 1)
    prv = lambda j: jnp.maximum(j - 1, 0)
    return pl.pallas_call(
        functools.partial(_ffn_up_kernel, tiles_per_seq=seq // bm, nj=nj),
        grid=(m // bm, nj + 1),
        in_specs=[pl.BlockSpec((bm, k), lambda i, j: (i, 0)),
                  pl.BlockSpec((None, k, bn), lambda i, j: (layer, 0, cur(j))),
                  pl.BlockSpec((None, k, bn), lambda i, j: (layer, 0, cur(j) + nj)),
                  pl.BlockSpec((SUBLANE, bn), lambda i, j: (0, prv(j))),
                  pl.BlockSpec((SUBLANE, bn), lambda i, j: (0, prv(j) + nj))],
        out_specs=pl.BlockSpec((bm, bn), lambda i, j: (i, prv(j))),
        out_shape=jax.ShapeDtypeStruct((m, f), BF16),
        scratch_shapes=[pltpu.VMEM((2, bm, bn), F32), pltpu.VMEM((2, nj, 2 * SUBLANE, bn), F32)],
        compiler_params=_cparams(("arbitrary", "arbitrary"), est),
        name="ffn_up",
    )(xn, w_up, w_up, conv_w, conv_w)


def _gdn_pre_kernel(x_ref, w_ref, o_ref, *, nq, nk):
    j = pl.program_id(1)
    x = x_ref[...]
    w = w_ref[...]
    kw = 4
    row = lax.broadcasted_iota(jnp.int32, x.shape, 0)
    acc = w[kw - 1:kw] * x
    for s in range(1, kw):
        acc = acc + w[kw - 1 - s:kw - s] * jnp.where(row >= s, pltpu.roll(x, s, 0), 0.0)
    y = jax.nn.silu(acc)
    parts = []
    for c in range(x.shape[1] // GDN_DK):
        yh = y[:, c * GDN_DK:(c + 1) * GDN_DK]
        parts.append(yh * lax.rsqrt(jnp.sum(yh * yh, axis=-1, keepdims=True) + NORM_EPS))
    yn = parts[0] if len(parts) == 1 else jnp.concatenate(parts, axis=1)
    is_q = j < nq
    is_k = jnp.logical_and(j >= nq, j < nq + nk)
    o_ref[...] = jnp.where(is_q, yn * (GDN_DK ** -0.5), jnp.where(is_k, yn, y))


def _gdn_pre(src, conv_w, batch, seq, heads):
    width = conv_w.shape[1]
    cb = GDN_DK
    nq = heads * GDN_DK // cb
    est = 4 * seq * cb * 4 + 8 * seq * cb * 4
    return pl.pallas_call(
        functools.partial(_gdn_pre_kernel, nq=nq, nk=nq),
        grid=(batch, width // cb),
        in_specs=[pl.BlockSpec((seq, cb), lambda b, j: (b, j)),
                  pl.BlockSpec((SUBLANE, cb), lambda b, j: (0, j))],
        out_specs=pl.BlockSpec((seq, cb), lambda b, j: (b, j)),
        out_shape=jax.ShapeDtypeStruct((batch * seq, width), F32),
        compiler_params=_cparams(("parallel", "parallel"), est),
        name="gdn_pre",
    )(src, conv_w)


GDN_ROWS = 256
GDN_HEADS_PER_STEP = 2


def _dot_hi(a, b):
    return jnp.dot(a, b, precision=HI, preferred_element_type=F32)


def _bdot(a, b):
    return jnp.dot(a.astype(BF16), b.astype(BF16), preferred_element_type=F32)


def _bdot_nt(a, b):
    return lax.dot_general(a.astype(BF16), b.astype(BF16), (((1,), (1,)), ((), ())), preferred_element_type=F32)


def _bdot_tn(a, b):
    return lax.dot_general(a.astype(BF16), b.astype(BF16), (((0,), (0,)), ((), ())), preferred_element_type=F32)


def _gdn_head(alog, dtb, q, k, v, a_raw, b_raw, gate, ng, s, masks, rows):
    same, tril, tril_t, strict, eye, eye_f = masks
    a_raw = a_raw + dtb
    softplus = jnp.maximum(a_raw, 0.0) + jnp.log(1.0 + jnp.exp(-jnp.abs(a_raw)))
    g = -jnp.exp(jnp.zeros_like(a_raw) + alog) * softplus
    beta = jax.nn.sigmoid(b_raw)
    g_cols = jnp.broadcast_to(g, (rows, rows))
    g_rows = jnp.broadcast_to(jnp.sum(jnp.where(eye, g_cols, 0.0), axis=0, keepdims=True), (rows, rows))
    gc_col = jnp.sum(jnp.where(tril, g_rows, 0.0), axis=1, keepdims=True)
    gl_col = jnp.sum(jnp.where(same, g_rows, 0.0), axis=1, keepdims=True)
    gc_row = jnp.sum(jnp.where(tril_t, g_cols, 0.0), axis=0, keepdims=True)
    gc = jnp.broadcast_to(gc_col, (rows, LANE))
    gl = jnp.broadcast_to(gl_col, (rows, LANE))
    decay = jnp.exp(jnp.where(tril, gc_col - gc_row, MASK_VALUE))
    kb = k * beta
    vb = v * beta
    lmat = jnp.where(strict, _bdot_nt(kb, k) * decay, 0.0)
    tinv = eye_f - lmat
    pw = lmat
    for _ in range(int(math.log2(CHUNK)) - 1):
        pw = _bdot(pw, pw)
        tinv = tinv + _bdot(tinv, pw)
    eg = jnp.exp(gc)
    sol = _bdot(tinv, jnp.concatenate([vb, kb * eg], axis=1))
    u_c, w_c = sol[:, :GDN_DV], sol[:, GDN_DV:]
    intra = _bdot_nt(q, k) * decay
    q_dec = q * eg
    k_dec = k * jnp.exp(gl - gc)
    egl = jnp.exp(gl)
    iuw = _bdot(intra, sol)
    o_loc = iuw[:, :GDN_DV]
    q_eff = q_dec - iuw[:, GDN_DV:]
    outs = []
    for c in range(rows // CHUNK):
        lo, hi = c * CHUNK, (c + 1) * CHUNK
        kuw = _bdot_tn(k_dec[lo:hi], sol[lo:hi])
        outs.append(_bdot(q_eff[lo:hi], s) + o_loc[lo:hi])
        s = s * egl[lo:lo + 1, :] + kuw[:, :GDN_DV] - _bdot(kuw[:, GDN_DV:], s)
    o = jnp.concatenate(outs, axis=0)
    o = o * lax.rsqrt(jnp.mean(o * o, axis=-1, keepdims=True) + NORM_EPS) * ng
    return o * jax.nn.silu(gate), s


def _gdn_kernel(alog_ref, dtb_ref, q_ref, k_ref, v_ref, ab_ref, gate_ref, ng_ref, o_ref, state, *,
                rows, hps, heads):
    hb = pl.program_id(1)
    r = pl.program_id(2)
    ab = ab_ref[...]
    ab_lane = lax.broadcasted_iota(jnp.int32, ab.shape, 1)

    @pl.when(r == 0)
    def _():
        state[...] = jnp.zeros_like(state)

    ri = lax.broadcasted_iota(jnp.int32, (rows, rows), 0)
    ci = lax.broadcasted_iota(jnp.int32, (rows, rows), 1)
    same = (ri // CHUNK) == (ci // CHUNK)
    tril = jnp.logical_and(same, ci <= ri)
    tril_t = jnp.logical_and(same, ri <= ci)
    strict = jnp.logical_and(same, ci < ri)
    eye = ri == ci
    masks = (same, tril, tril_t, strict, eye, jnp.where(eye, 1.0, 0.0))
    ng = ng_ref[...]
    for i in range(hps):
        cs = slice(i * GDN_DK, (i + 1) * GDN_DK)
        h = hb * hps + i
        a_raw = jnp.sum(jnp.where(ab_lane == h, ab, 0.0), axis=1, keepdims=True)
        b_raw = jnp.sum(jnp.where(ab_lane == heads + h, ab, 0.0), axis=1, keepdims=True)
        o, s = _gdn_head(alog_ref[h], dtb_ref[h], q_ref[:, cs], k_ref[:, cs], v_ref[:, cs], a_raw, b_raw,
                         gate_ref[:, cs], ng, state[i], masks, rows)
        state[i] = s
        o_ref[:, cs] = o.astype(o_ref.dtype)


def _gdn_core(qkv, ab, gate_src, gate_col0, a_log, dt_bias, norm_g, batch, seq, heads):
    rows = _pick(seq, GDN_ROWS, CHUNK)
    nr = seq // rows
    hps = GDN_HEADS_PER_STEP if heads % GDN_HEADS_PER_STEP == 0 else 1
    nhb = heads // hps
    wblk = hps * GDN_DK
    gcb = gate_col0 // wblk
    est = hps * (12 * rows * LANE * 4 + 16 * rows * rows * 4 + 8 * rows * LANE * 4)
    tok = lambda b, h, r: b * nr + r
    return pl.pallas_call(
        functools.partial(_gdn_kernel, rows=rows, hps=hps, heads=heads),
        grid=(batch, nhb, nr),
        in_specs=[pl.BlockSpec(memory_space=pltpu.SMEM),
                  pl.BlockSpec(memory_space=pltpu.SMEM),
                  pl.BlockSpec((rows, wblk), lambda b, h, r: (tok(b, h, r), h)),
                  pl.BlockSpec((rows, wblk), lambda b, h, r: (tok(b, h, r), nhb + h)),
                  pl.BlockSpec((rows, wblk), lambda b, h, r: (tok(b, h, r), 2 * nhb + h)),
                  pl.BlockSpec((rows, LANE), lambda b, h, r: (tok(b, h, r), 0)),
                  pl.BlockSpec((rows, wblk), lambda b, h, r: (tok(b, h, r), gcb + h)),
                  pl.BlockSpec((1, GDN_DV), lambda b, h, r: (0, 0))],
        out_specs=pl.BlockSpec((rows, wblk), lambda b, h, r: (tok(b, h, r), h)),
        out_shape=jax.ShapeDtypeStruct((batch * seq, heads * GDN_DV), BF16),
        scratch_shapes=[pltpu.VMEM((hps, GDN_DK, GDN_DV), F32)],
        compiler_params=_cparams(("parallel", "parallel", "arbitrary"), est),
        name="gdn_core",
    )(a_log.astype(F32), dt_bias.astype(F32), qkv, qkv, qkv, ab, gate_src,
      norm_g.reshape(1, GDN_DV).astype(F32))


SB_TQ = 512
SB_TK = 256
SB_HEADS_PER_STEP = 2


def _sb_kernel(q_ref, kt_ref, v_ref, o_ref, acc_s, c_s, *, tq, tk, hps):
    i = pl.program_id(2)
    r = tq // tk
    ur = lax.broadcasted_iota(jnp.int32, (2 * tk, tk), 0) % tk
    uc = lax.broadcasted_iota(jnp.int32, (2 * tk, tk), 1)
    upper2 = jnp.where(ur > uc, 1.0, 0.0).astype(BF16)
    ri = lax.broadcasted_iota(jnp.int32, (tq, tk), 0)
    ci = lax.broadcasted_iota(jnp.int32, (tq, tk), 1)
    acc_s[...] = jnp.zeros_like(acc_s)
    c_s[...] = jnp.zeros_like(c_s)

    def cols(hh):
        return slice(hh * SB_D, (hh + 1) * SB_D)

    def logits(hh, j):
        z = jnp.dot(q_ref[:, cols(hh)], kt_ref[j, cols(hh), :], preferred_element_type=F32)
        return z, -(jnp.maximum(z, 0.0) + jnp.log(1.0 + jnp.exp2(-jnp.abs(z))) * LOG2E)

    def weights(z, lm, c):
        hi = lm.astype(BF16)
        lo = (lm - hi.astype(F32)).astype(BF16)
        rest = jnp.dot(jnp.concatenate([hi, lo], axis=1), upper2, preferred_element_type=F32)
        return jnp.exp2(z + lm + rest + c)

    def values(hh, j):
        return v_ref[pl.ds(pl.multiple_of(j * tk, tk), tk), cols(hh)]

    for d in range(r - 1, -1, -1):
        j = i * r + d
        before = (ci + d * tk) < ri
        for hh in range(hps):
            z, lm = logits(hh, j)
            lm = jnp.where(before, lm, 0.0)
            w = jnp.where(before, weights(z, lm, c_s[hh]), 0.0)
            acc_s[:, cols(hh)] += jnp.dot(w.astype(BF16), values(hh, j), preferred_element_type=F32)
            c_s[hh] += jnp.sum(lm, axis=1, keepdims=True)

    def single(j):
        for hh in range(hps):
            z, lm = logits(hh, j)
            w = weights(z, lm, c_s[hh])
            acc_s[:, cols(hh)] += jnp.dot(w.astype(BF16), values(hh, j), preferred_element_type=F32)
            c_s[hh] += jnp.sum(lm, axis=1, keepdims=True)

    def pair(p, carry):
        ja = i * r - 1 - 2 * p
        jb = ja - 1
        for hh in range(hps):
            za, lma = logits(hh, ja)
            zb, lmb = logits(hh, jb)
            c = c_s[hh]
            ta = jnp.sum(lma, axis=1, keepdims=True)
            wa = weights(za, lma, c)
            wb = weights(zb, lmb, c + ta)
            acc_s[:, cols(hh)] += (jnp.dot(wa.astype(BF16), values(hh, ja), preferred_element_type=F32)
                                   + jnp.dot(wb.astype(BF16), values(hh, jb), preferred_element_type=F32))
            c_s[hh] = c + ta + jnp.sum(lmb, axis=1, keepdims=True)
        return carry

    n_full = i * r
    lax.fori_loop(0, n_full // 2, pair, 0)
    if r % 2 == 1:
        @pl.when(n_full % 2 == 1)
        def _():
            single(0)
    o_ref[...] = acc_s[...].astype(o_ref.dtype)


def _sb_tiles(seq):
    tq = _pick(seq, SB_TQ, LANE)
    return tq, _pick(tq, SB_TK, LANE)


def _sb_attention(qv, kt, batch, seq, heads):
    tq, tk = _sb_tiles(seq)
    nq, nk = seq // tq, seq // tk
    hps = SB_HEADS_PER_STEP if heads % SB_HEADS_PER_STEP == 0 else 1
    nhb = heads // hps
    wblk = hps * SB_D
    est = hps * (4 * tq * SB_D * 2 + 4 * seq * SB_D * 2 + 2 * tq * SB_D * 4 + 24 * tq * tk * 4)
    return pl.pallas_call(
        functools.partial(_sb_kernel, tq=tq, tk=tk, hps=hps),
        grid=(batch, nhb, nq),
        in_specs=[pl.BlockSpec((tq, wblk), lambda b, h, i: (b * nq + i, h)),
                  pl.BlockSpec((nk, wblk, tk), lambda b, h, i: (b, h, 0)),
                  pl.BlockSpec((seq, wblk), lambda b, h, i: (b, nhb + h))],
        out_specs=pl.BlockSpec((tq, wblk), lambda b, h, i: (b * nq + i, h)),
        out_shape=jax.ShapeDtypeStruct((batch * seq, heads * SB_D), BF16),
        scratch_shapes=[pltpu.VMEM((tq, wblk), F32), pltpu.VMEM((hps, tq, 1), F32)],
        compiler_params=_cparams(("parallel", "parallel", "arbitrary"), est),
        name="sb_attention",
    )(qv, kt, qv)


def _pad_cols(w, n):
    return jnp.pad(w, ((0, 0), (0, n - w.shape[1])))


def _ffn(h, ln_g, w_up, layer, conv_w, w_down_bf16, seq):
    xn = _rmsnorm(h, ln_g, BF16)
    conv8 = jnp.pad(conv_w.astype(F32), ((0, SUBLANE - conv_w.shape[0]), (0, 0)))
    act = _ffn_up(xn, w_up, layer, conv8, seq)
    return _matmul([(act, w_down_bf16)], F32, residual=h, b_layer=layer)


def _even_layer(h, ln_g, w_in, w_out, q_norm, kv_norm, w_uq, w_ukv,
                a_re, a_im, log_step, b_re, b_im, c_re, c_im, d_skip, w_glu, batch, seq):
    d = h.shape[1]
    q_rank, kv_rank = q_norm.shape[0], kv_norm.shape[0]
    heads = w_uq.shape[1] // (MLA_NOPE + MLA_ROPE)
    s5_width = d_skip.shape[0]
    o_kr = q_rank + kv_rank
    o_s5 = o_kr + MLA_ROPE
    latent_w = -(-(o_kr + LANE) // s5_width) * s5_width if s5_width >= LANE else o_kr + LANE
    w_lat = _pad_cols(w_in[:, :o_s5], latent_w)
    w_cat = jnp.concatenate([w_lat, w_in[:, o_s5:]], axis=1).astype(BF16)
    hn = _rmsnorm(h, ln_g, BF16)
    proj = _matmul([(hn, w_cat)], F32)
    cq = _rmsnorm(proj, q_norm, BF16, col0=0, width=q_rank)
    ckv = _rmsnorm(proj, kv_norm, BF16, col0=q_rank, width=kv_rank)
    q_scale = (MLA_NOPE + MLA_ROPE) ** -0.5 * LOG2E
    w_uq_p = jnp.pad(w_uq.reshape(q_rank, heads, MLA_NOPE + MLA_ROPE) * q_scale,
                     ((0, 0), (0, 0), (0, MLA_QK - MLA_NOPE - MLA_ROPE))).reshape(q_rank, heads * MLA_QK)
    q = _matmul([(cq, w_uq_p.astype(BF16))], F32)
    tq = _mla_tile(seq)
    w_kv = w_ukv.reshape(kv_rank, heads, MLA_NOPE + MLA_V)
    w_kn = w_kv[:, :, :MLA_NOPE].reshape(kv_rank, heads * MLA_NOPE).astype(BF16)
    w_v = w_kv[:, :, MLA_NOPE:].reshape(kv_rank, heads * MLA_V).astype(BF16)
    knt = _matmul_keys_t(ckv, w_kn, tq)
    v = _matmul([(ckv, w_v)], BF16)
    cosf, sinf = _rope_tables(seq)
    krt = _rope_k(proj, o_kr, cosf, sinf, seq, tq)
    out_a = _mla_attention(q, knt, krt, v, cosf, sinf, batch, seq, heads)
    params = _s5t_params(a_re, a_im, log_step, b_re, b_im, c_re, c_im, d_skip)
    g = _s5_toeplitz(proj, latent_w, params, batch, seq)
    out_b = _s5_glu(g, w_glu.astype(BF16))
    na = heads * MLA_V
    w_out_all, w_out_layer = w_out
    return _matmul([(out_a, w_out_all), (out_b, w_out_all)], F32, residual=h,
                   b_layer=w_out_layer, b_row0=[0, na])


def _odd_layer(h, ln_g, w_in, w_out, conv_w, a_log, dt_bias, norm_g, batch, seq):
    heads = a_log.shape[0]
    qkv_w = heads * (2 * GDN_DK + GDN_DV)
    o2 = qkv_w + 2 * heads
    o3 = o2 + heads * GDN_DV
    hn = _rmsnorm(h, ln_g, BF16)
    w_t = jnp.swapaxes(w_in, 0, 1)
    w_main = jnp.concatenate([w_t[:qkv_w], w_t[o2:o3]], axis=0).astype(BF16)
    w_ab = jnp.pad(w_t[qkv_w:o2], ((0, LANE - (o2 - qkv_w)), (0, 0))).astype(BF16)
    sb_heads = (w_in.shape[1] - o3) // (3 * SB_D)
    nsq = sb_heads * SB_D
    w_sb_qv = jnp.concatenate([w_t[o3:o3 + nsq] * (SB_D ** -0.5 * LOG2E), w_t[o3 + 2 * nsq:]], axis=0).astype(BF16)
    w_sb_k = w_t[o3 + nsq:o3 + 2 * nsq].astype(BF16)
    main = _matmul([(hn, w_main)], F32, b_nk=True)
    ab = _matmul([(hn, w_ab)], F32, b_nk=True)
    sb_qv = _matmul([(hn, w_sb_qv)], BF16, b_nk=True)
    sb_kt = _matmul_keys_t(hn, w_sb_k, _sb_tiles(seq)[1], b_nk=True)
    conv8 = jnp.pad(conv_w.astype(F32), ((0, SUBLANE - conv_w.shape[0]), (0, 0)))
    qkv = _gdn_pre(main, conv8, batch, seq, heads)
    out_c = _gdn_core(qkv, ab, main, qkv_w, a_log, dt_bias, norm_g, batch, seq, heads)
    out_d = _sb_attention(sb_qv, sb_kt, batch, seq, sb_heads)
    nc = heads * GDN_DV
    w_out_all, w_out_layer = w_out
    return _matmul([(out_c, w_out_all), (out_d, w_out_all)], F32, residual=h,
                   b_layer=w_out_layer, b_row0=[0, nc])


def kernel(x, ln_mix, ln_ffn, ln_final, ffn_w_up, ffn_conv, ffn_w_down, ev_w_in, ev_w_out, mla_q_norm,
           mla_kv_norm, mla_w_uq, mla_w_ukv, s5_a_re, s5_a_im, s5_log_step, s5_b_re, s5_b_im, s5_c_re,
           s5_c_im, s5_d, s5_w_glu, od_w_in, od_w_out, gdn_conv, gdn_a_log, gdn_dt_bias, gdn_norm):
    batch, seq, d = x.shape
    h = x.reshape(batch * seq, d).astype(F32)
    ffn_w_up = ffn_w_up.astype(F32)
    w_down_bf16 = ffn_w_down.astype(BF16)
    for layer in range(ln_mix.shape[0]):
        i = layer // 2
        if layer % 2 == 0:
            h = _even_layer(h, ln_mix[layer], ev_w_in[i], (ev_w_out.astype(F32), i), mla_q_norm[i], mla_kv_norm[i],
                            mla_w_uq[i], mla_w_ukv[i], s5_a_re[i], s5_a_im[i], s5_log_step[i], s5_b_re[i],
                            s5_b_im[i], s5_c_re[i], s5_c_im[i], s5_d[i], s5_w_glu[i], batch, seq)
        else:
            h = _odd_layer(h, ln_mix[layer], od_w_in[i], (od_w_out.astype(F32), i), gdn_conv[i], gdn_a_log[i],
                           gdn_dt_bias[i], gdn_norm[i], batch, seq)
        h = _ffn(h, ln_ffn[layer], ffn_w_up, layer, ffn_conv[layer], w_down_bf16, seq)
    return _rmsnorm(h, ln_final, x.dtype).reshape(batch, seq, d)
```

```python
import functools
import math

import jax
import jax.numpy as jnp
from jax import lax
from jax.experimental import pallas as pl
from jax.experimental.pallas import tpu as pltpu

F32 = jnp.float32
BF16 = jnp.bfloat16
HI = lax.Precision.HIGHEST

CHUNK = 64
NORM_EPS = 1e-6
MLA_NOPE = 128
MLA_ROPE = 64
MLA_V = 128
MLA_QK = 256
ROPE_BASE = 10000.0
S5_GROUP = 16
S5_STATE = 64
GDN_DK = 128
GDN_DV = 128
SB_D = 128

LANE = 128
SUBLANE = 8
VMEM_LIMIT_MAX = 58 * 1024 * 1024
VMEM_LIMIT_MIN = 32 * 1024 * 1024
MASK_VALUE = -1e30
LOG2E = math.log2(math.e)


def _cparams(sem, vmem_est):
    limit = int(min(max(vmem_est * 5 // 4 + (4 << 20), VMEM_LIMIT_MIN), VMEM_LIMIT_MAX))
    return pltpu.CompilerParams(dimension_semantics=sem, vmem_limit_bytes=limit)


def _pick(n, pref, mult):
    if n <= pref:
        return n
    t = (pref // mult) * mult
    while t >= mult:
        if n % t == 0:
            return t
        t -= mult
    return n


def _rmsnorm_kernel(x_ref, g_ref, o_ref):
    x = x_ref[...].astype(F32)
    ms = jnp.mean(x * x, axis=-1, keepdims=True)
    o_ref[...] = (x * lax.rsqrt(ms + NORM_EPS) * g_ref[...]).astype(o_ref.dtype)


def _rmsnorm(x, g, out_dtype, *, col0=0, width=None):
    m = x.shape[0]
    width = x.shape[1] if width is None else width
    tm = _pick(m, 256, SUBLANE)
    cb = col0 // width
    est = 2 * tm * width * (x.dtype.itemsize + jnp.dtype(out_dtype).itemsize) + 4 * tm * width * 4
    return pl.pallas_call(
        _rmsnorm_kernel,
        grid=(m // tm,),
        in_specs=[pl.BlockSpec((tm, width), lambda i: (i, cb)),
                  pl.BlockSpec((1, width), lambda i: (0, 0))],
        out_specs=pl.BlockSpec((tm, width), lambda i: (i, 0)),
        out_shape=jax.ShapeDtypeStruct((m, width), out_dtype),
        compiler_params=_cparams(("parallel",), est),
        name="rmsnorm",
    )(x, g.reshape(1, width).astype(F32))


def _mm_kernel(*refs, n_pairs, has_res, b_nk):
    o_ref = refs[-1]
    acc = None
    for p in range(n_pairs):
        a = refs[2 * p][...].astype(BF16)
        b = refs[2 * p + 1][...].astype(BF16)
        if b_nk:
            d = lax.dot_general(a, b, (((1,), (1,)), ((), ())), preferred_element_type=F32)
        else:
            d = jnp.dot(a, b, preferred_element_type=F32)
        acc = d if acc is None else acc + d
    if has_res:
        acc = acc + refs[2 * n_pairs][...]
    o_ref[...] = acc.astype(o_ref.dtype)


def _mm_tiles(m, ks, n, a_bytes, out_bytes, has_res, budget=52 << 20, b_bytes=2):
    for bm_p, bn_p in ((1024, 1024), (1024, 512), (512, 512), (512, 256), (256, 256), (256, 128), (128, 128)):
        bm = _pick(m, bm_p, SUBLANE)
        bn = _pick(n, bn_p, LANE)
        est = bm * bn * 4
        for k, ab in zip(ks, a_bytes):
            est += 2 * (bm * k * ab + k * bn * b_bytes) + (k * bn * 2 if b_bytes > 2 else 0)
        est += 2 * bm * bn * (out_bytes + (4 if has_res else 0))
        if est <= budget:
            return bm, bn, est
    return bm, bn, est


def _matmul(pairs, out_dtype, residual=None, *, a_col0=None, b_layer=None, b_row0=None, b_nk=False):
    m = pairs[0][0].shape[0]
    n = pairs[0][1].shape[0] if b_nk else pairs[0][1].shape[-1]
    if b_row0 is not None:
        ks = [a.shape[1] for a, _ in pairs]
    else:
        ks = [b.shape[1] if b_nk else b.shape[-2] for _, b in pairs]
        b_row0 = [0] * len(pairs)
    a_col0 = [0] * len(pairs) if a_col0 is None else a_col0
    a_bytes = [a.dtype.itemsize for a, _ in pairs]
    bm, bn, est = _mm_tiles(m, ks, n, a_bytes, jnp.dtype(out_dtype).itemsize, residual is not None,
                            b_bytes=pairs[0][1].dtype.itemsize)
    in_specs, args = [], []
    for (a, b), k, c0, r0 in zip(pairs, ks, a_col0, b_row0):
        cb = c0 // k
        rb = r0 // k
        in_specs.append(pl.BlockSpec((bm, k), lambda i, j, cb=cb: (i, cb)))
        if b_nk:
            in_specs.append(pl.BlockSpec((bn, k), lambda i, j: (j, 0)))
        elif b_layer is None:
            in_specs.append(pl.BlockSpec((k, bn), lambda i, j: (0, j)))
        else:
            in_specs.append(pl.BlockSpec((None, k, bn), lambda i, j, rb=rb: (b_layer, rb, j)))
        args += [a, b]
    if residual is not None:
        in_specs.append(pl.BlockSpec((bm, bn), lambda i, j: (i, j)))
        args.append(residual)
    return pl.pallas_call(
        functools.partial(_mm_kernel, n_pairs=len(pairs), has_res=residual is not None, b_nk=b_nk),
        grid=(m // bm, n // bn),
        in_specs=in_specs,
        out_specs=pl.BlockSpec((bm, bn), lambda i, j: (i, j)),
        out_shape=jax.ShapeDtypeStruct((m, n), out_dtype),
        compiler_params=_cparams(("parallel", "parallel"), est),
        name="matmul",
    )(*args)


def _mm_t_kernel(a_ref, b_ref, o_ref, acc_ref, *, tk, b_nk):
    a = a_ref[...].astype(BF16)
    if b_nk:
        acc_ref[...] = lax.dot_general(a, b_ref[...], (((1,), (1,)), ((), ())), preferred_element_type=F32)
    else:
        acc_ref[...] = jnp.dot(a, b_ref[...], preferred_element_type=F32)
    res_t = jnp.transpose(acc_ref[...])
    for c in range(o_ref.shape[0]):
        o_ref[c] = res_t[:, c * tk:(c + 1) * tk].astype(o_ref.dtype)


def _matmul_keys_t(a, b, tk, *, b_nk=False):
    m, k = a.shape
    n = b.shape[0] if b_nk else b.shape[1]
    bm = _pick(m, 1024, tk)
    bn = _pick(n, 512, LANE)
    est = 2 * (bm * k * a.dtype.itemsize + k * bn * 2 + bm * bn * 2) + 3 * bm * bn * 4
    return pl.pallas_call(
        functools.partial(_mm_t_kernel, tk=tk, b_nk=b_nk),
        grid=(m // bm, n // bn),
        in_specs=[pl.BlockSpec((bm, k), lambda i, j: (i, 0)),
                  pl.BlockSpec((bn, k), lambda i, j: (j, 0)) if b_nk else
                  pl.BlockSpec((k, bn), lambda i, j: (0, j))],
        out_specs=pl.BlockSpec((bm // tk, bn, tk), lambda i, j: (i, j, 0)),
        out_shape=jax.ShapeDtypeStruct((m // tk, n, tk), BF16),
        scratch_shapes=[pltpu.VMEM((bm, bn), F32)],
        compiler_params=_cparams(("parallel", "parallel"), est),
        name="matmul_keys_t",
    )(a, b)


def _rope_tables(seq):
    half = MLA_ROPE // 2
    inv_freq = ROPE_BASE ** (-jnp.arange(half, dtype=F32) / half)
    ang = jnp.arange(seq, dtype=F32)[:, None] * inv_freq[None, :]
    cos, sin = jnp.cos(ang), jnp.sin(ang)
    zeros = jnp.zeros((seq, LANE - MLA_ROPE), F32)
    return (jnp.concatenate([cos, cos, zeros], axis=1),
            jnp.concatenate([-sin, sin, zeros], axis=1))


def _rope_lanes(x, cosf, sinf):
    half = MLA_ROPE // 2
    lane = lax.broadcasted_iota(jnp.int32, x.shape, 1)
    swapped = jnp.where(lane < half, pltpu.roll(x, LANE - half, 1), pltpu.roll(x, half, 1))
    return x * cosf + swapped * sinf


def _rope_kernel(x_ref, cos_ref, sin_ref, o_ref):
    o_ref[0] = jnp.transpose(_rope_lanes(x_ref[...], cos_ref[...], sin_ref[...])).astype(o_ref.dtype)


def _rope_k(x, col0, cosf, sinf, seq, tq):
    m = x.shape[0]
    per_seq = seq // tq
    cb = col0 // LANE
    return pl.pallas_call(
        _rope_kernel,
        grid=(m // tq,),
        in_specs=[pl.BlockSpec((tq, LANE), lambda i: (i, cb)),
                  pl.BlockSpec((tq, LANE), lambda i: (i % per_seq, 0)),
                  pl.BlockSpec((tq, LANE), lambda i: (i % per_seq, 0))],
        out_specs=pl.BlockSpec((1, LANE, tq), lambda i: (i, 0, 0)),
        out_shape=jax.ShapeDtypeStruct((m // tq, LANE, tq), BF16),
        compiler_params=_cparams(("parallel",), 16 * tq * LANE * 4),
        name="rope_k",
    )(x, cosf, sinf)


def _mla_kernel(q_ref, knt_ref, krt_ref, v_ref, cos_ref, sin_ref, o_ref, qs, m_s, l_s, acc_s, *, tq, hps):
    i = pl.program_id(2)
    cos, sin = cos_ref[...], sin_ref[...]
    for hh in range(hps):
        q = q_ref[:, hh * MLA_QK:(hh + 1) * MLA_QK]
        qs[hh, :, :MLA_NOPE] = q[:, :MLA_NOPE].astype(BF16)
        qs[hh, :, MLA_NOPE:] = _rope_lanes(q[:, MLA_NOPE:], cos, sin).astype(BF16)

    def vcols(hh):
        return slice(hh * MLA_V, (hh + 1) * MLA_V)

    def tile(hh, j):
        kt = jnp.concatenate([knt_ref[j, hh * MLA_NOPE:(hh + 1) * MLA_NOPE, :], krt_ref[j]], axis=0)
        s = jnp.dot(qs[hh], kt, preferred_element_type=F32)
        return s, v_ref[pl.ds(pl.multiple_of(j * tq, tq), tq), vcols(hh)]

    rc = lax.broadcasted_iota(jnp.int32, (tq, tq), 0) // CHUNK
    cc = lax.broadcasted_iota(jnp.int32, (tq, tq), 1) // CHUNK
    for hh in range(hps):
        s, v = tile(hh, i)
        s = jnp.where(cc <= rc, s, MASK_VALUE)
        m = jnp.max(s, axis=1, keepdims=True)
        p = jnp.exp2(s - m)
        m_s[hh] = m
        l_s[hh] = jnp.sum(p, axis=1, keepdims=True)
        acc_s[:, vcols(hh)] = jnp.dot(p.astype(BF16), v, preferred_element_type=F32)

    def body(j, carry):
        for hh in range(hps):
            s, v = tile(hh, j)
            m_prev = m_s[hh]
            m_new = jnp.maximum(m_prev, jnp.max(s, axis=1, keepdims=True))
            alpha = jnp.exp2(m_prev - m_new)
            p = jnp.exp2(s - m_new)
            l_s[hh] = alpha * l_s[hh] + jnp.sum(p, axis=1, keepdims=True)
            acc_s[:, vcols(hh)] = (alpha * acc_s[:, vcols(hh)]
                                   + jnp.dot(p.astype(BF16), v, preferred_element_type=F32))
            m_s[hh] = m_new
        return carry

    lax.fori_loop(0, i, body, 0)
    for hh in range(hps):
        o_ref[:, vcols(hh)] = (acc_s[:, vcols(hh)] / l_s[hh]).astype(o_ref.dtype)


MLA_TILE = 1024
MLA_HEADS_PER_STEP = 2


def _mla_tile(seq):
    return _pick(seq, MLA_TILE, CHUNK)


def _mla_attention(q, knt, krt, v, cosf, sinf, batch, seq, heads):
    tq = _mla_tile(seq)
    nq = seq // tq
    hps = MLA_HEADS_PER_STEP if heads % MLA_HEADS_PER_STEP == 0 else 1
    est = hps * (2 * tq * MLA_QK * 4 + 2 * seq * (MLA_QK + LANE) * 2 + 4 * tq * LANE * 4 + 2 * tq * LANE * 2
                 + tq * MLA_QK * 2 + 3 * tq * LANE * 4 + 3 * tq * tq * 4)
    return pl.pallas_call(
        functools.partial(_mla_kernel, tq=tq, hps=hps),
        grid=(batch, heads // hps, nq),
        in_specs=[pl.BlockSpec((tq, hps * MLA_QK), lambda b, h, i: (b * nq + i, h)),
                  pl.BlockSpec((nq, hps * MLA_NOPE, tq), lambda b, h, i: (b, h, 0)),
                  pl.BlockSpec((nq, LANE, tq), lambda b, h, i: (b, 0, 0)),
                  pl.BlockSpec((seq, hps * MLA_V), lambda b, h, i: (b, h)),
                  pl.BlockSpec((tq, LANE), lambda b, h, i: (i, 0)),
                  pl.BlockSpec((tq, LANE), lambda b, h, i: (i, 0))],
        out_specs=pl.BlockSpec((tq, hps * MLA_V), lambda b, h, i: (b * nq + i, h)),
        out_shape=jax.ShapeDtypeStruct((batch * seq, heads * MLA_V), BF16),
        scratch_shapes=[pltpu.VMEM((hps, tq, MLA_QK), BF16), pltpu.VMEM((hps, tq, 1), F32),
                        pltpu.VMEM((hps, tq, 1), F32), pltpu.VMEM((tq, hps * MLA_V), F32)],
        compiler_params=_cparams(("parallel", "parallel", "arbitrary"), est),
        name="mla_attention",
    )(q, knt, krt, v, cosf, sinf)


S5_GROUPS_PER_BLOCK = 8


def _s5_kernel(u_ref, bre_ref, bim_ref, cre_ref, cim_ref, pw_ref, d_ref, o_ref, sre, sim, car, *, tt):
    t = pl.program_id(2)

    @pl.when(t == 0)
    def _():
        car[...] = jnp.zeros_like(car)

    u = u_ref[...]
    ub = u.astype(BF16)
    sre[...] = jnp.dot(ub, bre_ref[0], preferred_element_type=F32)
    sim[...] = jnp.dot(ub, bim_ref[0], preferred_element_type=F32)
    ns = sre.shape[1]
    row = lax.broadcasted_iota(jnp.int32, (SUBLANE, ns), 0)
    steps = [(1, pw_ref[0, 0], pw_ref[0, 1]), (2, pw_ref[0, 2], pw_ref[0, 3]), (4, pw_ref[0, 4], pw_ref[0, 5])]
    pwr, pwi = pw_ref[0, 6], pw_ref[0, 7]

    def body(r, carry):
        cr, ci = carry
        off = pl.multiple_of(r * SUBLANE, SUBLANE)
        xr = sre[pl.ds(off, SUBLANE), :]
        xi = sim[pl.ds(off, SUBLANE), :]
        for k, ar, ai in steps:
            sr = jnp.where(row >= k, pltpu.roll(xr, k, 0), 0.0)
            si = jnp.where(row >= k, pltpu.roll(xi, k, 0), 0.0)
            xr, xi = xr + ar * sr - ai * si, xi + ar * si + ai * sr
        xr, xi = xr + pwr * cr - pwi * ci, xi + pwr * ci + pwi * cr
        sre[pl.ds(off, SUBLANE), :] = xr
        sim[pl.ds(off, SUBLANE), :] = xi
        return xr[SUBLANE - 1:, :], xi[SUBLANE - 1:, :]

    cr, ci = lax.fori_loop(0, tt // SUBLANE, body, (car[0:1, :], car[1:2, :]))
    car[0:1, :] = cr
    car[1:2, :] = ci
    y = (jnp.dot(sre[...].astype(BF16), cre_ref[0], preferred_element_type=F32)
         - jnp.dot(sim[...].astype(BF16), cim_ref[0], preferred_element_type=F32))
    y = y + d_ref[...] * u
    o_ref[...] = jax.nn.gelu(y, approximate=True)


def _s5_params(a_re, a_im, log_step, b_re, b_im, c_re, c_im):
    g, p = a_re.shape
    gb = min(S5_GROUPS_PER_BLOCK, g)
    nb = g // gb
    lam = lax.complex(a_re.astype(F32), a_im.astype(F32))
    step = jnp.exp(log_step.astype(F32))[:, None]
    lam_dt = lam * step
    lam_bar = jnp.exp(lam_dt)
    b_bar = ((lam_bar - 1.0) / lam)[..., None] * lax.complex(b_re.astype(F32), b_im.astype(F32))
    c_c = lax.complex(c_re.astype(F32), c_im.astype(F32))
    eye = jnp.eye(gb, dtype=F32)

    def bd_in(x):
        x = x.reshape(nb, gb, p, S5_GROUP)
        return jnp.einsum('ngpi,gh->ngihp', x, eye).reshape(nb, gb * S5_GROUP, gb * p)

    def bd_out(x):
        x = x.reshape(nb, gb, S5_GROUP, p)
        return jnp.einsum('ngip,gh->ngphi', x, eye).reshape(nb, gb * p, gb * S5_GROUP)

    def rows(z):
        z = z.reshape(nb, 1, gb * p)
        return (jnp.broadcast_to(jnp.real(z), (nb, SUBLANE, gb * p)),
                jnp.broadcast_to(jnp.imag(z), (nb, SUBLANE, gb * p)))

    planes = []
    for k in (1, 2, 4):
        planes += list(rows(jnp.exp(lam_dt * float(k))))
    pw = jnp.exp(lam_dt[None] * jnp.arange(1, SUBLANE + 1, dtype=F32)[:, None, None])
    pw = jnp.moveaxis(pw.reshape(SUBLANE, nb, gb * p), 0, 1)
    planes += [jnp.real(pw), jnp.imag(pw)]
    return (bd_in(jnp.real(b_bar)).astype(BF16), bd_in(jnp.imag(b_bar)).astype(BF16),
            bd_out(jnp.real(c_c)).astype(BF16), bd_out(jnp.imag(c_c)).astype(BF16),
            jnp.stack(planes, axis=1), gb)


def _s5_scan(u_src, col0, params, d_skip, batch, seq):
    bre, bim, cre, cim, pw, gb = params
    nb = bre.shape[0]
    gc = gb * S5_GROUP
    ns = gb * S5_STATE
    tt = _pick(seq, 512, SUBLANE)
    nt = seq // tt
    cb0 = col0 // gc
    est = 4 * tt * gc * 4 + 8 * gc * ns * 4 + 4 * SUBLANE * 8 * ns * 4 + 6 * tt * ns * 4
    return pl.pallas_call(
        functools.partial(_s5_kernel, tt=tt),
        grid=(batch, nb, nt),
        in_specs=[pl.BlockSpec((tt, gc), lambda b, g, t: (b * nt + t, cb0 + g)),
                  pl.BlockSpec((1, gc, ns), lambda b, g, t: (g, 0, 0)),
                  pl.BlockSpec((1, gc, ns), lambda b, g, t: (g, 0, 0)),
                  pl.BlockSpec((1, ns, gc), lambda b, g, t: (g, 0, 0)),
                  pl.BlockSpec((1, ns, gc), lambda b, g, t: (g, 0, 0)),
                  pl.BlockSpec((1, 8, SUBLANE, ns), lambda b, g, t: (g, 0, 0, 0)),
                  pl.BlockSpec((1, gc), lambda b, g, t: (0, g))],
        out_specs=pl.BlockSpec((tt, gc), lambda b, g, t: (b * nt + t, g)),
        out_shape=jax.ShapeDtypeStruct((batch * seq, nb * gc), F32),
        scratch_shapes=[pltpu.VMEM((tt, ns), F32), pltpu.VMEM((tt, ns), F32), pltpu.VMEM((SUBLANE, ns), F32)],
        compiler_params=_cparams(("parallel", "parallel", "arbitrary"), est),
        name="s5_scan",
    )(u_src, bre, bim, cre, cim, pw, d_skip.reshape(1, -1).astype(F32))


S5_BLOCK = 16
S5_GROUPS_PER_STEP = 4


def _s5t_kernel(x_ref, toep_ref, ere_ref, eim_ref, fre_ref, fim_ref, coef_ref, d_ref, o_ref,
                u_ref, y_ref, er_s, ei_s, hr_s, hi_s, *, gps, nb, nb_seq):
    t_blk = S5_BLOCK
    tw = t_blk * S5_GROUP
    npair = gps // 2
    per_slab = LANE // S5_GROUP
    steps = [x_ref[pl.ds(t, nb, stride=t_blk), :] for t in range(t_blk)]
    win = lax.broadcasted_iota(jnp.int32, (nb, LANE), 1) // S5_GROUP
    in_win = [win == w for w in range(per_slab)]

    def place(pieces):
        out = None
        for w, (arr, src) in enumerate(pieces):
            sh = ((w - src) % per_slab) * S5_GROUP
            r = arr if sh == 0 else pltpu.roll(arr, sh, 1)
            out = r if out is None else jnp.where(in_win[w], r, out)
        return out

    for g in range(gps):
        for hf in range(tw // LANE):
            u_ref[g, :, hf * LANE:(hf + 1) * LANE] = place([(steps[hf * per_slab + tt], g) for tt in range(per_slab)])
    for p in range(npair):
        up = jnp.concatenate([u_ref[2 * p].astype(BF16), u_ref[2 * p + 1].astype(BF16)], axis=1)
        er_s[p] = jnp.dot(up, ere_ref[p], preferred_element_type=F32)
        ei_s[p] = jnp.dot(up, eim_ref[p], preferred_element_type=F32)
    row = lax.broadcasted_iota(jnp.int32, (SUBLANE, LANE), 0)
    groups_per_seq = nb_seq // SUBLANE

    def body(r, carry):
        off = pl.multiple_of(r * SUBLANE, SUBLANE)
        keep = jnp.where((r % groups_per_seq) == 0, 0.0, 1.0)
        new = []
        for p in range(npair):
            xr = er_s[p, pl.ds(off, SUBLANE), :]
            xi = ei_s[p, pl.ds(off, SUBLANE), :]
            for idx, k in enumerate((1, 2, 4)):
                ar, ai = coef_ref[p, 2 * idx], coef_ref[p, 2 * idx + 1]
                sr = jnp.where(row >= k, pltpu.roll(xr, k, 0), 0.0)
                si = jnp.where(row >= k, pltpu.roll(xi, k, 0), 0.0)
                xr, xi = xr + ar * sr - ai * si, xi + ar * si + ai * sr
            cr = jnp.broadcast_to(carry[p][0] * keep, (SUBLANE, LANE))
            ci = jnp.broadcast_to(carry[p][1] * keep, (SUBLANE, LANE))
            pwr, pwi = coef_ref[p, 6], coef_ref[p, 7]
            xr, xi = xr + pwr * cr - pwi * ci, xi + pwr * ci + pwi * cr
            hr_s[p, pl.ds(off, SUBLANE), :] = jnp.where(row == 0, cr, pltpu.roll(xr, 1, 0))
            hi_s[p, pl.ds(off, SUBLANE), :] = jnp.where(row == 0, ci, pltpu.roll(xi, 1, 0))
            new.append((xr[SUBLANE - 1:, :], xi[SUBLANE - 1:, :]))
        return tuple(new)

    zero = jnp.zeros((1, LANE), F32)
    lax.fori_loop(0, nb // SUBLANE, body, tuple((zero, zero) for _ in range(npair)))
    for p in range(npair):
        yc = (jnp.dot(hr_s[p].astype(BF16), fre_ref[p], preferred_element_type=F32)
              + jnp.dot(hi_s[p].astype(BF16), fim_ref[p], preferred_element_type=F32))
        for a in range(2):
            g = 2 * p + a
            u = u_ref[g]
            y = jnp.dot(u.astype(BF16), toep_ref[g], preferred_element_type=F32) + yc[:, a * tw:(a + 1) * tw]
            y_ref[g] = jax.nn.gelu(y + d_ref[g] * u, approximate=True)
    for t in range(t_blk):
        hf, tt = divmod(t, per_slab)
        o_ref[pl.ds(t, nb, stride=t_blk), :] = place(
            [(y_ref[g, :, hf * LANE:(hf + 1) * LANE], tt) for g in range(gps)])


def _s5t_params(a_re, a_im, log_step, b_re, b_im, c_re, c_im, d_skip):
    g, p = a_re.shape
    t = S5_BLOCK
    lam = lax.complex(a_re.astype(F32), a_im.astype(F32))
    lam_dt = lam * jnp.exp(log_step.astype(F32))[:, None]
    lam_bar = jnp.exp(lam_dt)
    b_bar = ((lam_bar - 1.0) / lam)[..., None] * lax.complex(b_re.astype(F32), b_im.astype(F32))
    c_c = lax.complex(c_re.astype(F32), c_im.astype(F32))
    tau = jnp.arange(t, dtype=F32)
    apow = jnp.exp(lam_dt[None] * tau[:, None, None])
    apow1 = jnp.exp(lam_dt[None] * (tau + 1.0)[:, None, None])
    kern = jnp.real(jnp.einsum('gip,tgp,gpj->gtij', c_c, apow, b_bar, precision=HI))
    lag = jnp.arange(t)[None, :] - jnp.arange(t)[:, None]
    shift = (lag[None] == jnp.arange(t)[:, None, None]).astype(F32)
    toep = jnp.einsum('gtij,tsu->gsjui', kern, shift, precision=HI)
    toep = toep.reshape(g, t * S5_GROUP, t * S5_GROUP)
    e_c = (apow[::-1][:, :, None, :] * jnp.moveaxis(b_bar, 2, 1)[None]).transpose(1, 0, 2, 3)
    e_c = e_c.reshape(g, t * S5_GROUP, p)
    f_c = (c_c[:, None] * apow1.transpose(1, 0, 2)[:, :, None, :])
    f_c = jnp.moveaxis(f_c.reshape(g, t * S5_GROUP, p), 1, 2)
    eye2 = jnp.eye(2, dtype=F32)

    def pair_in(x):
        return jnp.einsum('narp,ab->narbp', x.reshape(g // 2, 2, t * S5_GROUP, p), eye2).reshape(
            g // 2, 2 * t * S5_GROUP, 2 * p)

    def pair_out(x):
        return jnp.einsum('napc,ab->napbc', x.reshape(g // 2, 2, p, t * S5_GROUP), eye2).reshape(
            g // 2, 2 * p, 2 * t * S5_GROUP)

    def planes(z):
        z = jnp.broadcast_to(z, (g, SUBLANE, p)).reshape(g // 2, 2, SUBLANE, p).transpose(0, 2, 1, 3)
        z = z.reshape(g // 2, SUBLANE, 2 * p)
        return [jnp.real(z), jnp.imag(z)]

    coef = []
    for k in (1, 2, 4):
        coef += planes(jnp.exp(lam_dt * float(t * k))[:, None, :])
    coef += planes(jnp.exp(lam_dt[:, None, :] * (float(t) * jnp.arange(1, SUBLANE + 1, dtype=F32))[None, :, None]))
    dtile = jnp.tile(d_skip.astype(F32).reshape(g, 1, S5_GROUP), (1, 1, t))
    return (toep.astype(BF16), pair_in(jnp.real(e_c)).astype(BF16), pair_in(jnp.imag(e_c)).astype(BF16),
            pair_out(jnp.real(f_c)).astype(BF16), pair_out(-jnp.imag(f_c)).astype(BF16),
            jnp.stack(coef, axis=1), dtile)


def _s5_toeplitz(u_src, col0, params, batch, seq):
    toep, ere, eim, fre, fim, coef, dtile = params
    g = toep.shape[0]
    t = S5_BLOCK
    tw = t * S5_GROUP
    ns = 2 * S5_STATE
    m = u_src.shape[0]
    nb = m // t
    gps = LANE // S5_GROUP
    npair = gps // 2
    cb0 = col0 // LANE
    est = 4 * m * LANE * 4 + 2 * gps * nb * tw * 4 + 4 * npair * nb * ns * 4 + 16 * gps * tw * tw * 2 + 12 * nb * tw * 4
    return pl.pallas_call(
        functools.partial(_s5t_kernel, gps=gps, nb=nb, nb_seq=seq // t),
        grid=(g // gps,),
        in_specs=[pl.BlockSpec((m, LANE), lambda i: (0, cb0 + i)),
                  pl.BlockSpec((gps, tw, tw), lambda i: (i, 0, 0)),
                  pl.BlockSpec((npair, 2 * tw, ns), lambda i: (i, 0, 0)),
                  pl.BlockSpec((npair, 2 * tw, ns), lambda i: (i, 0, 0)),
                  pl.BlockSpec((npair, ns, 2 * tw), lambda i: (i, 0, 0)),
                  pl.BlockSpec((npair, ns, 2 * tw), lambda i: (i, 0, 0)),
                  pl.BlockSpec((npair, 8, SUBLANE, ns), lambda i: (i, 0, 0, 0)),
                  pl.BlockSpec((gps, 1, tw), lambda i: (i, 0, 0))],
        out_specs=pl.BlockSpec((m, LANE), lambda i: (0, i)),
        out_shape=jax.ShapeDtypeStruct((m, g * S5_GROUP), F32),
        scratch_shapes=([pltpu.VMEM((gps, nb, tw), F32) for _ in range(2)]
                        + [pltpu.VMEM((npair, nb, ns), F32) for _ in range(4)]),
        compiler_params=_cparams(("parallel",), est),
        name="s5_blocks",
    )(u_src, toep, ere, eim, fre, fim, coef, dtile)


def _glu_kernel(a_ref, w_ref, g_ref, o_ref):
    z = jnp.dot(a_ref[...].astype(BF16), w_ref[...], preferred_element_type=F32)
    o_ref[...] = (g_ref[...] * jax.nn.sigmoid(z)).astype(o_ref.dtype)


def _s5_glu(g, w_glu):
    m, k = g.shape
    bm, bn, est = _mm_tiles(m, [k], k, [4], 2, True)
    return pl.pallas_call(
        _glu_kernel,
        grid=(m // bm, k // bn),
        in_specs=[pl.BlockSpec((bm, k), lambda i, j: (i, 0)),
                  pl.BlockSpec((k, bn), lambda i, j: (0, j)),
                  pl.BlockSpec((bm, bn), lambda i, j: (i, j))],
        out_specs=pl.BlockSpec((bm, bn), lambda i, j: (i, j)),
        out_shape=jax.ShapeDtypeStruct((m, k), BF16),
        compiler_params=_cparams(("parallel", "parallel"), est),
        name="s5_glu",
    )(g, w_glu, g)


def _shift_rows(h, prev, k):
    r = pltpu.roll(h, k, 0)
    p = pltpu.roll(prev, k, 0)
    row = lax.broadcasted_iota(jnp.int32, prev.shape, 0)
    top = jnp.where(row < k, p, r[:SUBLANE])
    return jnp.concatenate([top, r[SUBLANE:]], axis=0)


def _ffn_up_kernel(a_ref, wa_ref, wb_ref, cwa_ref, cwb_ref, o_ref, hbuf, carry, *, tiles_per_seq, nj):
    i, j = pl.program_id(0), pl.program_id(1)
    slot = i % 2
    rows = hbuf.shape[1]

    @pl.when(jnp.logical_and(i == 0, j == 0))
    def _():
        hbuf[...] = jnp.zeros_like(hbuf)
        carry[...] = jnp.zeros_like(carry)

    def epilogue():
        jp = jnp.maximum(j - 1, 0)
        keep = jnp.where((i % tiles_per_seq) == 0, 0.0, 1.0)
        outs = []
        for s, cw_ref in enumerate((cwa_ref, cwb_ref)):
            h = hbuf[s]
            prev = carry[1 - slot, jp, s * SUBLANE:(s + 1) * SUBLANE, :] * keep
            cw = cw_ref[...]
            outs.append(cw[2:3] * h + cw[1:2] * _shift_rows(h, prev, 1) + cw[0:1] * _shift_rows(h, prev, 2))
        o_ref[...] = (jax.nn.silu(outs[0]) * outs[1]).astype(o_ref.dtype)

    @pl.when(j < nj)
    def _():
        epilogue()
        a = a_ref[...]
        ha = jnp.dot(a, wa_ref[...].astype(BF16), preferred_element_type=F32)
        hb = jnp.dot(a, wb_ref[...].astype(BF16), preferred_element_type=F32)
        hbuf[0] = ha
        hbuf[1] = hb
        carry[slot, j, 0:SUBLANE, :] = ha[rows - SUBLANE:, :]
        carry[slot, j, SUBLANE:2 * SUBLANE, :] = hb[rows - SUBLANE:, :]

    @pl.when(j == nj)
    def _():
        epilogue()


def _ffn_up(xn, w_up, layer, conv_w, seq):
    m, k = xn.shape
    f = w_up.shape[2] // 2
    bm = _pick(seq, 1024, SUBLANE)
    bn = _pick(f, 256, LANE)
    nj = f // bn
    est = (2 * bm * k * 2 + 4 * k * bn * 4 + 2 * k * bn * 2 + 2 * bm * bn * 2 + 2 * nj * 16 * bn * 4
           + 2 * bm * bn * 4 + 12 * bm * bn * 4)
    cur = lambda j: jnp.minimum(j, nj - 1)
    prv = lambda j: jnp.maximum(j - 1, 0)
    return pl.pallas_call(
        functools.partial(_ffn_up_kernel, tiles_per_seq=seq // bm, nj=nj),
        grid=(m // bm, nj + 1),
        in_specs=[pl.BlockSpec((bm, k), lambda i, j: (i, 0)),
                  pl.BlockSpec((None, k, bn), lambda i, j: (layer, 0, cur(j))),
                  pl.BlockSpec((None, k, bn), lambda i, j: (layer, 0, cur(j) + nj)),
                  pl.BlockSpec((SUBLANE, bn), lambda i, j: (0, prv(j))),
                  pl.BlockSpec((SUBLANE, bn), lambda i, j: (0, prv(j) + nj))],
        out_specs=pl.BlockSpec((bm, bn), lambda i, j: (i, prv(j))),
        out_shape=jax.ShapeDtypeStruct((m, f), BF16),
        scratch_shapes=[pltpu.VMEM((2, bm, bn), F32), pltpu.VMEM((2, nj, 2 * SUBLANE, bn), F32)],
        compiler_params=_cparams(("arbitrary", "arbitrary"), est),
        name="ffn_up",
    )(xn, w_up, w_up, conv_w, conv_w)


def _gdn_pre_kernel(x_ref, w_ref, o_ref, *, nq, nk):
    j = pl.program_id(1)
    x = x_ref[...]
    w = w_ref[...]
    kw = 4
    row = lax.broadcasted_iota(jnp.int32, x.shape, 0)
    acc = w[kw - 1:kw] * x
    for s in range(1, kw):
        acc = acc + w[kw - 1 - s:kw - s] * jnp.where(row >= s, pltpu.roll(x, s, 0), 0.0)
    y = jax.nn.silu(acc)
    parts = []
    for c in range(x.shape[1] // GDN_DK):
        yh = y[:, c * GDN_DK:(c + 1) * GDN_DK]
        parts.append(yh * lax.rsqrt(jnp.sum(yh * yh, axis=-1, keepdims=True) + NORM_EPS))
    yn = parts[0] if len(parts) == 1 else jnp.concatenate(parts, axis=1)
    is_q = j < nq
    is_k = jnp.logical_and(j >= nq, j < nq + nk)
    o_ref[...] = jnp.where(is_q, yn * (GDN_DK ** -0.5), jnp.where(is_k, yn, y))


def _gdn_pre(src, conv_w, batch, seq, heads):
    width = conv_w.shape[1]
    cb = GDN_DK
    nq = heads * GDN_DK // cb
    est = 4 * seq * cb * 4 + 8 * seq * cb * 4
    return pl.pallas_call(
        functools.partial(_gdn_pre_kernel, nq=nq, nk=nq),
        grid=(batch, width // cb),
        in_specs=[pl.BlockSpec((seq, cb), lambda b, j: (b, j)),
                  pl.BlockSpec((SUBLANE, cb), lambda b, j: (0, j))],
        out_specs=pl.BlockSpec((seq, cb), lambda b, j: (b, j)),
        out_shape=jax.ShapeDtypeStruct((batch * seq, width), F32),
        compiler_params=_cparams(("parallel", "parallel"), est),
        name="gdn_pre",
    )(src, conv_w)


GDN_ROWS = 256
GDN_HEADS_PER_STEP = 2


def _dot_hi(a, b):
    return jnp.dot(a, b, precision=HI, preferred_element_type=F32)


def _bdot(a, b):
    return jnp.dot(a.astype(BF16), b.astype(BF16), preferred_element_type=F32)


def _bdot_nt(a, b):
    return lax.dot_general(a.astype(BF16), b.astype(BF16), (((1,), (1,)), ((), ())), preferred_element_type=F32)


def _bdot_tn(a, b):
    return lax.dot_general(a.astype(BF16), b.astype(BF16), (((0,), (0,)), ((), ())), preferred_element_type=F32)


def _gdn_head(alog, dtb, q, k, v, a_raw, b_raw, gate, ng, s, masks, rows):
    same, tril, tril_t, strict, eye, eye_f = masks
    a_raw = a_raw + dtb
    softplus = jnp.maximum(a_raw, 0.0) + jnp.log(1.0 + jnp.exp(-jnp.abs(a_raw)))
    g = -jnp.exp(jnp.zeros_like(a_raw) + alog) * softplus
    beta = jax.nn.sigmoid(b_raw)
    g_cols = jnp.broadcast_to(g, (rows, rows))
    g_rows = jnp.broadcast_to(jnp.sum(jnp.where(eye, g_cols, 0.0), axis=0, keepdims=True), (rows, rows))
    gc_col = jnp.sum(jnp.where(tril, g_rows, 0.0), axis=1, keepdims=True)
    gl_col = jnp.sum(jnp.where(same, g_rows, 0.0), axis=1, keepdims=True)
    gc_row = jnp.sum(jnp.where(tril_t, g_cols, 0.0), axis=0, keepdims=True)
    gc = jnp.broadcast_to(gc_col, (rows, LANE))
    gl = jnp.broadcast_to(gl_col, (rows, LANE))
    decay = jnp.exp(jnp.where(tril, gc_col - gc_row, MASK_VALUE))
    kb = k * beta
    vb = v * beta
    lmat = jnp.where(strict, _bdot_nt(kb, k) * decay, 0.0)
    tinv = eye_f - lmat
    pw = lmat
    for _ in range(int(math.log2(CHUNK)) - 1):
        pw = _bdot(pw, pw)
        tinv = tinv + _bdot(tinv, pw)
    eg = jnp.exp(gc)
    sol = _bdot(tinv, jnp.concatenate([vb, kb * eg], axis=1))
    u_c, w_c = sol[:, :GDN_DV], sol[:, GDN_DV:]
    intra = _bdot_nt(q, k) * decay
    q_dec = q * eg
    k_dec = k * jnp.exp(gl - gc)
    egl = jnp.exp(gl)
    iuw = _bdot(intra, sol)
    o_loc = iuw[:, :GDN_DV]
    q_eff = q_dec - iuw[:, GDN_DV:]
    outs = []
    for c in range(rows // CHUNK):
        lo, hi = c * CHUNK, (c + 1) * CHUNK
        kuw = _bdot_tn(k_dec[lo:hi], sol[lo:hi])
        outs.append(_bdot(q_eff[lo:hi], s) + o_loc[lo:hi])
        s = s * egl[lo:lo + 1, :] + kuw[:, :GDN_DV] - _bdot(kuw[:, GDN_DV:], s)
    o = jnp.concatenate(outs, axis=0)
    o = o * lax.rsqrt(jnp.mean(o * o, axis=-1, keepdims=True) + NORM_EPS) * ng
    return o * jax.nn.silu(gate), s


GDN_CONV = 4


def _gdn_kernel(alog_ref, dtb_ref, q_ref, k_ref, v_ref, cq_ref, ck_ref, cv_ref, ab_ref, gate_ref, ng_ref,
                o_ref, state, hist, *, rows, hps, heads):
    hb = pl.program_id(1)
    r = pl.program_id(2)
    ab = ab_ref[...]
    ab_lane = lax.broadcasted_iota(jnp.int32, ab.shape, 1)

    @pl.when(r == 0)
    def _():
        state[...] = jnp.zeros_like(state)

    ri = lax.broadcasted_iota(jnp.int32, (rows, rows), 0)
    ci = lax.broadcasted_iota(jnp.int32, (rows, rows), 1)
    same = (ri // CHUNK) == (ci // CHUNK)
    tril = jnp.logical_and(same, ci <= ri)
    tril_t = jnp.logical_and(same, ri <= ci)
    strict = jnp.logical_and(same, ci < ri)
    eye = ri == ci
    masks = (same, tril, tril_t, strict, eye, jnp.where(eye, 1.0, 0.0))
    ng = ng_ref[...]

    @pl.when(r == 0)
    def _():
        hist[...] = jnp.zeros_like(hist)

    qkv = []
    for idx, (x_ref, w_ref) in enumerate(((q_ref, cq_ref), (k_ref, ck_ref), (v_ref, cv_ref))):
        x = x_ref[...]
        prev = hist[idx]
        hist[idx] = x[rows - SUBLANE:, :]
        w = w_ref[...]
        kw = GDN_CONV
        acc = w[kw - 1:kw] * x
        for sft in range(1, kw):
            acc = acc + w[kw - 1 - sft:kw - sft] * _shift_rows(x, prev, sft)
        qkv.append(jax.nn.silu(acc))

    def l2n(y):
        return y * lax.rsqrt(jnp.sum(y * y, axis=-1, keepdims=True) + NORM_EPS)

    for i in range(hps):
        cs = slice(i * GDN_DK, (i + 1) * GDN_DK)
        h = hb * hps + i
        a_raw = jnp.sum(jnp.where(ab_lane == h, ab, 0.0), axis=1, keepdims=True)
        b_raw = jnp.sum(jnp.where(ab_lane == heads + h, ab, 0.0), axis=1, keepdims=True)
        o, s = _gdn_head(alog_ref[h], dtb_ref[h], l2n(qkv[0][:, cs]) * (GDN_DK ** -0.5), l2n(qkv[1][:, cs]),
                         qkv[2][:, cs], a_raw, b_raw,
                         gate_ref[:, cs], ng, state[i], masks, rows)
        state[i] = s
        o_ref[:, cs] = o.astype(o_ref.dtype)


def _gdn_core(qkv, conv_w, ab, gate_src, gate_col0, a_log, dt_bias, norm_g, batch, seq, heads):
    rows = _pick(seq, GDN_ROWS, CHUNK)
    nr = seq // rows
    hps = GDN_HEADS_PER_STEP if heads % GDN_HEADS_PER_STEP == 0 else 1
    nhb = heads // hps
    wblk = hps * GDN_DK
    gcb = gate_col0 // wblk
    est = hps * (24 * rows * LANE * 4 + 16 * rows * rows * 4 + 8 * rows * LANE * 4)
    tok = lambda b, h, r: b * nr + r
    return pl.pallas_call(
        functools.partial(_gdn_kernel, rows=rows, hps=hps, heads=heads),
        grid=(batch, nhb, nr),
        in_specs=[pl.BlockSpec(memory_space=pltpu.SMEM),
                  pl.BlockSpec(memory_space=pltpu.SMEM),
                  pl.BlockSpec((rows, wblk), lambda b, h, r: (tok(b, h, r), h)),
                  pl.BlockSpec((rows, wblk), lambda b, h, r: (tok(b, h, r), nhb + h)),
                  pl.BlockSpec((rows, wblk), lambda b, h, r: (tok(b, h, r), 2 * nhb + h)),
                  pl.BlockSpec((SUBLANE, wblk), lambda b, h, r: (0, h)),
                  pl.BlockSpec((SUBLANE, wblk), lambda b, h, r: (0, nhb + h)),
                  pl.BlockSpec((SUBLANE, wblk), lambda b, h, r: (0, 2 * nhb + h)),
                  pl.BlockSpec((rows, LANE), lambda b, h, r: (tok(b, h, r), 0)),
                  pl.BlockSpec((rows, wblk), lambda b, h, r: (tok(b, h, r), gcb + h)),
                  pl.BlockSpec((1, GDN_DV), lambda b, h, r: (0, 0))],
        out_specs=pl.BlockSpec((rows, wblk), lambda b, h, r: (tok(b, h, r), h)),
        out_shape=jax.ShapeDtypeStruct((batch * seq, heads * GDN_DV), BF16),
        scratch_shapes=[pltpu.VMEM((hps, GDN_DK, GDN_DV), F32), pltpu.VMEM((3, SUBLANE, wblk), F32)],
        compiler_params=_cparams(("parallel", "parallel", "arbitrary"), est),
        name="gdn_core",
    )(a_log.astype(F32), dt_bias.astype(F32), qkv, qkv, qkv, conv_w, conv_w, conv_w, ab, gate_src,
      norm_g.reshape(1, GDN_DV).astype(F32))


SB_TQ = 512
SB_TK = 256
SB_HEADS_PER_STEP = 2


def _sb_kernel(q_ref, kt_ref, v_ref, o_ref, acc_s, c_s, *, tq, tk, hps):
    i = pl.program_id(2)
    r = tq // tk
    ur = lax.broadcasted_iota(jnp.int32, (2 * tk, tk), 0) % tk
    uc = lax.broadcasted_iota(jnp.int32, (2 * tk, tk), 1)
    upper2 = jnp.where(ur > uc, 1.0, 0.0).astype(BF16)
    ri = lax.broadcasted_iota(jnp.int32, (tq, tk), 0)
    ci = lax.broadcasted_iota(jnp.int32, (tq, tk), 1)
    acc_s[...] = jnp.zeros_like(acc_s)
    c_s[...] = jnp.zeros_like(c_s)

    def cols(hh):
        return slice(hh * SB_D, (hh + 1) * SB_D)

    def logits(hh, j):
        z = jnp.dot(q_ref[:, cols(hh)], kt_ref[j, cols(hh), :], preferred_element_type=F32)
        return z, -(jnp.maximum(z, 0.0) + jnp.log(1.0 + jnp.exp2(-jnp.abs(z))) * LOG2E)

    def weights(z, lm, c):
        hi = lm.astype(BF16)
        lo = (lm - hi.astype(F32)).astype(BF16)
        rest = jnp.dot(jnp.concatenate([hi, lo], axis=1), upper2, preferred_element_type=F32)
        return jnp.exp2(z + lm + rest + c)

    def values(hh, j):
        return v_ref[pl.ds(pl.multiple_of(j * tk, tk), tk), cols(hh)]

    for d in range(r - 1, -1, -1):
        j = i * r + d
        before = (ci + d * tk) < ri
        for hh in range(hps):
            z, lm = logits(hh, j)
            lm = jnp.where(before, lm, 0.0)
            w = jnp.where(before, weights(z, lm, c_s[hh]), 0.0)
            acc_s[:, cols(hh)] += jnp.dot(w.astype(BF16), values(hh, j), preferred_element_type=F32)
            c_s[hh] += jnp.sum(lm, axis=1, keepdims=True)

    def single(j):
        for hh in range(hps):
            z, lm = logits(hh, j)
            w = weights(z, lm, c_s[hh])
            acc_s[:, cols(hh)] += jnp.dot(w.astype(BF16), values(hh, j), preferred_element_type=F32)
            c_s[hh] += jnp.sum(lm, axis=1, keepdims=True)

    def pair(p, carry):
        ja = i * r - 1 - 2 * p
        jb = ja - 1
        for hh in range(hps):
            za, lma = logits(hh, ja)
            zb, lmb = logits(hh, jb)
            c = c_s[hh]
            ta = jnp.sum(lma, axis=1, keepdims=True)
            wa = weights(za, lma, c)
            wb = weights(zb, lmb, c + ta)
            acc_s[:, cols(hh)] += (jnp.dot(wa.astype(BF16), values(hh, ja), preferred_element_type=F32)
                                   + jnp.dot(wb.astype(BF16), values(hh, jb), preferred_element_type=F32))
            c_s[hh] = c + ta + jnp.sum(lmb, axis=1, keepdims=True)
        return carry

    n_full = i * r
    lax.fori_loop(0, n_full // 2, pair, 0)
    if r % 2 == 1:
        @pl.when(n_full % 2 == 1)
        def _():
            single(0)
    o_ref[...] = acc_s[...].astype(o_ref.dtype)


def _sb_tiles(seq):
    tq = _pick(seq, SB_TQ, LANE)
    return tq, _pick(tq, SB_TK, LANE)


def _sb_attention(qv, kt, batch, seq, heads):
    tq, tk = _sb_tiles(seq)
    nq, nk = seq // tq, seq // tk
    hps = SB_HEADS_PER_STEP if heads % SB_HEADS_PER_STEP == 0 else 1
    nhb = heads // hps
    wblk = hps * SB_D
    est = hps * (4 * tq * SB_D * 2 + 4 * seq * SB_D * 2 + 2 * tq * SB_D * 4 + 24 * tq * tk * 4)
    return pl.pallas_call(
        functools.partial(_sb_kernel, tq=tq, tk=tk, hps=hps),
        grid=(batch, nhb, nq),
        in_specs=[pl.BlockSpec((tq, wblk), lambda b, h, i: (b * nq + i, h)),
                  pl.BlockSpec((nk, wblk, tk), lambda b, h, i: (b, h, 0)),
                  pl.BlockSpec((seq, wblk), lambda b, h, i: (b, nhb + h))],
        out_specs=pl.BlockSpec((tq, wblk), lambda b, h, i: (b * nq + i, h)),
        out_shape=jax.ShapeDtypeStruct((batch * seq, heads * SB_D), BF16),
        scratch_shapes=[pltpu.VMEM((tq, wblk), F32), pltpu.VMEM((hps, tq, 1), F32)],
        compiler_params=_cparams(("parallel", "parallel", "arbitrary"), est),
        name="sb_attention",
    )(qv, kt, qv)


def _pad_cols(w, n):
    return jnp.pad(w, ((0, 0), (0, n - w.shape[1])))


def _ffn(h, ln_g, w_up, layer, conv_w, w_down_bf16, seq):
    xn = _rmsnorm(h, ln_g, BF16)
    conv8 = jnp.pad(conv_w.astype(F32), ((0, SUBLANE - conv_w.shape[0]), (0, 0)))
    act = _ffn_up(xn, w_up, layer, conv8, seq)
    return _matmul([(act, w_down_bf16)], F32, residual=h, b_layer=layer)


def _even_layer(h, ln_g, w_in, w_out, q_norm, kv_norm, w_uq, w_ukv,
                a_re, a_im, log_step, b_re, b_im, c_re, c_im, d_skip, w_glu, batch, seq):
    d = h.shape[1]
    q_rank, kv_rank = q_norm.shape[0], kv_norm.shape[0]
    heads = w_uq.shape[1] // (MLA_NOPE + MLA_ROPE)
    s5_width = d_skip.shape[0]
    o_kr = q_rank + kv_rank
    o_s5 = o_kr + MLA_ROPE
    latent_w = -(-(o_kr + LANE) // s5_width) * s5_width if s5_width >= LANE else o_kr + LANE
    w_lat = _pad_cols(w_in[:, :o_s5], latent_w)
    w_cat = jnp.concatenate([w_lat, w_in[:, o_s5:]], axis=1).astype(BF16)
    hn = _rmsnorm(h, ln_g, BF16)
    proj = _matmul([(hn, w_cat)], F32)
    cq = _rmsnorm(proj, q_norm, BF16, col0=0, width=q_rank)
    ckv = _rmsnorm(proj, kv_norm, BF16, col0=q_rank, width=kv_rank)
    q_scale = (MLA_NOPE + MLA_ROPE) ** -0.5 * LOG2E
    w_uq_p = jnp.pad(w_uq.reshape(q_rank, heads, MLA_NOPE + MLA_ROPE) * q_scale,
                     ((0, 0), (0, 0), (0, MLA_QK - MLA_NOPE - MLA_ROPE))).reshape(q_rank, heads * MLA_QK)
    q = _matmul([(cq, w_uq_p.astype(BF16))], F32)
    tq = _mla_tile(seq)
    w_kv = w_ukv.reshape(kv_rank, heads, MLA_NOPE + MLA_V)
    w_kn = w_kv[:, :, :MLA_NOPE].reshape(kv_rank, heads * MLA_NOPE).astype(BF16)
    w_v = w_kv[:, :, MLA_NOPE:].reshape(kv_rank, heads * MLA_V).astype(BF16)
    knt = _matmul_keys_t(ckv, w_kn, tq)
    v = _matmul([(ckv, w_v)], BF16)
    cosf, sinf = _rope_tables(seq)
    krt = _rope_k(proj, o_kr, cosf, sinf, seq, tq)
    out_a = _mla_attention(q, knt, krt, v, cosf, sinf, batch, seq, heads)
    params = _s5t_params(a_re, a_im, log_step, b_re, b_im, c_re, c_im, d_skip)
    g = _s5_toeplitz(proj, latent_w, params, batch, seq)
    out_b = _s5_glu(g, w_glu.astype(BF16))
    na = heads * MLA_V
    w_out_all, w_out_layer = w_out
    return _matmul([(out_a, w_out_all), (out_b, w_out_all)], F32, residual=h,
                   b_layer=w_out_layer, b_row0=[0, na])


def _odd_layer(h, ln_g, w_in, w_out, conv_w, a_log, dt_bias, norm_g, batch, seq):
    heads = a_log.shape[0]
    qkv_w = heads * (2 * GDN_DK + GDN_DV)
    o2 = qkv_w + 2 * heads
    o3 = o2 + heads * GDN_DV
    hn = _rmsnorm(h, ln_g, BF16)
    w_t = jnp.swapaxes(w_in, 0, 1)
    w_main = jnp.concatenate([w_t[:qkv_w], w_t[o2:o3]], axis=0).astype(BF16)
    w_ab = jnp.pad(w_t[qkv_w:o2], ((0, LANE - (o2 - qkv_w)), (0, 0))).astype(BF16)
    sb_heads = (w_in.shape[1] - o3) // (3 * SB_D)
    nsq = sb_heads * SB_D
    w_sb_qv = jnp.concatenate([w_t[o3:o3 + nsq] * (SB_D ** -0.5 * LOG2E), w_t[o3 + 2 * nsq:]], axis=0).astype(BF16)
    w_sb_k = w_t[o3 + nsq:o3 + 2 * nsq].astype(BF16)
    main = _matmul([(hn, w_main)], F32, b_nk=True)
    ab = _matmul([(hn, w_ab)], F32, b_nk=True)
    sb_qv = _matmul([(hn, w_sb_qv)], BF16, b_nk=True)
    sb_kt = _matmul_keys_t(hn, w_sb_k, _sb_tiles(seq)[1], b_nk=True)
    conv8 = jnp.pad(conv_w.astype(F32), ((0, SUBLANE - conv_w.shape[0]), (0, 0)))
    out_c = _gdn_core(main, conv8, ab, main, qkv_w, a_log, dt_bias, norm_g, batch, seq, heads)
    out_d = _sb_attention(sb_qv, sb_kt, batch, seq, sb_heads)
    nc = heads * GDN_DV
    w_out_all, w_out_layer = w_out
    return _matmul([(out_c, w_out_all), (out_d, w_out_all)], F32, residual=h,
                   b_layer=w_out_layer, b_row0=[0, nc])


def kernel(x, ln_mix, ln_ffn, ln_final, ffn_w_up, ffn_conv, ffn_w_down, ev_w_in, ev_w_out, mla_q_norm,
           mla_kv_norm, mla_w_uq, mla_w_ukv, s5_a_re, s5_a_im, s5_log_step, s5_b_re, s5_b_im, s5_c_re,
           s5_c_im, s5_d, s5_w_glu, od_w_in, od_w_out, gdn_conv, gdn_a_log, gdn_dt_bias, gdn_norm):
    batch, seq, d = x.shape
    h = x.reshape(batch * seq, d).astype(F32)
    ffn_w_up = ffn_w_up.astype(F32)
    w_down_bf16 = ffn_w_down.astype(BF16)
    for layer in range(ln_mix.shape[0]):
        i = layer // 2
        if layer % 2 == 0:
            h = _even_layer(h, ln_mix[layer], ev_w_in[i], (ev_w_out.astype(F32), i), mla_q_norm[i], mla_kv_norm[i],
                            mla_w_uq[i], mla_w_ukv[i], s5_a_re[i], s5_a_im[i], s5_log_step[i], s5_b_re[i],
                            s5_b_im[i], s5_c_re[i], s5_c_im[i], s5_d[i], s5_w_glu[i], batch, seq)
        else:
            h = _odd_layer(h, ln_mix[layer], od_w_in[i], (od_w_out.astype(F32), i), gdn_conv[i], gdn_a_log[i],
                           gdn_dt_bias[i], gdn_norm[i], batch, seq)
        h = _ffn(h, ln_ffn[layer], ffn_w_up, layer, ffn_conv[layer], w_down_bf16, seq)
    return _rmsnorm(h, ln_final, x.dtype).reshape(batch, seq, d)
```

```python
import functools
import math

import jax
import jax.numpy as jnp
from jax import lax
from jax.experimental import pallas as pl
from jax.experimental.pallas import tpu as pltpu

F32 = jnp.float32
BF16 = jnp.bfloat16
HI = lax.Precision.HIGHEST

CHUNK = 64
NORM_EPS = 1e-6
MLA_NOPE = 128
MLA_ROPE = 64
MLA_V = 128
MLA_QK = 256
ROPE_BASE = 10000.0
S5_GROUP = 16
S5_STATE = 64
GDN_DK = 128
GDN_DV = 128
SB_D = 128

LANE = 128
SUBLANE = 8
VMEM_LIMIT_MAX = 58 * 1024 * 1024
VMEM_LIMIT_MIN = 32 * 1024 * 1024
MASK_VALUE = -1e30
LOG2E = math.log2(math.e)


def _cparams(sem, vmem_est):
    limit = int(min(max(vmem_est * 5 // 4 + (4 << 20), VMEM_LIMIT_MIN), VMEM_LIMIT_MAX))
    return pltpu.CompilerParams(dimension_semantics=sem, vmem_limit_bytes=limit)


def _pick(n, pref, mult):
    if n <= pref:
        return n
    t = (pref // mult) * mult
    while t >= mult:
        if n % t == 0:
            return t
        t -= mult
    return n


def _rmsnorm_kernel(x_ref, g_ref, o_ref):
    x = x_ref[...].astype(F32)
    ms = jnp.mean(x * x, axis=-1, keepdims=True)
    o_ref[...] = (x * lax.rsqrt(ms + NORM_EPS) * g_ref[...]).astype(o_ref.dtype)


def _rmsnorm(x, g, out_dtype, *, col0=0, width=None):
    m = x.shape[0]
    width = x.shape[1] if width is None else width
    tm = _pick(m, 256, SUBLANE)
    cb = col0 // width
    est = 2 * tm * width * (x.dtype.itemsize + jnp.dtype(out_dtype).itemsize) + 4 * tm * width * 4
    return pl.pallas_call(
        _rmsnorm_kernel,
        grid=(m // tm,),
        in_specs=[pl.BlockSpec((tm, width), lambda i: (i, cb)),
                  pl.BlockSpec((1, width), lambda i: (0, 0))],
        out_specs=pl.BlockSpec((tm, width), lambda i: (i, 0)),
        out_shape=jax.ShapeDtypeStruct((m, width), out_dtype),
        compiler_params=_cparams(("parallel",), est),
        name="rmsnorm",
    )(x, g.reshape(1, width).astype(F32))


def _mm_kernel(*refs, n_pairs, has_res, b_nk):
    o_ref = refs[-1]
    acc = None
    for p in range(n_pairs):
        a = refs[2 * p][...].astype(BF16)
        b = refs[2 * p + 1][...].astype(BF16)
        if b_nk:
            d = lax.dot_general(a, b, (((1,), (1,)), ((), ())), preferred_element_type=F32)
        else:
            d = jnp.dot(a, b, preferred_element_type=F32)
        acc = d if acc is None else acc + d
    if has_res:
        acc = acc + refs[2 * n_pairs][...]
    o_ref[...] = acc.astype(o_ref.dtype)


def _mm_tiles(m, ks, n, a_bytes, out_bytes, has_res, budget=52 << 20, b_bytes=2):
    for bm_p, bn_p in ((1024, 1024), (1024, 512), (512, 512), (512, 256), (256, 256), (256, 128), (128, 128)):
        bm = _pick(m, bm_p, SUBLANE)
        bn = _pick(n, bn_p, LANE)
        est = bm * bn * 4
        for k, ab in zip(ks, a_bytes):
            est += 2 * (bm * k * ab + k * bn * b_bytes) + (k * bn * 2 if b_bytes > 2 else 0)
        est += 2 * bm * bn * (out_bytes + (4 if has_res else 0))
        if est <= budget:
            return bm, bn, est
    return bm, bn, est


def _matmul(pairs, out_dtype, residual=None, *, a_col0=None, b_layer=None, b_row0=None, b_nk=False):
    m = pairs[0][0].shape[0]
    n = pairs[0][1].shape[0] if b_nk else pairs[0][1].shape[-1]
    if b_row0 is not None:
        ks = [a.shape[1] for a, _ in pairs]
    else:
        ks = [b.shape[1] if b_nk else b.shape[-2] for _, b in pairs]
        b_row0 = [0] * len(pairs)
    a_col0 = [0] * len(pairs) if a_col0 is None else a_col0
    a_bytes = [a.dtype.itemsize for a, _ in pairs]
    bm, bn, est = _mm_tiles(m, ks, n, a_bytes, jnp.dtype(out_dtype).itemsize, residual is not None,
                            b_bytes=pairs[0][1].dtype.itemsize)
    in_specs, args = [], []
    for (a, b), k, c0, r0 in zip(pairs, ks, a_col0, b_row0):
        cb = c0 // k
        rb = r0 // k
        in_specs.append(pl.BlockSpec((bm, k), lambda i, j, cb=cb: (i, cb)))
        if b_nk:
            in_specs.append(pl.BlockSpec((bn, k), lambda i, j: (j, 0)))
        elif b_layer is None:
            in_specs.append(pl.BlockSpec((k, bn), lambda i, j: (0, j)))
        else:
            in_specs.append(pl.BlockSpec((None, k, bn), lambda i, j, rb=rb: (b_layer, rb, j)))
        args += [a, b]
    if residual is not None:
        in_specs.append(pl.BlockSpec((bm, bn), lambda i, j: (i, j)))
        args.append(residual)
    return pl.pallas_call(
        functools.partial(_mm_kernel, n_pairs=len(pairs), has_res=residual is not None, b_nk=b_nk),
        grid=(m // bm, n // bn),
        in_specs=in_specs,
        out_specs=pl.BlockSpec((bm, bn), lambda i, j: (i, j)),
        out_shape=jax.ShapeDtypeStruct((m, n), out_dtype),
        compiler_params=_cparams(("parallel", "parallel"), est),
        name="matmul",
    )(*args)


def _mm_t_kernel(a_ref, b_ref, o_ref, acc_ref, *, tk, b_nk):
    a = a_ref[...].astype(BF16)
    if b_nk:
        acc_ref[...] = lax.dot_general(a, b_ref[...], (((1,), (1,)), ((), ())), preferred_element_type=F32)
    else:
        acc_ref[...] = jnp.dot(a, b_ref[...], preferred_element_type=F32)
    res_t = jnp.transpose(acc_ref[...])
    for c in range(o_ref.shape[0]):
        o_ref[c] = res_t[:, c * tk:(c + 1) * tk].astype(o_ref.dtype)


def _matmul_keys_t(a, b, tk, *, b_nk=False):
    m, k = a.shape
    n = b.shape[0] if b_nk else b.shape[1]
    bm = _pick(m, 1024, tk)
    bn = _pick(n, 512, LANE)
    est = 2 * (bm * k * a.dtype.itemsize + k * bn * 2 + bm * bn * 2) + 3 * bm * bn * 4
    return pl.pallas_call(
        functools.partial(_mm_t_kernel, tk=tk, b_nk=b_nk),
        grid=(m // bm, n // bn),
        in_specs=[pl.BlockSpec((bm, k), lambda i, j: (i, 0)),
                  pl.BlockSpec((bn, k), lambda i, j: (j, 0)) if b_nk else
                  pl.BlockSpec((k, bn), lambda i, j: (0, j))],
        out_specs=pl.BlockSpec((bm // tk, bn, tk), lambda i, j: (i, j, 0)),
        out_shape=jax.ShapeDtypeStruct((m // tk, n, tk), BF16),
        scratch_shapes=[pltpu.VMEM((bm, bn), F32)],
        compiler_params=_cparams(("parallel", "parallel"), est),
        name="matmul_keys_t",
    )(a, b)


def _rope_tables(seq):
    half = MLA_ROPE // 2
    inv_freq = ROPE_BASE ** (-jnp.arange(half, dtype=F32) / half)
    ang = jnp.arange(seq, dtype=F32)[:, None] * inv_freq[None, :]
    cos, sin = jnp.cos(ang), jnp.sin(ang)
    zeros = jnp.zeros((seq, LANE - MLA_ROPE), F32)
    return (jnp.concatenate([cos, cos, zeros], axis=1),
            jnp.concatenate([-sin, sin, zeros], axis=1))


def _rope_lanes(x, cosf, sinf):
    half = MLA_ROPE // 2
    lane = lax.broadcasted_iota(jnp.int32, x.shape, 1)
    swapped = jnp.where(lane < half, pltpu.roll(x, LANE - half, 1), pltpu.roll(x, half, 1))
    return x * cosf + swapped * sinf


def _rope_kernel(x_ref, cos_ref, sin_ref, o_ref):
    o_ref[0] = jnp.transpose(_rope_lanes(x_ref[...], cos_ref[...], sin_ref[...])).astype(o_ref.dtype)


def _rope_k(x, col0, cosf, sinf, seq, tq):
    m = x.shape[0]
    per_seq = seq // tq
    cb = col0 // LANE
    return pl.pallas_call(
        _rope_kernel,
        grid=(m // tq,),
        in_specs=[pl.BlockSpec((tq, LANE), lambda i: (i, cb)),
                  pl.BlockSpec((tq, LANE), lambda i: (i % per_seq, 0)),
                  pl.BlockSpec((tq, LANE), lambda i: (i % per_seq, 0))],
        out_specs=pl.BlockSpec((1, LANE, tq), lambda i: (i, 0, 0)),
        out_shape=jax.ShapeDtypeStruct((m // tq, LANE, tq), BF16),
        compiler_params=_cparams(("parallel",), 16 * tq * LANE * 4),
        name="rope_k",
    )(x, cosf, sinf)


def _mla_kernel(q_ref, knt_ref, krt_ref, v_ref, cos_ref, sin_ref, o_ref, qs, m_s, l_s, acc_s, *, tq, hps):
    i = pl.program_id(2)
    cos, sin = cos_ref[...], sin_ref[...]
    for hh in range(hps):
        q = q_ref[:, hh * MLA_QK:(hh + 1) * MLA_QK]
        qs[hh, :, :MLA_NOPE] = q[:, :MLA_NOPE].astype(BF16)
        qs[hh, :, MLA_NOPE:] = _rope_lanes(q[:, MLA_NOPE:], cos, sin).astype(BF16)

    def vcols(hh):
        return slice(hh * MLA_V, (hh + 1) * MLA_V)

    def tile(hh, j):
        kt = jnp.concatenate([knt_ref[j, hh * MLA_NOPE:(hh + 1) * MLA_NOPE, :], krt_ref[j]], axis=0)
        s = jnp.dot(qs[hh], kt, preferred_element_type=F32)
        return s, v_ref[pl.ds(pl.multiple_of(j * tq, tq), tq), vcols(hh)]

    rc = lax.broadcasted_iota(jnp.int32, (tq, tq), 0) // CHUNK
    cc = lax.broadcasted_iota(jnp.int32, (tq, tq), 1) // CHUNK
    for hh in range(hps):
        s, v = tile(hh, i)
        s = jnp.where(cc <= rc, s, MASK_VALUE)
        m = jnp.max(s, axis=1, keepdims=True)
        p = jnp.exp2(s - m)
        m_s[hh] = m
        l_s[hh] = jnp.sum(p, axis=1, keepdims=True)
        acc_s[:, vcols(hh)] = jnp.dot(p.astype(BF16), v, preferred_element_type=F32)

    def body(j, carry):
        for hh in range(hps):
            s, v = tile(hh, j)
            m_prev = m_s[hh]
            m_new = jnp.maximum(m_prev, jnp.max(s, axis=1, keepdims=True))
            alpha = jnp.exp2(m_prev - m_new)
            p = jnp.exp2(s - m_new)
            l_s[hh] = alpha * l_s[hh] + jnp.sum(p, axis=1, keepdims=True)
            acc_s[:, vcols(hh)] = (alpha * acc_s[:, vcols(hh)]
                                   + jnp.dot(p.astype(BF16), v, preferred_element_type=F32))
            m_s[hh] = m_new
        return carry

    lax.fori_loop(0, i, body, 0)
    for hh in range(hps):
        o_ref[:, vcols(hh)] = (acc_s[:, vcols(hh)] / l_s[hh]).astype(o_ref.dtype)


MLA_TILE = 1024
MLA_HEADS_PER_STEP = 2


def _mla_tile(seq):
    return _pick(seq, MLA_TILE, CHUNK)


def _mla_attention(q, knt, krt, v, cosf, sinf, batch, seq, heads):
    tq = _mla_tile(seq)
    nq = seq // tq
    hps = MLA_HEADS_PER_STEP if heads % MLA_HEADS_PER_STEP == 0 else 1
    est = hps * (2 * tq * MLA_QK * 4 + 2 * seq * (MLA_QK + LANE) * 2 + 4 * tq * LANE * 4 + 2 * tq * LANE * 2
                 + tq * MLA_QK * 2 + 3 * tq * LANE * 4 + 3 * tq * tq * 4)
    return pl.pallas_call(
        functools.partial(_mla_kernel, tq=tq, hps=hps),
        grid=(batch, heads // hps, nq),
        in_specs=[pl.BlockSpec((tq, hps * MLA_QK), lambda b, h, i: (b * nq + i, h)),
                  pl.BlockSpec((nq, hps * MLA_NOPE, tq), lambda b, h, i: (b, h, 0)),
                  pl.BlockSpec((nq, LANE, tq), lambda b, h, i: (b, 0, 0)),
                  pl.BlockSpec((seq, hps * MLA_V), lambda b, h, i: (b, h)),
                  pl.BlockSpec((tq, LANE), lambda b, h, i: (i, 0)),
                  pl.BlockSpec((tq, LANE), lambda b, h, i: (i, 0))],
        out_specs=pl.BlockSpec((tq, hps * MLA_V), lambda b, h, i: (b * nq + i, h)),
        out_shape=jax.ShapeDtypeStruct((batch * seq, heads * MLA_V), BF16),
        scratch_shapes=[pltpu.VMEM((hps, tq, MLA_QK), BF16), pltpu.VMEM((hps, tq, 1), F32),
                        pltpu.VMEM((hps, tq, 1), F32), pltpu.VMEM((tq, hps * MLA_V), F32)],
        compiler_params=_cparams(("parallel", "parallel", "arbitrary"), est),
        name="mla_attention",
    )(q, knt, krt, v, cosf, sinf)


S5_BLOCK = 16


def _s5t_kernel(x_ref, toep_ref, ere_ref, eim_ref, fre_ref, fim_ref, coef_ref, d_ref, o_ref,
                u_ref, y_ref, er_s, ei_s, hr_s, hi_s, *, gps, nb, nb_seq):
    t_blk = S5_BLOCK
    tw = t_blk * S5_GROUP
    npair = gps // 2
    per_slab = LANE // S5_GROUP
    steps = [x_ref[pl.ds(t, nb, stride=t_blk), :] for t in range(t_blk)]
    win = lax.broadcasted_iota(jnp.int32, (nb, LANE), 1) // S5_GROUP
    in_win = [win == w for w in range(per_slab)]

    def place(pieces):
        out = None
        for w, (arr, src) in enumerate(pieces):
            sh = ((w - src) % per_slab) * S5_GROUP
            r = arr if sh == 0 else pltpu.roll(arr, sh, 1)
            out = r if out is None else jnp.where(in_win[w], r, out)
        return out

    for g in range(gps):
        for hf in range(tw // LANE):
            u_ref[g, :, hf * LANE:(hf + 1) * LANE] = place([(steps[hf * per_slab + tt], g) for tt in range(per_slab)])
    for p in range(npair):
        up = jnp.concatenate([u_ref[2 * p].astype(BF16), u_ref[2 * p + 1].astype(BF16)], axis=1)
        er_s[p] = jnp.dot(up, ere_ref[p], preferred_element_type=F32)
        ei_s[p] = jnp.dot(up, eim_ref[p], preferred_element_type=F32)
    row = lax.broadcasted_iota(jnp.int32, (SUBLANE, LANE), 0)
    groups_per_seq = nb_seq // SUBLANE

    def body(r, carry):
        off = pl.multiple_of(r * SUBLANE, SUBLANE)
        keep = jnp.where((r % groups_per_seq) == 0, 0.0, 1.0)
        new = []
        for p in range(npair):
            xr = er_s[p, pl.ds(off, SUBLANE), :]
            xi = ei_s[p, pl.ds(off, SUBLANE), :]
            for idx, k in enumerate((1, 2, 4)):
                ar, ai = coef_ref[p, 2 * idx], coef_ref[p, 2 * idx + 1]
                sr = jnp.where(row >= k, pltpu.roll(xr, k, 0), 0.0)
                si = jnp.where(row >= k, pltpu.roll(xi, k, 0), 0.0)
                xr, xi = xr + ar * sr - ai * si, xi + ar * si + ai * sr
            cr = jnp.broadcast_to(carry[p][0] * keep, (SUBLANE, LANE))
            ci = jnp.broadcast_to(carry[p][1] * keep, (SUBLANE, LANE))
            pwr, pwi = coef_ref[p, 6], coef_ref[p, 7]
            xr, xi = xr + pwr * cr - pwi * ci, xi + pwr * ci + pwi * cr
            hr_s[p, pl.ds(off, SUBLANE), :] = jnp.where(row == 0, cr, pltpu.roll(xr, 1, 0))
            hi_s[p, pl.ds(off, SUBLANE), :] = jnp.where(row == 0, ci, pltpu.roll(xi, 1, 0))
            new.append((xr[SUBLANE - 1:, :], xi[SUBLANE - 1:, :]))
        return tuple(new)

    zero = jnp.zeros((1, LANE), F32)
    lax.fori_loop(0, nb // SUBLANE, body, tuple((zero, zero) for _ in range(npair)))
    for p in range(npair):
        yc = (jnp.dot(hr_s[p].astype(BF16), fre_ref[p], preferred_element_type=F32)
              + jnp.dot(hi_s[p].astype(BF16), fim_ref[p], preferred_element_type=F32))
        for a in range(2):
            g = 2 * p + a
            u = u_ref[g]
            y = jnp.dot(u.astype(BF16), toep_ref[g], preferred_element_type=F32) + yc[:, a * tw:(a + 1) * tw]
            y_ref[g] = jax.nn.gelu(y + d_ref[g] * u, approximate=True)
    for t in range(t_blk):
        hf, tt = divmod(t, per_slab)
        o_ref[pl.ds(t, nb, stride=t_blk), :] = place(
            [(y_ref[g, :, hf * LANE:(hf + 1) * LANE], tt) for g in range(gps)])


def _s5t_params(a_re, a_im, log_step, b_re, b_im, c_re, c_im, d_skip):
    g, p = a_re.shape
    t = S5_BLOCK
    lam = lax.complex(a_re.astype(F32), a_im.astype(F32))
    lam_dt = lam * jnp.exp(log_step.astype(F32))[:, None]
    lam_bar = jnp.exp(lam_dt)
    b_bar = ((lam_bar - 1.0) / lam)[..., None] * lax.complex(b_re.astype(F32), b_im.astype(F32))
    c_c = lax.complex(c_re.astype(F32), c_im.astype(F32))
    tau = jnp.arange(t, dtype=F32)
    apow = jnp.exp(lam_dt[None] * tau[:, None, None])
    apow1 = jnp.exp(lam_dt[None] * (tau + 1.0)[:, None, None])
    kern = jnp.real(jnp.einsum('gip,tgp,gpj->gtij', c_c, apow, b_bar, precision=HI))
    lag = jnp.arange(t)[None, :] - jnp.arange(t)[:, None]
    shift = (lag[None] == jnp.arange(t)[:, None, None]).astype(F32)
    toep = jnp.einsum('gtij,tsu->gsjui', kern, shift, precision=HI)
    toep = toep.reshape(g, t * S5_GROUP, t * S5_GROUP)
    e_c = (apow[::-1][:, :, None, :] * jnp.moveaxis(b_bar, 2, 1)[None]).transpose(1, 0, 2, 3)
    e_c = e_c.reshape(g, t * S5_GROUP, p)
    f_c = (c_c[:, None] * apow1.transpose(1, 0, 2)[:, :, None, :])
    f_c = jnp.moveaxis(f_c.reshape(g, t * S5_GROUP, p), 1, 2)
    eye2 = jnp.eye(2, dtype=F32)

    def pair_in(x):
        return jnp.einsum('narp,ab->narbp', x.reshape(g // 2, 2, t * S5_GROUP, p), eye2).reshape(
            g // 2, 2 * t * S5_GROUP, 2 * p)

    def pair_out(x):
        return jnp.einsum('napc,ab->napbc', x.reshape(g // 2, 2, p, t * S5_GROUP), eye2).reshape(
            g // 2, 2 * p, 2 * t * S5_GROUP)

    def planes(z):
        z = jnp.broadcast_to(z, (g, SUBLANE, p)).reshape(g // 2, 2, SUBLANE, p).transpose(0, 2, 1, 3)
        z = z.reshape(g // 2, SUBLANE, 2 * p)
        return [jnp.real(z), jnp.imag(z)]

    coef = []
    for k in (1, 2, 4):
        coef += planes(jnp.exp(lam_dt * float(t * k))[:, None, :])
    coef += planes(jnp.exp(lam_dt[:, None, :] * (float(t) * jnp.arange(1, SUBLANE + 1, dtype=F32))[None, :, None]))
    dtile = jnp.tile(d_skip.astype(F32).reshape(g, 1, S5_GROUP), (1, 1, t))
    return (toep.astype(BF16), pair_in(jnp.real(e_c)).astype(BF16), pair_in(jnp.imag(e_c)).astype(BF16),
            pair_out(jnp.real(f_c)).astype(BF16), pair_out(-jnp.imag(f_c)).astype(BF16),
            jnp.stack(coef, axis=1), dtile)


def _s5_toeplitz(u_src, col0, params, batch, seq):
    toep, ere, eim, fre, fim, coef, dtile = params
    g = toep.shape[0]
    t = S5_BLOCK
    tw = t * S5_GROUP
    ns = 2 * S5_STATE
    m = u_src.shape[0]
    nb = m // t
    gps = LANE // S5_GROUP
    npair = gps // 2
    cb0 = col0 // LANE
    est = 4 * m * LANE * 4 + 2 * gps * nb * tw * 4 + 4 * npair * nb * ns * 4 + 16 * gps * tw * tw * 2 + 12 * nb * tw * 4
    return pl.pallas_call(
        functools.partial(_s5t_kernel, gps=gps, nb=nb, nb_seq=seq // t),
        grid=(g // gps,),
        in_specs=[pl.BlockSpec((m, LANE), lambda i: (0, cb0 + i)),
                  pl.BlockSpec((gps, tw, tw), lambda i: (i, 0, 0)),
                  pl.BlockSpec((npair, 2 * tw, ns), lambda i: (i, 0, 0)),
                  pl.BlockSpec((npair, 2 * tw, ns), lambda i: (i, 0, 0)),
                  pl.BlockSpec((npair, ns, 2 * tw), lambda i: (i, 0, 0)),
                  pl.BlockSpec((npair, ns, 2 * tw), lambda i: (i, 0, 0)),
                  pl.BlockSpec((npair, 8, SUBLANE, ns), lambda i: (i, 0, 0, 0)),
                  pl.BlockSpec((gps, 1, tw), lambda i: (i, 0, 0))],
        out_specs=pl.BlockSpec((m, LANE), lambda i: (0, i)),
        out_shape=jax.ShapeDtypeStruct((m, g * S5_GROUP), F32),
        scratch_shapes=([pltpu.VMEM((gps, nb, tw), F32) for _ in range(2)]
                        + [pltpu.VMEM((npair, nb, ns), F32) for _ in range(4)]),
        compiler_params=_cparams(("parallel",), est),
        name="s5_blocks",
    )(u_src, toep, ere, eim, fre, fim, coef, dtile)


def _glu_kernel(a_ref, w_ref, g_ref, o_ref):
    z = jnp.dot(a_ref[...].astype(BF16), w_ref[...], preferred_element_type=F32)
    o_ref[...] = (g_ref[...] * jax.nn.sigmoid(z)).astype(o_ref.dtype)


def _s5_glu(g, w_glu):
    m, k = g.shape
    bm, bn, est = _mm_tiles(m, [k], k, [4], 2, True)
    return pl.pallas_call(
        _glu_kernel,
        grid=(m // bm, k // bn),
        in_specs=[pl.BlockSpec((bm, k), lambda i, j: (i, 0)),
                  pl.BlockSpec((k, bn), lambda i, j: (0, j)),
                  pl.BlockSpec((bm, bn), lambda i, j: (i, j))],
        out_specs=pl.BlockSpec((bm, bn), lambda i, j: (i, j)),
        out_shape=jax.ShapeDtypeStruct((m, k), BF16),
        compiler_params=_cparams(("parallel", "parallel"), est),
        name="s5_glu",
    )(g, w_glu, g)


def _shift_rows(h, prev, k):
    r = pltpu.roll(h, k, 0)
    p = pltpu.roll(prev, k, 0)
    row = lax.broadcasted_iota(jnp.int32, prev.shape, 0)
    top = jnp.where(row < k, p, r[:SUBLANE])
    return jnp.concatenate([top, r[SUBLANE:]], axis=0)


def _ffn_up_kernel(a_ref, wa_ref, wb_ref, cwa_ref, cwb_ref, o_ref, hbuf, carry, *, tiles_per_seq, nj):
    i, j = pl.program_id(0), pl.program_id(1)
    slot = i % 2
    rows = hbuf.shape[1]

    @pl.when(jnp.logical_and(i == 0, j == 0))
    def _():
        hbuf[...] = jnp.zeros_like(hbuf)
        carry[...] = jnp.zeros_like(carry)

    def epilogue():
        jp = jnp.maximum(j - 1, 0)
        keep = jnp.where((i % tiles_per_seq) == 0, 0.0, 1.0)
        outs = []
        for s, cw_ref in enumerate((cwa_ref, cwb_ref)):
            h = hbuf[s]
            prev = carry[1 - slot, jp, s * SUBLANE:(s + 1) * SUBLANE, :] * keep
            cw = cw_ref[...]
            outs.append(cw[2:3] * h + cw[1:2] * _shift_rows(h, prev, 1) + cw[0:1] * _shift_rows(h, prev, 2))
        o_ref[...] = (jax.nn.silu(outs[0]) * outs[1]).astype(o_ref.dtype)

    @pl.when(j < nj)
    def _():
        epilogue()
        a = a_ref[...]
        ha = jnp.dot(a, wa_ref[...].astype(BF16), preferred_element_type=F32)
        hb = jnp.dot(a, wb_ref[...].astype(BF16), preferred_element_type=F32)
        hbuf[0] = ha
        hbuf[1] = hb
        carry[slot, j, 0:SUBLANE, :] = ha[rows - SUBLANE:, :]
        carry[slot, j, SUBLANE:2 * SUBLANE, :] = hb[rows - SUBLANE:, :]

    @pl.when(j == nj)
    def _():
        epilogue()


def _ffn_up(xn, w_up, layer, conv_w, seq):
    m, k = xn.shape
    f = w_up.shape[2] // 2
    bm = _pick(seq, 1024, SUBLANE)
    bn = _pick(f, 256, LANE)
    nj = f // bn
    est = (2 * bm * k * 2 + 4 * k * bn * 4 + 2 * k * bn * 2 + 2 * bm * bn * 2 + 2 * nj * 16 * bn * 4
           + 2 * bm * bn * 4 + 12 * bm * bn * 4)
    cur = lambda j: jnp.minimum(j, nj - 1)
    prv = lambda j: jnp.maximum(j - 1, 0)
    return pl.pallas_call(
        functools.partial(_ffn_up_kernel, tiles_per_seq=seq // bm, nj=nj),
        grid=(m // bm, nj + 1),
        in_specs=[pl.BlockSpec((bm, k), lambda i, j: (i, 0)),
                  pl.BlockSpec((None, k, bn), lambda i, j: (layer, 0, cur(j))),
                  pl.BlockSpec((None, k, bn), lambda i, j: (layer, 0, cur(j) + nj)),
                  pl.BlockSpec((SUBLANE, bn), lambda i, j: (0, prv(j))),
                  pl.BlockSpec((SUBLANE, bn), lambda i, j: (0, prv(j) + nj))],
        out_specs=pl.BlockSpec((bm, bn), lambda i, j: (i, prv(j))),
        out_shape=jax.ShapeDtypeStruct((m, f), BF16),
        scratch_shapes=[pltpu.VMEM((2, bm, bn), F32), pltpu.VMEM((2, nj, 2 * SUBLANE, bn), F32)],
        compiler_params=_cparams(("arbitrary", "arbitrary"), est),
        name="ffn_up",
    )(xn, w_up, w_up, conv_w, conv_w)


GDN_ROWS = 256
GDN_HEADS_PER_STEP = 2
GDN_CONV = 4


def _bdot(a, b):
    return jnp.dot(a.astype(BF16), b.astype(BF16), preferred_element_type=F32)


def _bdot_nt(a, b):
    return lax.dot_general(a.astype(BF16), b.astype(BF16), (((1,), (1,)), ((), ())), preferred_element_type=F32)


def _bdot_tn(a, b):
    return lax.dot_general(a.astype(BF16), b.astype(BF16), (((0,), (0,)), ((), ())), preferred_element_type=F32)


def _gdn_head(alog, dtb, q, k, v, a_raw, b_raw, gate, ng, s, masks, rows):
    same, tril, tril_t, strict, eye, eye_f = masks
    a_raw = a_raw + dtb
    softplus = jnp.maximum(a_raw, 0.0) + jnp.log(1.0 + jnp.exp(-jnp.abs(a_raw)))
    g = -jnp.exp(jnp.zeros_like(a_raw) + alog) * softplus
    beta = jax.nn.sigmoid(b_raw)
    g_cols = jnp.broadcast_to(g, (rows, rows))
    g_rows = jnp.broadcast_to(jnp.sum(jnp.where(eye, g_cols, 0.0), axis=0, keepdims=True), (rows, rows))
    gc_col = jnp.sum(jnp.where(tril, g_rows, 0.0), axis=1, keepdims=True)
    gl_col = jnp.sum(jnp.where(same, g_rows, 0.0), axis=1, keepdims=True)
    gc_row = jnp.sum(jnp.where(tril_t, g_cols, 0.0), axis=0, keepdims=True)
    gc = jnp.broadcast_to(gc_col, (rows, LANE))
    gl = jnp.broadcast_to(gl_col, (rows, LANE))
    decay = jnp.exp(jnp.where(tril, gc_col - gc_row, MASK_VALUE))
    kb = k * beta
    vb = v * beta
    lmat = jnp.where(strict, _bdot_nt(kb, k) * decay, 0.0)
    tinv = eye_f - lmat
    pw = lmat
    for _ in range(int(math.log2(CHUNK)) - 1):
        pw = _bdot(pw, pw)
        tinv = tinv + _bdot(tinv, pw)
    eg = jnp.exp(gc)
    sol = _bdot(tinv, jnp.concatenate([vb, kb * eg], axis=1))
    u_c, w_c = sol[:, :GDN_DV], sol[:, GDN_DV:]
    intra = _bdot_nt(q, k) * decay
    q_dec = q * eg
    k_dec = k * jnp.exp(gl - gc)
    egl = jnp.exp(gl)
    iuw = _bdot(intra, sol)
    o_loc = iuw[:, :GDN_DV]
    q_eff = q_dec - iuw[:, GDN_DV:]
    outs = []
    for c in range(rows // CHUNK):
        lo, hi = c * CHUNK, (c + 1) * CHUNK
        kuw = _bdot_tn(k_dec[lo:hi], sol[lo:hi])
        outs.append(_bdot(q_eff[lo:hi], s) + o_loc[lo:hi])
        s = s * egl[lo:lo + 1, :] + kuw[:, :GDN_DV] - _bdot(kuw[:, GDN_DV:], s)
    o = jnp.concatenate(outs, axis=0)
    o = o * lax.rsqrt(jnp.mean(o * o, axis=-1, keepdims=True) + NORM_EPS) * ng
    return o * jax.nn.silu(gate), s


def _gdn_kernel(alog_ref, dtb_ref, q_ref, k_ref, v_ref, cq_ref, ck_ref, cv_ref, ab_ref, gate_ref, ng_ref,
                o_ref, state, hist, *, rows, hps, heads):
    hb = pl.program_id(1)
    r = pl.program_id(2)
    ab = ab_ref[...]
    ab_lane = lax.broadcasted_iota(jnp.int32, ab.shape, 1)

    @pl.when(r == 0)
    def _():
        state[...] = jnp.zeros_like(state)

    ri = lax.broadcasted_iota(jnp.int32, (rows, rows), 0)
    ci = lax.broadcasted_iota(jnp.int32, (rows, rows), 1)
    same = (ri // CHUNK) == (ci // CHUNK)
    tril = jnp.logical_and(same, ci <= ri)
    tril_t = jnp.logical_and(same, ri <= ci)
    strict = jnp.logical_and(same, ci < ri)
    eye = ri == ci
    masks = (same, tril, tril_t, strict, eye, jnp.where(eye, 1.0, 0.0))
    ng = ng_ref[...]

    @pl.when(r == 0)
    def _():
        hist[...] = jnp.zeros_like(hist)

    qkv = []
    for idx, (x_ref, w_ref) in enumerate(((q_ref, cq_ref), (k_ref, ck_ref), (v_ref, cv_ref))):
        x = x_ref[...]
        prev = hist[idx]
        hist[idx] = x[rows - SUBLANE:, :]
        w = w_ref[...]
        kw = GDN_CONV
        acc = w[kw - 1:kw] * x
        for sft in range(1, kw):
            acc = acc + w[kw - 1 - sft:kw - sft] * _shift_rows(x, prev, sft)
        qkv.append(jax.nn.silu(acc))

    def l2n(y):
        return y * lax.rsqrt(jnp.sum(y * y, axis=-1, keepdims=True) + NORM_EPS)

    for i in range(hps):
        cs = slice(i * GDN_DK, (i + 1) * GDN_DK)
        h = hb * hps + i
        a_raw = jnp.sum(jnp.where(ab_lane == h, ab, 0.0), axis=1, keepdims=True)
        b_raw = jnp.sum(jnp.where(ab_lane == heads + h, ab, 0.0), axis=1, keepdims=True)
        o, s = _gdn_head(alog_ref[h], dtb_ref[h], l2n(qkv[0][:, cs]) * (GDN_DK ** -0.5), l2n(qkv[1][:, cs]),
                         qkv[2][:, cs], a_raw, b_raw,
                         gate_ref[:, cs], ng, state[i], masks, rows)
        state[i] = s
        o_ref[:, cs] = o.astype(o_ref.dtype)


def _gdn_core(qkv, conv_w, ab, gate_src, gate_col0, a_log, dt_bias, norm_g, batch, seq, heads):
    rows = _pick(seq, GDN_ROWS, CHUNK)
    nr = seq // rows
    hps = GDN_HEADS_PER_STEP if heads % GDN_HEADS_PER_STEP == 0 else 1
    nhb = heads // hps
    wblk = hps * GDN_DK
    gcb = gate_col0 // wblk
    est = hps * (24 * rows * LANE * 4 + 16 * rows * rows * 4 + 8 * rows * LANE * 4)
    tok = lambda b, h, r: b * nr + r
    return pl.pallas_call(
        functools.partial(_gdn_kernel, rows=rows, hps=hps, heads=heads),
        grid=(batch, nhb, nr),
        in_specs=[pl.BlockSpec(memory_space=pltpu.SMEM),
                  pl.BlockSpec(memory_space=pltpu.SMEM),
                  pl.BlockSpec((rows, wblk), lambda b, h, r: (tok(b, h, r), h)),
                  pl.BlockSpec((rows, wblk), lambda b, h, r: (tok(b, h, r), nhb + h)),
                  pl.BlockSpec((rows, wblk), lambda b, h, r: (tok(b, h, r), 2 * nhb + h)),
                  pl.BlockSpec((SUBLANE, wblk), lambda b, h, r: (0, h)),
                  pl.BlockSpec((SUBLANE, wblk), lambda b, h, r: (0, nhb + h)),
                  pl.BlockSpec((SUBLANE, wblk), lambda b, h, r: (0, 2 * nhb + h)),
                  pl.BlockSpec((rows, LANE), lambda b, h, r: (tok(b, h, r), 0)),
                  pl.BlockSpec((rows, wblk), lambda b, h, r: (tok(b, h, r), gcb + h)),
                  pl.BlockSpec((1, GDN_DV), lambda b, h, r: (0, 0))],
        out_specs=pl.BlockSpec((rows, wblk), lambda b, h, r: (tok(b, h, r), h)),
        out_shape=jax.ShapeDtypeStruct((batch * seq, heads * GDN_DV), BF16),
        scratch_shapes=[pltpu.VMEM((hps, GDN_DK, GDN_DV), F32), pltpu.VMEM((3, SUBLANE, wblk), F32)],
        compiler_params=_cparams(("parallel", "parallel", "arbitrary"), est),
        name="gdn_core",
    )(a_log.astype(F32), dt_bias.astype(F32), qkv, qkv, qkv, conv_w, conv_w, conv_w, ab, gate_src,
      norm_g.reshape(1, GDN_DV).astype(F32))


SB_TQ = 512
SB_TK = 256
SB_HEADS_PER_STEP = 2


def _sb_kernel(q_ref, kt_ref, v_ref, o_ref, acc_s, c_s, *, tq, tk, hps):
    i = pl.program_id(2)
    r = tq // tk
    ur = lax.broadcasted_iota(jnp.int32, (2 * tk, tk), 0) % tk
    uc = lax.broadcasted_iota(jnp.int32, (2 * tk, tk), 1)
    upper2 = jnp.where(ur > uc, 1.0, 0.0).astype(BF16)
    ri = lax.broadcasted_iota(jnp.int32, (tq, tk), 0)
    ci = lax.broadcasted_iota(jnp.int32, (tq, tk), 1)
    acc_s[...] = jnp.zeros_like(acc_s)
    c_s[...] = jnp.zeros_like(c_s)

    def cols(hh):
        return slice(hh * SB_D, (hh + 1) * SB_D)

    def logits(hh, j):
        z = jnp.dot(q_ref[:, cols(hh)], kt_ref[j, cols(hh), :], preferred_element_type=F32)
        return z, jnp.maximum(z, 0.0) + jnp.log(1.0 + jnp.exp2(-jnp.abs(z))) * LOG2E

    def weights(z, lm, c):
        hi = lm.astype(BF16)
        lo = (lm - hi.astype(F32)).astype(BF16)
        rest = jnp.dot(jnp.concatenate([hi, lo], axis=1), upper2, preferred_element_type=F32)
        return jnp.exp2(z - lm - rest - c)

    def values(hh, j):
        return v_ref[pl.ds(pl.multiple_of(j * tk, tk), tk), cols(hh)]

    for d in range(r - 1, -1, -1):
        j = i * r + d
        before = (ci + d * tk) < ri
        for hh in range(hps):
            z, lm = logits(hh, j)
            lm = jnp.where(before, lm, 0.0)
            w = jnp.where(before, weights(z, lm, c_s[hh]), 0.0)
            acc_s[:, cols(hh)] += jnp.dot(w.astype(BF16), values(hh, j), preferred_element_type=F32)
            c_s[hh] += jnp.sum(lm, axis=1, keepdims=True)

    def single(j):
        for hh in range(hps):
            z, lm = logits(hh, j)
            w = weights(z, lm, c_s[hh])
            acc_s[:, cols(hh)] += jnp.dot(w.astype(BF16), values(hh, j), preferred_element_type=F32)
            c_s[hh] += jnp.sum(lm, axis=1, keepdims=True)

    def pair(p, carry):
        ja = i * r - 1 - 2 * p
        jb = ja - 1
        for hh in range(hps):
            za, lma = logits(hh, ja)
            zb, lmb = logits(hh, jb)
            c = c_s[hh]
            ta = jnp.sum(lma, axis=1, keepdims=True)
            wa = weights(za, lma, c)
            wb = weights(zb, lmb, c + ta)
            acc_s[:, cols(hh)] += (jnp.dot(wa.astype(BF16), values(hh, ja), preferred_element_type=F32)
                                   + jnp.dot(wb.astype(BF16), values(hh, jb), preferred_element_type=F32))
            c_s[hh] = c + ta + jnp.sum(lmb, axis=1, keepdims=True)
        return carry

    n_full = i * r
    lax.fori_loop(0, n_full // 2, pair, 0)
    if r % 2 == 1:
        @pl.when(n_full % 2 == 1)
        def _():
            single(0)
    o_ref[...] = acc_s[...].astype(o_ref.dtype)


def _sb_tiles(seq):
    tq = _pick(seq, SB_TQ, LANE)
    return tq, _pick(tq, SB_TK, LANE)


def _sb_attention(qv, kt, batch, seq, heads):
    tq, tk = _sb_tiles(seq)
    nq, nk = seq // tq, seq // tk
    hps = SB_HEADS_PER_STEP if heads % SB_HEADS_PER_STEP == 0 else 1
    nhb = heads // hps
    wblk = hps * SB_D
    est = hps * (4 * tq * SB_D * 2 + 4 * seq * SB_D * 2 + 2 * tq * SB_D * 4 + 24 * tq * tk * 4)
    return pl.pallas_call(
        functools.partial(_sb_kernel, tq=tq, tk=tk, hps=hps),
        grid=(batch, nhb, nq),
        in_specs=[pl.BlockSpec((tq, wblk), lambda b, h, i: (b * nq + i, h)),
                  pl.BlockSpec((nk, wblk, tk), lambda b, h, i: (b, h, 0)),
                  pl.BlockSpec((seq, wblk), lambda b, h, i: (b, nhb + h))],
        out_specs=pl.BlockSpec((tq, wblk), lambda b, h, i: (b * nq + i, h)),
        out_shape=jax.ShapeDtypeStruct((batch * seq, heads * SB_D), BF16),
        scratch_shapes=[pltpu.VMEM((tq, wblk), F32), pltpu.VMEM((hps, tq, 1), F32)],
        compiler_params=_cparams(("parallel", "parallel", "arbitrary"), est),
        name="sb_attention",
    )(qv, kt, qv)


def _pad_cols(w, n):
    return jnp.pad(w, ((0, 0), (0, n - w.shape[1])))


def _ffn(h, ln_g, w_up, layer, conv_w, w_down_bf16, seq):
    xn = _rmsnorm(h, ln_g, BF16)
    conv8 = jnp.pad(conv_w.astype(F32), ((0, SUBLANE - conv_w.shape[0]), (0, 0)))
    act = _ffn_up(xn, w_up, layer, conv8, seq)
    return _matmul([(act, w_down_bf16)], F32, residual=h, b_layer=layer)


def _even_layer(h, ln_g, w_in, w_out, q_norm, kv_norm, w_uq, w_ukv,
                a_re, a_im, log_step, b_re, b_im, c_re, c_im, d_skip, w_glu, batch, seq):
    d = h.shape[1]
    q_rank, kv_rank = q_norm.shape[0], kv_norm.shape[0]
    heads = w_uq.shape[1] // (MLA_NOPE + MLA_ROPE)
    s5_width = d_skip.shape[0]
    o_kr = q_rank + kv_rank
    o_s5 = o_kr + MLA_ROPE
    latent_w = -(-(o_kr + LANE) // s5_width) * s5_width if s5_width >= LANE else o_kr + LANE
    w_lat = _pad_cols(w_in[:, :o_s5], latent_w)
    w_cat = jnp.concatenate([w_lat, w_in[:, o_s5:]], axis=1).astype(BF16)
    hn = _rmsnorm(h, ln_g, BF16)
    proj = _matmul([(hn, w_cat)], F32)
    cq = _rmsnorm(proj, q_norm, BF16, col0=0, width=q_rank)
    ckv = _rmsnorm(proj, kv_norm, BF16, col0=q_rank, width=kv_rank)
    q_scale = (MLA_NOPE + MLA_ROPE) ** -0.5 * LOG2E
    w_uq_p = jnp.pad(w_uq.reshape(q_rank, heads, MLA_NOPE + MLA_ROPE) * q_scale,
                     ((0, 0), (0, 0), (0, MLA_QK - MLA_NOPE - MLA_ROPE))).reshape(q_rank, heads * MLA_QK)
    q = _matmul([(cq, w_uq_p.astype(BF16))], F32)
    tq = _mla_tile(seq)
    w_kv = w_ukv.reshape(kv_rank, heads, MLA_NOPE + MLA_V)
    w_kn = w_kv[:, :, :MLA_NOPE].reshape(kv_rank, heads * MLA_NOPE).astype(BF16)
    w_v = w_kv[:, :, MLA_NOPE:].reshape(kv_rank, heads * MLA_V).astype(BF16)
    knt = _matmul_keys_t(ckv, w_kn, tq)
    v = _matmul([(ckv, w_v)], BF16)
    cosf, sinf = _rope_tables(seq)
    krt = _rope_k(proj, o_kr, cosf, sinf, seq, tq)
    out_a = _mla_attention(q, knt, krt, v, cosf, sinf, batch, seq, heads)
    params = _s5t_params(a_re, a_im, log_step, b_re, b_im, c_re, c_im, d_skip)
    g = _s5_toeplitz(proj, latent_w, params, batch, seq)
    out_b = _s5_glu(g, w_glu.astype(BF16))
    na = heads * MLA_V
    w_out_all, w_out_layer = w_out
    return _matmul([(out_a, w_out_all), (out_b, w_out_all)], F32, residual=h,
                   b_layer=w_out_layer, b_row0=[0, na])


def _odd_layer(h, ln_g, w_in, w_out, conv_w, a_log, dt_bias, norm_g, batch, seq):
    heads = a_log.shape[0]
    qkv_w = heads * (2 * GDN_DK + GDN_DV)
    o2 = qkv_w + 2 * heads
    o3 = o2 + heads * GDN_DV
    hn = _rmsnorm(h, ln_g, BF16)
    w_t = jnp.swapaxes(w_in, 0, 1)
    w_main = jnp.concatenate([w_t[:qkv_w], w_t[o2:o3]], axis=0).astype(BF16)
    w_ab = jnp.pad(w_t[qkv_w:o2], ((0, LANE - (o2 - qkv_w)), (0, 0))).astype(BF16)
    sb_heads = (w_in.shape[1] - o3) // (3 * SB_D)
    nsq = sb_heads * SB_D
    w_sb_qv = jnp.concatenate([w_t[o3:o3 + nsq] * (SB_D ** -0.5 * LOG2E), w_t[o3 + 2 * nsq:]], axis=0).astype(BF16)
    w_sb_k = w_t[o3 + nsq:o3 + 2 * nsq].astype(BF16)
    main = _matmul([(hn, w_main)], F32, b_nk=True)
    ab = _matmul([(hn, w_ab)], F32, b_nk=True)
    sb_qv = _matmul([(hn, w_sb_qv)], BF16, b_nk=True)
    sb_kt = _matmul_keys_t(hn, w_sb_k, _sb_tiles(seq)[1], b_nk=True)
    conv8 = jnp.pad(conv_w.astype(F32), ((0, SUBLANE - conv_w.shape[0]), (0, 0)))
    out_c = _gdn_core(main, conv8, ab, main, qkv_w, a_log, dt_bias, norm_g, batch, seq, heads)
    out_d = _sb_attention(sb_qv, sb_kt, batch, seq, sb_heads)
    nc = heads * GDN_DV
    w_out_all, w_out_layer = w_out
    return _matmul([(out_c, w_out_all), (out_d, w_out_all)], F32, residual=h,
                   b_layer=w_out_layer, b_row0=[0, nc])


def kernel(x, ln_mix, ln_ffn, ln_final, ffn_w_up, ffn_conv, ffn_w_down, ev_w_in, ev_w_out, mla_q_norm,
           mla_kv_norm, mla_w_uq, mla_w_ukv, s5_a_re, s5_a_im, s5_log_step, s5_b_re, s5_b_im, s5_c_re,
           s5_c_im, s5_d, s5_w_glu, od_w_in, od_w_out, gdn_conv, gdn_a_log, gdn_dt_bias, gdn_norm):
    batch, seq, d = x.shape
    h = x.reshape(batch * seq, d).astype(F32)
    ffn_w_up = ffn_w_up.astype(F32)
    w_down_bf16 = ffn_w_down.astype(BF16)
    for layer in range(ln_mix.shape[0]):
        i = layer // 2
        if layer % 2 == 0:
            h = _even_layer(h, ln_mix[layer], ev_w_in[i], (ev_w_out.astype(F32), i), mla_q_norm[i], mla_kv_norm[i],
                            mla_w_uq[i], mla_w_ukv[i], s5_a_re[i], s5_a_im[i], s5_log_step[i], s5_b_re[i],
                            s5_b_im[i], s5_c_re[i], s5_c_im[i], s5_d[i], s5_w_glu[i], batch, seq)
        else:
            h = _odd_layer(h, ln_mix[layer], od_w_in[i], (od_w_out.astype(F32), i), gdn_conv[i], gdn_a_log[i],
                           gdn_dt_bias[i], gdn_norm[i], batch, seq)
        h = _ffn(h, ln_ffn[layer], ffn_w_up, layer, ffn_conv[layer], w_down_bf16, seq)
    return _rmsnorm(h, ln_final, x.dtype).reshape(batch, seq, d)
```

```python
import functools
import math

import jax
import jax.numpy as jnp
from jax import lax
from jax.experimental import pallas as pl
from jax.experimental.pallas import tpu as pltpu

F32 = jnp.float32
BF16 = jnp.bfloat16
HI = lax.Precision.HIGHEST

CHUNK = 64
NORM_EPS = 1e-6
MLA_NOPE = 128
MLA_ROPE = 64
MLA_V = 128
MLA_QK = 256
ROPE_BASE = 10000.0
S5_GROUP = 16
S5_STATE = 64
GDN_DK = 128
GDN_DV = 128
SB_D = 128

LANE = 128
SUBLANE = 8
VMEM_LIMIT_MAX = 58 * 1024 * 1024
VMEM_LIMIT_MIN = 32 * 1024 * 1024
MASK_VALUE = -1e30
LOG2E = math.log2(math.e)

RMSNORM_ROWS = 256
MATMUL_TILES = ((1024, 1024), (1024, 512), (512, 512), (512, 256), (256, 256), (256, 128), (128, 128))
MATMUL_VMEM_BUDGET = 52 * 1024 * 1024
KEYS_T_TILE = (1024, 512)
FFN_UP_TILE = (1024, 256)


def _cparams(sem, vmem_est):
    limit = int(min(max(vmem_est * 5 // 4 + (4 << 20), VMEM_LIMIT_MIN), VMEM_LIMIT_MAX))
    return pltpu.CompilerParams(dimension_semantics=sem, vmem_limit_bytes=limit)


def _pick(n, pref, mult):
    if n <= pref:
        return n
    t = (pref // mult) * mult
    while t >= mult:
        if n % t == 0:
            return t
        t -= mult
    return n


def _rmsnorm_kernel(x_ref, g_ref, o_ref):
    x = x_ref[...].astype(F32)
    ms = jnp.mean(x * x, axis=-1, keepdims=True)
    o_ref[...] = (x * lax.rsqrt(ms + NORM_EPS) * g_ref[...]).astype(o_ref.dtype)


def _rmsnorm(x, g, out_dtype, *, col0=0, width=None):
    m = x.shape[0]
    width = x.shape[1] if width is None else width
    tm = _pick(m, RMSNORM_ROWS, SUBLANE)
    cb = col0 // width
    est = 2 * tm * width * (x.dtype.itemsize + jnp.dtype(out_dtype).itemsize) + 4 * tm * width * 4
    return pl.pallas_call(
        _rmsnorm_kernel,
        grid=(m // tm,),
        in_specs=[pl.BlockSpec((tm, width), lambda i: (i, cb)),
                  pl.BlockSpec((1, width), lambda i: (0, 0))],
        out_specs=pl.BlockSpec((tm, width), lambda i: (i, 0)),
        out_shape=jax.ShapeDtypeStruct((m, width), out_dtype),
        compiler_params=_cparams(("parallel",), est),
        name="rmsnorm",
    )(x, g.reshape(1, width).astype(F32))


def _mm_kernel(*refs, n_pairs, has_res, b_nk):
    o_ref = refs[-1]
    acc = None
    for p in range(n_pairs):
        a = refs[2 * p][...].astype(BF16)
        b = refs[2 * p + 1][...].astype(BF16)
        if b_nk:
            d = lax.dot_general(a, b, (((1,), (1,)), ((), ())), preferred_element_type=F32)
        else:
            d = jnp.dot(a, b, preferred_element_type=F32)
        acc = d if acc is None else acc + d
    if has_res:
        acc = acc + refs[2 * n_pairs][...]
    o_ref[...] = acc.astype(o_ref.dtype)


def _mm_tiles(m, ks, n, a_bytes, out_bytes, has_res, budget=MATMUL_VMEM_BUDGET, b_bytes=2):
    for bm_p, bn_p in MATMUL_TILES:
        bm = _pick(m, bm_p, SUBLANE)
        bn = _pick(n, bn_p, LANE)
        est = bm * bn * 4
        for k, ab in zip(ks, a_bytes):
            est += 2 * (bm * k * ab + k * bn * b_bytes) + (k * bn * 2 if b_bytes > 2 else 0)
        est += 2 * bm * bn * (out_bytes + (4 if has_res else 0))
        if est <= budget:
            return bm, bn, est
    return bm, bn, est


def _matmul(pairs, out_dtype, residual=None, *, a_col0=None, b_layer=None, b_row0=None, b_nk=False):
    m = pairs[0][0].shape[0]
    n = pairs[0][1].shape[0] if b_nk else pairs[0][1].shape[-1]
    if b_row0 is not None:
        ks = [a.shape[1] for a, _ in pairs]
    else:
        ks = [b.shape[1] if b_nk else b.shape[-2] for _, b in pairs]
        b_row0 = [0] * len(pairs)
    a_col0 = [0] * len(pairs) if a_col0 is None else a_col0
    a_bytes = [a.dtype.itemsize for a, _ in pairs]
    bm, bn, est = _mm_tiles(m, ks, n, a_bytes, jnp.dtype(out_dtype).itemsize, residual is not None,
                            b_bytes=pairs[0][1].dtype.itemsize)
    in_specs, args = [], []
    for (a, b), k, c0, r0 in zip(pairs, ks, a_col0, b_row0):
        cb = c0 // k
        rb = r0 // k
        in_specs.append(pl.BlockSpec((bm, k), lambda i, j, cb=cb: (i, cb)))
        if b_nk:
            in_specs.append(pl.BlockSpec((bn, k), lambda i, j: (j, 0)))
        elif b_layer is None:
            in_specs.append(pl.BlockSpec((k, bn), lambda i, j: (0, j)))
        else:
            in_specs.append(pl.BlockSpec((None, k, bn), lambda i, j, rb=rb: (b_layer, rb, j)))
        args += [a, b]
    if residual is not None:
        in_specs.append(pl.BlockSpec((bm, bn), lambda i, j: (i, j)))
        args.append(residual)
    return pl.pallas_call(
        functools.partial(_mm_kernel, n_pairs=len(pairs), has_res=residual is not None, b_nk=b_nk),
        grid=(m // bm, n // bn),
        in_specs=in_specs,
        out_specs=pl.BlockSpec((bm, bn), lambda i, j: (i, j)),
        out_shape=jax.ShapeDtypeStruct((m, n), out_dtype),
        compiler_params=_cparams(("parallel", "parallel"), est),
        name="matmul",
    )(*args)


def _mm_t_kernel(a_ref, b_ref, o_ref, acc_ref, *, tk, b_nk):
    a = a_ref[...].astype(BF16)
    if b_nk:
        acc_ref[...] = lax.dot_general(a, b_ref[...], (((1,), (1,)), ((), ())), preferred_element_type=F32)
    else:
        acc_ref[...] = jnp.dot(a, b_ref[...], preferred_element_type=F32)
    res_t = jnp.transpose(acc_ref[...])
    for c in range(o_ref.shape[0]):
        o_ref[c] = res_t[:, c * tk:(c + 1) * tk].astype(o_ref.dtype)


def _matmul_keys_t(a, b, tk, *, b_nk=False):
    m, k = a.shape
    n = b.shape[0] if b_nk else b.shape[1]
    bm = _pick(m, KEYS_T_TILE[0], tk)
    bn = _pick(n, KEYS_T_TILE[1], LANE)
    est = 2 * (bm * k * a.dtype.itemsize + k * bn * 2 + bm * bn * 2) + 3 * bm * bn * 4
    return pl.pallas_call(
        functools.partial(_mm_t_kernel, tk=tk, b_nk=b_nk),
        grid=(m // bm, n // bn),
        in_specs=[pl.BlockSpec((bm, k), lambda i, j: (i, 0)),
                  pl.BlockSpec((bn, k), lambda i, j: (j, 0)) if b_nk else
                  pl.BlockSpec((k, bn), lambda i, j: (0, j))],
        out_specs=pl.BlockSpec((bm // tk, bn, tk), lambda i, j: (i, j, 0)),
        out_shape=jax.ShapeDtypeStruct((m // tk, n, tk), BF16),
        scratch_shapes=[pltpu.VMEM((bm, bn), F32)],
        compiler_params=_cparams(("parallel", "parallel"), est),
        name="matmul_keys_t",
    )(a, b)


def _rope_tables(seq):
    half = MLA_ROPE // 2
    inv_freq = ROPE_BASE ** (-jnp.arange(half, dtype=F32) / half)
    ang = jnp.arange(seq, dtype=F32)[:, None] * inv_freq[None, :]
    cos, sin = jnp.cos(ang), jnp.sin(ang)
    zeros = jnp.zeros((seq, LANE - MLA_ROPE), F32)
    return (jnp.concatenate([cos, cos, zeros], axis=1),
            jnp.concatenate([-sin, sin, zeros], axis=1))


def _rope_lanes(x, cosf, sinf):
    half = MLA_ROPE // 2
    lane = lax.broadcasted_iota(jnp.int32, x.shape, 1)
    swapped = jnp.where(lane < half, pltpu.roll(x, LANE - half, 1), pltpu.roll(x, half, 1))
    return x * cosf + swapped * sinf


def _rope_kernel(x_ref, cos_ref, sin_ref, o_ref):
    o_ref[0] = jnp.transpose(_rope_lanes(x_ref[...], cos_ref[...], sin_ref[...])).astype(o_ref.dtype)


def _rope_k(x, col0, cosf, sinf, seq, tq):
    m = x.shape[0]
    per_seq = seq // tq
    cb = col0 // LANE
    return pl.pallas_call(
        _rope_kernel,
        grid=(m // tq,),
        in_specs=[pl.BlockSpec((tq, LANE), lambda i: (i, cb)),
                  pl.BlockSpec((tq, LANE), lambda i: (i % per_seq, 0)),
                  pl.BlockSpec((tq, LANE), lambda i: (i % per_seq, 0))],
        out_specs=pl.BlockSpec((1, LANE, tq), lambda i: (i, 0, 0)),
        out_shape=jax.ShapeDtypeStruct((m // tq, LANE, tq), BF16),
        compiler_params=_cparams(("parallel",), 16 * tq * LANE * 4),
        name="rope_k",
    )(x, cosf, sinf)


def _mla_kernel(q_ref, knt_ref, krt_ref, v_ref, cos_ref, sin_ref, o_ref, qs, m_s, l_s, acc_s, *, tq, hps):
    i = pl.program_id(2)
    cos, sin = cos_ref[...], sin_ref[...]
    for hh in range(hps):
        q = q_ref[:, hh * MLA_QK:(hh + 1) * MLA_QK]
        qs[hh, :, :MLA_NOPE] = q[:, :MLA_NOPE].astype(BF16)
        qs[hh, :, MLA_NOPE:] = _rope_lanes(q[:, MLA_NOPE:], cos, sin).astype(BF16)

    def vcols(hh):
        return slice(hh * MLA_V, (hh + 1) * MLA_V)

    def tile(hh, j):
        kt = jnp.concatenate([knt_ref[j, hh * MLA_NOPE:(hh + 1) * MLA_NOPE, :], krt_ref[j]], axis=0)
        s = jnp.dot(qs[hh], kt, preferred_element_type=F32)
        return s, v_ref[pl.ds(pl.multiple_of(j * tq, tq), tq), vcols(hh)]

    rc = lax.broadcasted_iota(jnp.int32, (tq, tq), 0) // CHUNK
    cc = lax.broadcasted_iota(jnp.int32, (tq, tq), 1) // CHUNK
    for hh in range(hps):
        s, v = tile(hh, i)
        s = jnp.where(cc <= rc, s, MASK_VALUE)
        m = jnp.max(s, axis=1, keepdims=True)
        p = jnp.exp2(s - m)
        m_s[hh] = m
        l_s[hh] = jnp.sum(p, axis=1, keepdims=True)
        acc_s[:, vcols(hh)] = jnp.dot(p.astype(BF16), v, preferred_element_type=F32)

    def body(j, carry):
        for hh in range(hps):
            s, v = tile(hh, j)
            m_prev = m_s[hh]
            m_new = jnp.maximum(m_prev, jnp.max(s, axis=1, keepdims=True))
            alpha = jnp.exp2(m_prev - m_new)
            p = jnp.exp2(s - m_new)
            l_s[hh] = alpha * l_s[hh] + jnp.sum(p, axis=1, keepdims=True)
            acc_s[:, vcols(hh)] = (alpha * acc_s[:, vcols(hh)]
                                   + jnp.dot(p.astype(BF16), v, preferred_element_type=F32))
            m_s[hh] = m_new
        return carry

    lax.fori_loop(0, i, body, 0)
    for hh in range(hps):
        o_ref[:, vcols(hh)] = (acc_s[:, vcols(hh)] / l_s[hh]).astype(o_ref.dtype)


MLA_TILE = 1024
MLA_HEADS_PER_STEP = 2


def _mla_tile(seq):
    return _pick(seq, MLA_TILE, CHUNK)


def _mla_attention(q, knt, krt, v, cosf, sinf, batch, seq, heads):
    tq = _mla_tile(seq)
    nq = seq // tq
    hps = MLA_HEADS_PER_STEP if heads % MLA_HEADS_PER_STEP == 0 else 1
    est = hps * (2 * tq * MLA_QK * 4 + 2 * seq * (MLA_QK + LANE) * 2 + 4 * tq * LANE * 4 + 2 * tq * LANE * 2
                 + tq * MLA_QK * 2 + 3 * tq * LANE * 4 + 3 * tq * tq * 4)
    return pl.pallas_call(
        functools.partial(_mla_kernel, tq=tq, hps=hps),
        grid=(batch, heads // hps, nq),
        in_specs=[pl.BlockSpec((tq, hps * MLA_QK), lambda b, h, i: (b * nq + i, h)),
                  pl.BlockSpec((nq, hps * MLA_NOPE, tq), lambda b, h, i: (b, h, 0)),
                  pl.BlockSpec((nq, LANE, tq), lambda b, h, i: (b, 0, 0)),
                  pl.BlockSpec((seq, hps * MLA_V), lambda b, h, i: (b, h)),
                  pl.BlockSpec((tq, LANE), lambda b, h, i: (i, 0)),
                  pl.BlockSpec((tq, LANE), lambda b, h, i: (i, 0))],
        out_specs=pl.BlockSpec((tq, hps * MLA_V), lambda b, h, i: (b * nq + i, h)),
        out_shape=jax.ShapeDtypeStruct((batch * seq, heads * MLA_V), BF16),
        scratch_shapes=[pltpu.VMEM((hps, tq, MLA_QK), BF16), pltpu.VMEM((hps, tq, 1), F32),
                        pltpu.VMEM((hps, tq, 1), F32), pltpu.VMEM((tq, hps * MLA_V), F32)],
        compiler_params=_cparams(("parallel", "parallel", "arbitrary"), est),
        name="mla_attention",
    )(q, knt, krt, v, cosf, sinf)


S5_BLOCK = 16


def _s5t_kernel(x_ref, toep_ref, ere_ref, eim_ref, fre_ref, fim_ref, coef_ref, d_ref, o_ref,
                u_ref, y_ref, er_s, ei_s, hr_s, hi_s, *, gps, nb, nb_seq):
    t_blk = S5_BLOCK
    tw = t_blk * S5_GROUP
    npair = gps // 2
    per_slab = LANE // S5_GROUP
    steps = [x_ref[pl.ds(t, nb, stride=t_blk), :] for t in range(t_blk)]
    win = lax.broadcasted_iota(jnp.int32, (nb, LANE), 1) // S5_GROUP
    in_win = [win == w for w in range(per_slab)]

    def place(pieces):
        out = None
        for w, (arr, src) in enumerate(pieces):
            sh = ((w - src) % per_slab) * S5_GROUP
            r = arr if sh == 0 else pltpu.roll(arr, sh, 1)
            out = r if out is None else jnp.where(in_win[w], r, out)
        return out

    for g in range(gps):
        for hf in range(tw // LANE):
            u_ref[g, :, hf * LANE:(hf + 1) * LANE] = place([(steps[hf * per_slab + tt], g) for tt in range(per_slab)])
    for p in range(npair):
        up = jnp.concatenate([u_ref[2 * p].astype(BF16), u_ref[2 * p + 1].astype(BF16)], axis=1)
        er_s[p] = jnp.dot(up, ere_ref[p], preferred_element_type=F32)
        ei_s[p] = jnp.dot(up, eim_ref[p], preferred_element_type=F32)
    row = lax.broadcasted_iota(jnp.int32, (SUBLANE, LANE), 0)
    groups_per_seq = nb_seq // SUBLANE

    def body(r, carry):
        off = pl.multiple_of(r * SUBLANE, SUBLANE)
        keep = jnp.where((r % groups_per_seq) == 0, 0.0, 1.0)
        new = []
        for p in range(npair):
            xr = er_s[p, pl.ds(off, SUBLANE), :]
            xi = ei_s[p, pl.ds(off, SUBLANE), :]
            for idx, k in enumerate((1, 2, 4)):
                ar, ai = coef_ref[p, 2 * idx], coef_ref[p, 2 * idx + 1]
                sr = jnp.where(row >= k, pltpu.roll(xr, k, 0), 0.0)
                si = jnp.where(row >= k, pltpu.roll(xi, k, 0), 0.0)
                xr, xi = xr + ar * sr - ai * si, xi + ar * si + ai * sr
            cr = jnp.broadcast_to(carry[p][0] * keep, (SUBLANE, LANE))
            ci = jnp.broadcast_to(carry[p][1] * keep, (SUBLANE, LANE))
            pwr, pwi = coef_ref[p, 6], coef_ref[p, 7]
            xr, xi = xr + pwr * cr - pwi * ci, xi + pwr * ci + pwi * cr
            hr_s[p, pl.ds(off, SUBLANE), :] = jnp.where(row == 0, cr, pltpu.roll(xr, 1, 0))
            hi_s[p, pl.ds(off, SUBLANE), :] = jnp.where(row == 0, ci, pltpu.roll(xi, 1, 0))
            new.append((xr[SUBLANE - 1:, :], xi[SUBLANE - 1:, :]))
        return tuple(new)

    zero = jnp.zeros((1, LANE), F32)
    lax.fori_loop(0, nb // SUBLANE, body, tuple((zero, zero) for _ in range(npair)))
    for p in range(npair):
        yc = (jnp.dot(hr_s[p].astype(BF16), fre_ref[p], preferred_element_type=F32)
              + jnp.dot(hi_s[p].astype(BF16), fim_ref[p], preferred_element_type=F32))
        for a in range(2):
            g = 2 * p + a
            u = u_ref[g]
            y = jnp.dot(u.astype(BF16), toep_ref[g], preferred_element_type=F32) + yc[:, a * tw:(a + 1) * tw]
            y_ref[g] = jax.nn.gelu(y + d_ref[g] * u, approximate=True)
    for t in range(t_blk):
        hf, tt = divmod(t, per_slab)
        o_ref[pl.ds(t, nb, stride=t_blk), :] = place(
            [(y_ref[g, :, hf * LANE:(hf + 1) * LANE], tt) for g in range(gps)])


def _s5t_params(a_re, a_im, log_step, b_re, b_im, c_re, c_im, d_skip):
    g, p = a_re.shape
    t = S5_BLOCK
    lam = lax.complex(a_re.astype(F32), a_im.astype(F32))
    lam_dt = lam * jnp.exp(log_step.astype(F32))[:, None]
    lam_bar = jnp.exp(lam_dt)
    b_bar = ((lam_bar - 1.0) / lam)[..., None] * lax.complex(b_re.astype(F32), b_im.astype(F32))
    c_c = lax.complex(c_re.astype(F32), c_im.astype(F32))
    tau = jnp.arange(t, dtype=F32)
    apow = jnp.exp(lam_dt[None] * tau[:, None, None])
    apow1 = jnp.exp(lam_dt[None] * (tau + 1.0)[:, None, None])
    kern = jnp.real(jnp.einsum('gip,tgp,gpj->gtij', c_c, apow, b_bar, precision=HI))
    lag = jnp.arange(t)[None, :] - jnp.arange(t)[:, None]
    shift = (lag[None] == jnp.arange(t)[:, None, None]).astype(F32)
    toep = jnp.einsum('gtij,tsu->gsjui', kern, shift, precision=HI)
    toep = toep.reshape(g, t * S5_GROUP, t * S5_GROUP)
    e_c = (apow[::-1][:, :, None, :] * jnp.moveaxis(b_bar, 2, 1)[None]).transpose(1, 0, 2, 3)
    e_c = e_c.reshape(g, t * S5_GROUP, p)
    f_c = (c_c[:, None] * apow1.transpose(1, 0, 2)[:, :, None, :])
    f_c = jnp.moveaxis(f_c.reshape(g, t * S5_GROUP, p), 1, 2)
    eye2 = jnp.eye(2, dtype=F32)

    def pair_in(x):
        return jnp.einsum('narp,ab->narbp', x.reshape(g // 2, 2, t * S5_GROUP, p), eye2).reshape(
            g // 2, 2 * t * S5_GROUP, 2 * p)

    def pair_out(x):
        return jnp.einsum('napc,ab->napbc', x.reshape(g // 2, 2, p, t * S5_GROUP), eye2).reshape(
            g // 2, 2 * p, 2 * t * S5_GROUP)

    def planes(z):
        z = jnp.broadcast_to(z, (g, SUBLANE, p)).reshape(g // 2, 2, SUBLANE, p).transpose(0, 2, 1, 3)
        z = z.reshape(g // 2, SUBLANE, 2 * p)
        return [jnp.real(z), jnp.imag(z)]

    coef = []
    for k in (1, 2, 4):
        coef += planes(jnp.exp(lam_dt * float(t * k))[:, None, :])
    coef += planes(jnp.exp(lam_dt[:, None, :] * (float(t) * jnp.arange(1, SUBLANE + 1, dtype=F32))[None, :, None]))
    dtile = jnp.tile(d_skip.astype(F32).reshape(g, 1, S5_GROUP), (1, 1, t))
    return (toep.astype(BF16), pair_in(jnp.real(e_c)).astype(BF16), pair_in(jnp.imag(e_c)).astype(BF16),
            pair_out(jnp.real(f_c)).astype(BF16), pair_out(-jnp.imag(f_c)).astype(BF16),
            jnp.stack(coef, axis=1), dtile)


def _s5_toeplitz(u_src, col0, params, batch, seq):
    toep, ere, eim, fre, fim, coef, dtile = params
    g = toep.shape[0]
    t = S5_BLOCK
    tw = t * S5_GROUP
    ns = 2 * S5_STATE
    m = u_src.shape[0]
    nb = m // t
    gps = LANE // S5_GROUP
    npair = gps // 2
    cb0 = col0 // LANE
    est = 4 * m * LANE * 4 + 2 * gps * nb * tw * 4 + 4 * npair * nb * ns * 4 + 16 * gps * tw * tw * 2 + 12 * nb * tw * 4
    return pl.pallas_call(
        functools.partial(_s5t_kernel, gps=gps, nb=nb, nb_seq=seq // t),
        grid=(g // gps,),
        in_specs=[pl.BlockSpec((m, LANE), lambda i: (0, cb0 + i)),
                  pl.BlockSpec((gps, tw, tw), lambda i: (i, 0, 0)),
                  pl.BlockSpec((npair, 2 * tw, ns), lambda i: (i, 0, 0)),
                  pl.BlockSpec((npair, 2 * tw, ns), lambda i: (i, 0, 0)),
                  pl.BlockSpec((npair, ns, 2 * tw), lambda i: (i, 0, 0)),
                  pl.BlockSpec((npair, ns, 2 * tw), lambda i: (i, 0, 0)),
                  pl.BlockSpec((npair, 8, SUBLANE, ns), lambda i: (i, 0, 0, 0)),
                  pl.BlockSpec((gps, 1, tw), lambda i: (i, 0, 0))],
        out_specs=pl.BlockSpec((m, LANE), lambda i: (0, i)),
        out_shape=jax.ShapeDtypeStruct((m, g * S5_GROUP), F32),
        scratch_shapes=([pltpu.VMEM((gps, nb, tw), F32) for _ in range(2)]
                        + [pltpu.VMEM((npair, nb, ns), F32) for _ in range(4)]),
        compiler_params=_cparams(("parallel",), est),
        name="s5_blocks",
    )(u_src, toep, ere, eim, fre, fim, coef, dtile)


def _glu_kernel(a_ref, w_ref, g_ref, o_ref):
    z = jnp.dot(a_ref[...].astype(BF16), w_ref[...], preferred_element_type=F32)
    o_ref[...] = (g_ref[...] * jax.nn.sigmoid(z)).astype(o_ref.dtype)


def _s5_glu(g, w_glu):
    m, k = g.shape
    bm, bn, est = _mm_tiles(m, [k], k, [4], 2, True)
    return pl.pallas_call(
        _glu_kernel,
        grid=(m // bm, k // bn),
        in_specs=[pl.BlockSpec((bm, k), lambda i, j: (i, 0)),
                  pl.BlockSpec((k, bn), lambda i, j: (0, j)),
                  pl.BlockSpec((bm, bn), lambda i, j: (i, j))],
        out_specs=pl.BlockSpec((bm, bn), lambda i, j: (i, j)),
        out_shape=jax.ShapeDtypeStruct((m, k), BF16),
        compiler_params=_cparams(("parallel", "parallel"), est),
        name="s5_glu",
    )(g, w_glu, g)


def _shift_rows(h, prev, k):
    r = pltpu.roll(h, k, 0)
    p = pltpu.roll(prev, k, 0)
    row = lax.broadcasted_iota(jnp.int32, prev.shape, 0)
    top = jnp.where(row < k, p, r[:SUBLANE])
    return jnp.concatenate([top, r[SUBLANE:]], axis=0)


def _ffn_up_kernel(a_ref, wa_ref, wb_ref, cwa_ref, cwb_ref, o_ref, hbuf, carry, *, tiles_per_seq, nj):
    i, j = pl.program_id(0), pl.program_id(1)
    slot = i % 2
    rows = hbuf.shape[1]

    @pl.when(jnp.logical_and(i == 0, j == 0))
    def _():
        hbuf[...] = jnp.zeros_like(hbuf)
        carry[...] = jnp.zeros_like(carry)

    def epilogue():
        jp = jnp.maximum(j - 1, 0)
        keep = jnp.where((i % tiles_per_seq) == 0, 0.0, 1.0)
        outs = []
        for s, cw_ref in enumerate((cwa_ref, cwb_ref)):
            h = hbuf[s]
            prev = carry[1 - slot, jp, s * SUBLANE:(s + 1) * SUBLANE, :] * keep
            cw = cw_ref[...]
            outs.append(cw[2:3] * h + cw[1:2] * _shift_rows(h, prev, 1) + cw[0:1] * _shift_rows(h, prev, 2))
        o_ref[...] = (jax.nn.silu(outs[0]) * outs[1]).astype(o_ref.dtype)

    @pl.when(j < nj)
    def _():
        epilogue()
        a = a_ref[...]
        ha = jnp.dot(a, wa_ref[...].astype(BF16), preferred_element_type=F32)
        hb = jnp.dot(a, wb_ref[...].astype(BF16), preferred_element_type=F32)
        hbuf[0] = ha
        hbuf[1] = hb
        carry[slot, j, 0:SUBLANE, :] = ha[rows - SUBLANE:, :]
        carry[slot, j, SUBLANE:2 * SUBLANE, :] = hb[rows - SUBLANE:, :]

    @pl.when(j == nj)
    def _():
        epilogue()


def _ffn_up(xn, w_up, layer, conv_w, seq):
    m, k = xn.shape
    f = w_up.shape[2] // 2
    bm = _pick(seq, FFN_UP_TILE[0], SUBLANE)
    bn = _pick(f, FFN_UP_TILE[1], LANE)
    nj = f // bn
    est = (2 * bm * k * 2 + 4 * k * bn * 4 + 2 * k * bn * 2 + 2 * bm * bn * 2 + 2 * nj * 16 * bn * 4
           + 2 * bm * bn * 4 + 12 * bm * bn * 4)
    cur = lambda j: jnp.minimum(j, nj - 1)
    prv = lambda j: jnp.maximum(j - 1, 0)
    return pl.pallas_call(
        functools.partial(_ffn_up_kernel, tiles_per_seq=seq // bm, nj=nj),
        grid=(m // bm, nj + 1),
        in_specs=[pl.BlockSpec((bm, k), lambda i, j: (i, 0)),
                  pl.BlockSpec((None, k, bn), lambda i, j: (layer, 0, cur(j))),
                  pl.BlockSpec((None, k, bn), lambda i, j: (layer, 0, cur(j) + nj)),
                  pl.BlockSpec((SUBLANE, bn), lambda i, j: (0, prv(j))),
                  pl.BlockSpec((SUBLANE, bn), lambda i, j: (0, prv(j) + nj))],
        out_specs=pl.BlockSpec((bm, bn), lambda i, j: (i, prv(j))),
        out_shape=jax.ShapeDtypeStruct((m, f), BF16),
        scratch_shapes=[pltpu.VMEM((2, bm, bn), F32), pltpu.VMEM((2, nj, 2 * SUBLANE, bn), F32)],
        compiler_params=_cparams(("arbitrary", "arbitrary"), est),
        name="ffn_up",
    )(xn, w_up, w_up, conv_w, conv_w)


GDN_ROWS = 256
GDN_HEADS_PER_STEP = 4
GDN_CONV = 4


def _bdot(a, b):
    return jnp.dot(a.astype(BF16), b.astype(BF16), preferred_element_type=F32)


def _bdot_nt(a, b):
    return lax.dot_general(a.astype(BF16), b.astype(BF16), (((1,), (1,)), ((), ())), preferred_element_type=F32)


def _bdot_tn(a, b):
    return lax.dot_general(a.astype(BF16), b.astype(BF16), (((0,), (0,)), ((), ())), preferred_element_type=F32)


def _gdn_head(alog, dtb, q, k, v, a_raw, b_raw, gate, ng, s, masks, rows):
    same, tril, tril_t, strict, eye, eye_f = masks
    a_raw = a_raw + dtb
    softplus = jnp.maximum(a_raw, 0.0) + jnp.log(1.0 + jnp.exp(-jnp.abs(a_raw)))
    g = -jnp.exp(jnp.zeros_like(a_raw) + alog) * softplus
    beta = jax.nn.sigmoid(b_raw)
    g_cols = jnp.broadcast_to(g, (rows, rows))
    g_rows = jnp.broadcast_to(jnp.sum(jnp.where(eye, g_cols, 0.0), axis=0, keepdims=True), (rows, rows))
    gc_col = jnp.sum(jnp.where(tril, g_rows, 0.0), axis=1, keepdims=True)
    gl_col = jnp.sum(jnp.where(same, g_rows, 0.0), axis=1, keepdims=True)
    gc_row = jnp.sum(jnp.where(tril_t, g_cols, 0.0), axis=0, keepdims=True)
    gc = jnp.broadcast_to(gc_col, (rows, LANE))
    gl = jnp.broadcast_to(gl_col, (rows, LANE))
    decay = jnp.exp(jnp.where(tril, gc_col - gc_row, MASK_VALUE))
    kb = k * beta
    vb = v * beta
    lmat = jnp.where(strict, _bdot_nt(kb, k) * decay, 0.0)
    tinv = eye_f - lmat
    pw = lmat
    for _ in range(int(math.log2(CHUNK)) - 1):
        pw = _bdot(pw, pw)
        tinv = tinv + _bdot(tinv, pw)
    eg = jnp.exp(gc)
    sol = _bdot(tinv, jnp.concatenate([vb, kb * eg], axis=1))
    u_c, w_c = sol[:, :GDN_DV], sol[:, GDN_DV:]
    intra = _bdot_nt(q, k) * decay
    q_dec = q * eg
    k_dec = k * jnp.exp(gl - gc)
    egl = jnp.exp(gl)
    iuw = _bdot(intra, sol)
    o_loc = iuw[:, :GDN_DV]
    q_eff = q_dec - iuw[:, GDN_DV:]
    outs = []
    for c in range(rows // CHUNK):
        lo, hi = c * CHUNK, (c + 1) * CHUNK
        kuw = _bdot_tn(k_dec[lo:hi], sol[lo:hi])
        outs.append(_bdot(q_eff[lo:hi], s) + o_loc[lo:hi])
        s = s * egl[lo:lo + 1, :] + kuw[:, :GDN_DV] - _bdot(kuw[:, GDN_DV:], s)
    o = jnp.concatenate(outs, axis=0)
    o = o * lax.rsqrt(jnp.mean(o * o, axis=-1, keepdims=True) + NORM_EPS) * ng
    return o * jax.nn.silu(gate), s


def _gdn_kernel(alog_ref, dtb_ref, q_ref, k_ref, v_ref, cq_ref, ck_ref, cv_ref, ab_ref, gate_ref, ng_ref,
                o_ref, state, hist, *, rows, hps, heads):
    hb = pl.program_id(1)
    r = pl.program_id(2)
    ab = ab_ref[...]
    ab_lane = lax.broadcasted_iota(jnp.int32, ab.shape, 1)

    @pl.when(r == 0)
    def _():
        state[...] = jnp.zeros_like(state)

    ri = lax.broadcasted_iota(jnp.int32, (rows, rows), 0)
    ci = lax.broadcasted_iota(jnp.int32, (rows, rows), 1)
    same = (ri // CHUNK) == (ci // CHUNK)
    tril = jnp.logical_and(same, ci <= ri)
    tril_t = jnp.logical_and(same, ri <= ci)
    strict = jnp.logical_and(same, ci < ri)
    eye = ri == ci
    masks = (same, tril, tril_t, strict, eye, jnp.where(eye, 1.0, 0.0))
    ng = ng_ref[...]

    @pl.when(r == 0)
    def _():
        hist[...] = jnp.zeros_like(hist)

    qkv = []
    for idx, (x_ref, w_ref) in enumerate(((q_ref, cq_ref), (k_ref, ck_ref), (v_ref, cv_ref))):
        x = x_ref[...]
        prev = hist[idx]
        hist[idx] = x[rows - SUBLANE:, :]
        w = w_ref[...]
        kw = GDN_CONV
        acc = w[kw - 1:kw] * x
        for sft in range(1, kw):
            acc = acc + w[kw - 1 - sft:kw - sft] * _shift_rows(x, prev, sft)
        qkv.append(jax.nn.silu(acc))

    def l2n(y):
        return y * lax.rsqrt(jnp.sum(y * y, axis=-1, keepdims=True) + NORM_EPS)

    for i in range(hps):
        cs = slice(i * GDN_DK, (i + 1) * GDN_DK)
        h = hb * hps + i
        a_raw = jnp.sum(jnp.where(ab_lane == h, ab, 0.0), axis=1, keepdims=True)
        b_raw = jnp.sum(jnp.where(ab_lane == heads + h, ab, 0.0), axis=1, keepdims=True)
        o, s = _gdn_head(alog_ref[h], dtb_ref[h], l2n(qkv[0][:, cs]) * (GDN_DK ** -0.5), l2n(qkv[1][:, cs]),
                         qkv[2][:, cs], a_raw, b_raw,
                         gate_ref[:, cs], ng, state[i], masks, rows)
        state[i] = s
        o_ref[:, cs] = o.astype(o_ref.dtype)


def _gdn_core(qkv, conv_w, ab, gate_src, gate_col0, a_log, dt_bias, norm_g, batch, seq, heads):
    rows = _pick(seq, GDN_ROWS, CHUNK)
    nr = seq // rows
    hps = GDN_HEADS_PER_STEP if heads % GDN_HEADS_PER_STEP == 0 else 1
    nhb = heads // hps
    wblk = hps * GDN_DK
    gcb = gate_col0 // wblk
    est = hps * (24 * rows * LANE * 4 + 16 * rows * rows * 4 + 8 * rows * LANE * 4)
    tok = lambda b, h, r: b * nr + r
    return pl.pallas_call(
        functools.partial(_gdn_kernel, rows=rows, hps=hps, heads=heads),
        grid=(batch, nhb, nr),
        in_specs=[pl.BlockSpec(memory_space=pltpu.SMEM),
                  pl.BlockSpec(memory_space=pltpu.SMEM),
                  pl.BlockSpec((rows, wblk), lambda b, h, r: (tok(b, h, r), h)),
                  pl.BlockSpec((rows, wblk), lambda b, h, r: (tok(b, h, r), nhb + h)),
                  pl.BlockSpec((rows, wblk), lambda b, h, r: (tok(b, h, r), 2 * nhb + h)),
                  pl.BlockSpec((SUBLANE, wblk), lambda b, h, r: (0, h)),
                  pl.BlockSpec((SUBLANE, wblk), lambda b, h, r: (0, nhb + h)),
                  pl.BlockSpec((SUBLANE, wblk), lambda b, h, r: (0, 2 * nhb + h)),
                  pl.BlockSpec((rows, LANE), lambda b, h, r: (tok(b, h, r), 0)),
                  pl.BlockSpec((rows, wblk), lambda b, h, r: (tok(b, h, r), gcb + h)),
                  pl.BlockSpec((1, GDN_DV), lambda b, h, r: (0, 0))],
        out_specs=pl.BlockSpec((rows, wblk), lambda b, h, r: (tok(b, h, r), h)),
        out_shape=jax.ShapeDtypeStruct((batch * seq, heads * GDN_DV), BF16),
        scratch_shapes=[pltpu.VMEM((hps, GDN_DK, GDN_DV), F32), pltpu.VMEM((3, SUBLANE, wblk), F32)],
        compiler_params=_cparams(("parallel", "parallel", "arbitrary"), est),
        name="gdn_core",
    )(a_log.astype(F32), dt_bias.astype(F32), qkv, qkv, qkv, conv_w, conv_w, conv_w, ab, gate_src,
      norm_g.reshape(1, GDN_DV).astype(F32))


SB_TQ = 512
SB_TK = 256
SB_HEADS_PER_STEP = 2


def _sb_kernel(q_ref, kt_ref, v_ref, o_ref, acc_s, c_s, *, tq, tk, hps):
    i = pl.program_id(2)
    r = tq // tk
    ur = lax.broadcasted_iota(jnp.int32, (2 * tk, tk), 0) % tk
    uc = lax.broadcasted_iota(jnp.int32, (2 * tk, tk), 1)
    upper2 = jnp.where(ur > uc, 1.0, 0.0).astype(BF16)
    ri = lax.broadcasted_iota(jnp.int32, (tq, tk), 0)
    ci = lax.broadcasted_iota(jnp.int32, (tq, tk), 1)
    acc_s[...] = jnp.zeros_like(acc_s)
    c_s[...] = jnp.zeros_like(c_s)

    def cols(hh):
        return slice(hh * SB_D, (hh + 1) * SB_D)

    def logits(hh, j):
        z = jnp.dot(q_ref[:, cols(hh)], kt_ref[j, cols(hh), :], preferred_element_type=F32)
        return z, jnp.maximum(z, 0.0) + jnp.log(1.0 + jnp.exp2(-jnp.abs(z))) * LOG2E

    def weights(z, lm, c):
        hi = lm.astype(BF16)
        lo = (lm - hi.astype(F32)).astype(BF16)
        rest = jnp.dot(jnp.concatenate([hi, lo], axis=1), upper2, preferred_element_type=F32)
        return jnp.exp2(z - lm - rest - c)

    def values(hh, j):
        return v_ref[pl.ds(pl.multiple_of(j * tk, tk), tk), cols(hh)]

    for d in range(r - 1, -1, -1):
        j = i * r + d
        before = (ci + d * tk) < ri
        for hh in range(hps):
            z, lm = logits(hh, j)
            lm = jnp.where(before, lm, 0.0)
            w = jnp.where(before, weights(z, lm, c_s[hh]), 0.0)
            acc_s[:, cols(hh)] += jnp.dot(w.astype(BF16), values(hh, j), preferred_element_type=F32)
            c_s[hh] += jnp.sum(lm, axis=1, keepdims=True)

    def single(j):
        for hh in range(hps):
            z, lm = logits(hh, j)
            w = weights(z, lm, c_s[hh])
            acc_s[:, cols(hh)] += jnp.dot(w.astype(BF16), values(hh, j), preferred_element_type=F32)
            c_s[hh] += jnp.sum(lm, axis=1, keepdims=True)

    def pair(p, carry):
        ja = i * r - 1 - 2 * p
        jb = ja - 1
        for hh in range(hps):
            za, lma = logits(hh, ja)
            zb, lmb = logits(hh, jb)
            c = c_s[hh]
            ta = jnp.sum(lma, axis=1, keepdims=True)
            wa = weights(za, lma, c)
            wb = weights(zb, lmb, c + ta)
            acc_s[:, cols(hh)] += (jnp.dot(wa.astype(BF16), values(hh, ja), preferred_element_type=F32)
                                   + jnp.dot(wb.astype(BF16), values(hh, jb), preferred_element_type=F32))
            c_s[hh] = c + ta + jnp.sum(lmb, axis=1, keepdims=True)
        return carry

    n_full = i * r
    lax.fori_loop(0, n_full // 2, pair, 0)
    if r % 2 == 1:
        @pl.when(n_full % 2 == 1)
        def _():
            single(0)
    o_ref[...] = acc_s[...].astype(o_ref.dtype)


def _sb_tiles(seq):
    tq = _pick(seq, SB_TQ, LANE)
    return tq, _pick(tq, SB_TK, LANE)


def _sb_attention(qv, kt, batch, seq, heads):
    tq, tk = _sb_tiles(seq)
    nq, nk = seq // tq, seq // tk
    hps = SB_HEADS_PER_STEP if heads % SB_HEADS_PER_STEP == 0 else 1
    nhb = heads // hps
    wblk = hps * SB_D
    est = hps * (4 * tq * SB_D * 2 + 4 * seq * SB_D * 2 + 2 * tq * SB_D * 4 + 24 * tq * tk * 4)
    return pl.pallas_call(
        functools.partial(_sb_kernel, tq=tq, tk=tk, hps=hps),
        grid=(batch, nhb, nq),
        in_specs=[pl.BlockSpec((tq, wblk), lambda b, h, i: (b * nq + i, h)),
                  pl.BlockSpec((nk, wblk, tk), lambda b, h, i: (b, h, 0)),
                  pl.BlockSpec((seq, wblk), lambda b, h, i: (b, nhb + h))],
        out_specs=pl.BlockSpec((tq, wblk), lambda b, h, i: (b * nq + i, h)),
        out_shape=jax.ShapeDtypeStruct((batch * seq, heads * SB_D), BF16),
        scratch_shapes=[pltpu.VMEM((tq, wblk), F32), pltpu.VMEM((hps, tq, 1), F32)],
        compiler_params=_cparams(("parallel", "parallel", "arbitrary"), est),
        name="sb_attention",
    )(qv, kt, qv)


def _pad_cols(w, n):
    return jnp.pad(w, ((0, 0), (0, n - w.shape[1])))


def _ffn(h, ln_g, w_up, layer, conv_w, w_down_bf16, seq):
    xn = _rmsnorm(h, ln_g, BF16)
    conv8 = jnp.pad(conv_w.astype(F32), ((0, SUBLANE - conv_w.shape[0]), (0, 0)))
    act = _ffn_up(xn, w_up, layer, conv8, seq)
    return _matmul([(act, w_down_bf16)], F32, residual=h, b_layer=layer)


def _even_layer(h, ln_g, w_in, w_out, q_norm, kv_norm, w_uq, w_ukv,
                a_re, a_im, log_step, b_re, b_im, c_re, c_im, d_skip, w_glu, batch, seq):
    d = h.shape[1]
    q_rank, kv_rank = q_norm.shape[0], kv_norm.shape[0]
    heads = w_uq.shape[1] // (MLA_NOPE + MLA_ROPE)
    s5_width = d_skip.shape[0]
    o_kr = q_rank + kv_rank
    o_s5 = o_kr + MLA_ROPE
    latent_w = -(-(o_kr + LANE) // s5_width) * s5_width if s5_width >= LANE else o_kr + LANE
    w_lat = _pad_cols(w_in[:, :o_s5], latent_w)
    w_cat = jnp.concatenate([w_lat, w_in[:, o_s5:]], axis=1).astype(BF16)
    hn = _rmsnorm(h, ln_g, BF16)
    proj = _matmul([(hn, w_cat)], F32)
    cq = _rmsnorm(proj, q_norm, BF16, col0=0, width=q_rank)
    ckv = _rmsnorm(proj, kv_norm, BF16, col0=q_rank, width=kv_rank)
    q_scale = (MLA_NOPE + MLA_ROPE) ** -0.5 * LOG2E
    w_uq_p = jnp.pad(w_uq.reshape(q_rank, heads, MLA_NOPE + MLA_ROPE) * q_scale,
                     ((0, 0), (0, 0), (0, MLA_QK - MLA_NOPE - MLA_ROPE))).reshape(q_rank, heads * MLA_QK)
    q = _matmul([(cq, w_uq_p.astype(BF16))], F32)
    tq = _mla_tile(seq)
    w_kv = w_ukv.reshape(kv_rank, heads, MLA_NOPE + MLA_V)
    w_kn = w_kv[:, :, :MLA_NOPE].reshape(kv_rank, heads * MLA_NOPE).astype(BF16)
    w_v = w_kv[:, :, MLA_NOPE:].reshape(kv_rank, heads * MLA_V).astype(BF16)
    knt = _matmul_keys_t(ckv, w_kn, tq)
    v = _matmul([(ckv, w_v)], BF16)
    cosf, sinf = _rope_tables(seq)
    krt = _rope_k(proj, o_kr, cosf, sinf, seq, tq)
    out_a = _mla_attention(q, knt, krt, v, cosf, sinf, batch, seq, heads)
    params = _s5t_params(a_re, a_im, log_step, b_re, b_im, c_re, c_im, d_skip)
    g = _s5_toeplitz(proj, latent_w, params, batch, seq)
    out_b = _s5_glu(g, w_glu.astype(BF16))
    na = heads * MLA_V
    w_out_all, w_out_layer = w_out
    return _matmul([(out_a, w_out_all), (out_b, w_out_all)], F32, residual=h,
                   b_layer=w_out_layer, b_row0=[0, na])


def _odd_layer(h, ln_g, w_in, w_out, conv_w, a_log, dt_bias, norm_g, batch, seq):
    heads = a_log.shape[0]
    qkv_w = heads * (2 * GDN_DK + GDN_DV)
    o2 = qkv_w + 2 * heads
    o3 = o2 + heads * GDN_DV
    hn = _rmsnorm(h, ln_g, BF16)
    w_t = jnp.swapaxes(w_in, 0, 1)
    w_main = jnp.concatenate([w_t[:qkv_w], w_t[o2:o3]], axis=0).astype(BF16)
    w_ab = jnp.pad(w_t[qkv_w:o2], ((0, LANE - (o2 - qkv_w)), (0, 0))).astype(BF16)
    sb_heads = (w_in.shape[1] - o3) // (3 * SB_D)
    nsq = sb_heads * SB_D
    w_sb_qv = jnp.concatenate([w_t[o3:o3 + nsq] * (SB_D ** -0.5 * LOG2E), w_t[o3 + 2 * nsq:]], axis=0).astype(BF16)
    w_sb_k = w_t[o3 + nsq:o3 + 2 * nsq].astype(BF16)
    main = _matmul([(hn, w_main)], F32, b_nk=True)
    ab = _matmul([(hn, w_ab)], F32, b_nk=True)
    sb_qv = _matmul([(hn, w_sb_qv)], BF16, b_nk=True)
    sb_kt = _matmul_keys_t(hn, w_sb_k, _sb_tiles(seq)[1], b_nk=True)
    conv8 = jnp.pad(conv_w.astype(F32), ((0, SUBLANE - conv_w.shape[0]), (0, 0)))
    out_c = _gdn_core(main, conv8, ab, main, qkv_w, a_log, dt_bias, norm_g, batch, seq, heads)
    out_d = _sb_attention(sb_qv, sb_kt, batch, seq, sb_heads)
    nc = heads * GDN_DV
    w_out_all, w_out_layer = w_out
    return _matmul([(out_c, w_out_all), (out_d, w_out_all)], F32, residual=h,
                   b_layer=w_out_layer, b_row0=[0, nc])


def kernel(x, ln_mix, ln_ffn, ln_final, ffn_w_up, ffn_conv, ffn_w_down, ev_w_in, ev_w_out, mla_q_norm,
           mla_kv_norm, mla_w_uq, mla_w_ukv, s5_a_re, s5_a_im, s5_log_step, s5_b_re, s5_b_im, s5_c_re,
           s5_c_im, s5_d, s5_w_glu, od_w_in, od_w_out, gdn_conv, gdn_a_log, gdn_dt_bias, gdn_norm):
    batch, seq, d = x.shape
    h = x.reshape(batch * seq, d).astype(F32)
    ffn_w_up = ffn_w_up.astype(F32)
    w_down_bf16 = ffn_w_down.astype(BF16)
    for layer in range(ln_mix.shape[0]):
        i = layer // 2
        if layer % 2 == 0:
            h = _even_layer(h, ln_mix[layer], ev_w_in[i], (ev_w_out.astype(F32), i), mla_q_norm[i], mla_kv_norm[i],
                            mla_w_uq[i], mla_w_ukv[i], s5_a_re[i], s5_a_im[i], s5_log_step[i], s5_b_re[i],
                            s5_b_im[i], s5_c_re[i], s5_c_im[i], s5_d[i], s5_w_glu[i], batch, seq)
        else:
            h = _odd_layer(h, ln_mix[layer], od_w_in[i], (od_w_out.astype(F32), i), gdn_conv[i], gdn_a_log[i],
                           gdn_dt_bias[i], gdn_norm[i], batch, seq)
        h = _ffn(h, ln_ffn[layer], ffn_w_up, layer, ffn_conv[layer], w_down_bf16, seq)
    return _rmsnorm(h, ln_final, x.dtype).reshape(batch, seq, d)
```

```python
import functools
import math

import jax
import jax.numpy as jnp
from jax import lax
from jax.experimental import pallas as pl
from jax.experimental.pallas import tpu as pltpu

F32 = jnp.float32
BF16 = jnp.bfloat16
HI = lax.Precision.HIGHEST

CHUNK = 64
NORM_EPS = 1e-6
MLA_NOPE = 128
MLA_ROPE = 64
MLA_V = 128
MLA_QK = 256
ROPE_BASE = 10000.0
S5_GROUP = 16
S5_STATE = 64
GDN_DK = 128
GDN_DV = 128
SB_D = 128

LANE = 128
SUBLANE = 8
VMEM_LIMIT_MAX = 58 * 1024 * 1024
VMEM_LIMIT_MIN = 32 * 1024 * 1024
MASK_VALUE = -1e30
LOG2E = math.log2(math.e)

RMSNORM_ROWS = 256
MATMUL_TILES = ((1024, 1024), (1024, 512), (512, 512), (512, 256), (256, 256), (256, 128), (128, 128))
MATMUL_VMEM_BUDGET = 52 * 1024 * 1024
KEYS_T_TILE = (1024, 512)
FFN_UP_TILE = (1024, 256)


def _cparams(sem, vmem_est):
    limit = int(min(max(vmem_est * 5 // 4 + (4 << 20), VMEM_LIMIT_MIN), VMEM_LIMIT_MAX))
    return pltpu.CompilerParams(dimension_semantics=sem, vmem_limit_bytes=limit)


def _pick(n, pref, mult):
    if n <= pref:
        return n
    t = (pref // mult) * mult
    while t >= mult:
        if n % t == 0:
            return t
        t -= mult
    return n


def _rmsnorm_kernel(x_ref, g_ref, o_ref):
    x = x_ref[...].astype(F32)
    ms = jnp.mean(x * x, axis=-1, keepdims=True)
    o_ref[...] = (x * lax.rsqrt(ms + NORM_EPS) * g_ref[...]).astype(o_ref.dtype)


def _rmsnorm(x, g, out_dtype, *, col0=0, width=None):
    m = x.shape[0]
    width = x.shape[1] if width is None else width
    tm = _pick(m, RMSNORM_ROWS, SUBLANE)
    cb = col0 // width
    est = 2 * tm * width * (x.dtype.itemsize + jnp.dtype(out_dtype).itemsize) + 4 * tm * width * 4
    return pl.pallas_call(
        _rmsnorm_kernel,
        grid=(m // tm,),
        in_specs=[pl.BlockSpec((tm, width), lambda i: (i, cb)),
                  pl.BlockSpec((1, width), lambda i: (0, 0))],
        out_specs=pl.BlockSpec((tm, width), lambda i: (i, 0)),
        out_shape=jax.ShapeDtypeStruct((m, width), out_dtype),
        compiler_params=_cparams(("parallel",), est),
        name="rmsnorm",
    )(x, g.reshape(1, width).astype(F32))


def _mm_kernel(*refs, n_pairs, has_res, b_nk):
    o_ref = refs[-1]
    acc = None
    for p in range(n_pairs):
        a = refs[2 * p][...].astype(BF16)
        b = refs[2 * p + 1][...].astype(BF16)
        if b_nk:
            d = lax.dot_general(a, b, (((1,), (1,)), ((), ())), preferred_element_type=F32)
        else:
            d = jnp.dot(a, b, preferred_element_type=F32)
        acc = d if acc is None else acc + d
    if has_res:
        acc = acc + refs[2 * n_pairs][...]
    o_ref[...] = acc.astype(o_ref.dtype)


def _mm_tiles(m, ks, n, a_bytes, out_bytes, has_res, budget=MATMUL_VMEM_BUDGET, b_bytes=2):
    for bm_p, bn_p in MATMUL_TILES:
        bm = _pick(m, bm_p, SUBLANE)
        bn = _pick(n, bn_p, LANE)
        est = bm * bn * 4
        for k, ab in zip(ks, a_bytes):
            est += 2 * (bm * k * ab + k * bn * b_bytes) + (k * bn * 2 if b_bytes > 2 else 0)
        est += 2 * bm * bn * (out_bytes + (4 if has_res else 0))
        if est <= budget:
            return bm, bn, est
    return bm, bn, est


def _matmul(pairs, out_dtype, residual=None, *, a_col0=None, b_layer=None, b_row0=None, b_nk=False):
    m = pairs[0][0].shape[0]
    n = pairs[0][1].shape[0] if b_nk else pairs[0][1].shape[-1]
    if b_row0 is not None:
        ks = [a.shape[1] for a, _ in pairs]
    else:
        ks = [b.shape[1] if b_nk else b.shape[-2] for _, b in pairs]
        b_row0 = [0] * len(pairs)
    a_col0 = [0] * len(pairs) if a_col0 is None else a_col0
    a_bytes = [a.dtype.itemsize for a, _ in pairs]
    bm, bn, est = _mm_tiles(m, ks, n, a_bytes, jnp.dtype(out_dtype).itemsize, residual is not None,
                            b_bytes=pairs[0][1].dtype.itemsize)
    in_specs, args = [], []
    for (a, b), k, c0, r0 in zip(pairs, ks, a_col0, b_row0):
        cb = c0 // k
        rb = r0 // k
        in_specs.append(pl.BlockSpec((bm, k), lambda i, j, cb=cb: (i, cb)))
        if b_nk:
            in_specs.append(pl.BlockSpec((bn, k), lambda i, j: (j, 0)))
        elif b_layer is None:
            in_specs.append(pl.BlockSpec((k, bn), lambda i, j: (0, j)))
        else:
            in_specs.append(pl.BlockSpec((None, k, bn), lambda i, j, rb=rb: (b_layer, rb, j)))
        args += [a, b]
    if residual is not None:
        in_specs.append(pl.BlockSpec((bm, bn), lambda i, j: (i, j)))
        args.append(residual)
    return pl.pallas_call(
        functools.partial(_mm_kernel, n_pairs=len(pairs), has_res=residual is not None, b_nk=b_nk),
        grid=(m // bm, n // bn),
        in_specs=in_specs,
        out_specs=pl.BlockSpec((bm, bn), lambda i, j: (i, j)),
        out_shape=jax.ShapeDtypeStruct((m, n), out_dtype),
        compiler_params=_cparams(("parallel", "parallel"), est),
        name="matmul",
    )(*args)


def _mm_t_kernel(a_ref, b_ref, o_ref, acc_ref, *, tk, b_nk):
    a = a_ref[...].astype(BF16)
    if b_nk:
        acc_ref[...] = lax.dot_general(a, b_ref[...], (((1,), (1,)), ((), ())), preferred_element_type=F32)
    else:
        acc_ref[...] = jnp.dot(a, b_ref[...], preferred_element_type=F32)
    res_t = jnp.transpose(acc_ref[...])
    for c in range(o_ref.shape[0]):
        o_ref[c] = res_t[:, c * tk:(c + 1) * tk].astype(o_ref.dtype)


def _matmul_keys_t(a, b, tk, *, b_nk=False):
    m, k = a.shape
    n = b.shape[0] if b_nk else b.shape[1]
    bm = _pick(m, KEYS_T_TILE[0], tk)
    bn = _pick(n, KEYS_T_TILE[1], LANE)
    est = 2 * (bm * k * a.dtype.itemsize + k * bn * 2 + bm * bn * 2) + 3 * bm * bn * 4
    return pl.pallas_call(
        functools.partial(_mm_t_kernel, tk=tk, b_nk=b_nk),
        grid=(m // bm, n // bn),
        in_specs=[pl.BlockSpec((bm, k), lambda i, j: (i, 0)),
                  pl.BlockSpec((bn, k), lambda i, j: (j, 0)) if b_nk else
                  pl.BlockSpec((k, bn), lambda i, j: (0, j))],
        out_specs=pl.BlockSpec((bm // tk, bn, tk), lambda i, j: (i, j, 0)),
        out_shape=jax.ShapeDtypeStruct((m // tk, n, tk), BF16),
        scratch_shapes=[pltpu.VMEM((bm, bn), F32)],
        compiler_params=_cparams(("parallel", "parallel"), est),
        name="matmul_keys_t",
    )(a, b)


def _rope_tables(seq):
    half = MLA_ROPE // 2
    inv_freq = ROPE_BASE ** (-jnp.arange(half, dtype=F32) / half)
    ang = jnp.arange(seq, dtype=F32)[:, None] * inv_freq[None, :]
    cos, sin = jnp.cos(ang), jnp.sin(ang)
    zeros = jnp.zeros((seq, LANE - MLA_ROPE), F32)
    return (jnp.concatenate([cos, cos, zeros], axis=1),
            jnp.concatenate([-sin, sin, zeros], axis=1))


def _rope_lanes(x, cosf, sinf):
    half = MLA_ROPE // 2
    lane = lax.broadcasted_iota(jnp.int32, x.shape, 1)
    swapped = jnp.where(lane < half, pltpu.roll(x, LANE - half, 1), pltpu.roll(x, half, 1))
    return x * cosf + swapped * sinf


def _rope_kernel(x_ref, cos_ref, sin_ref, o_ref):
    o_ref[0] = jnp.transpose(_rope_lanes(x_ref[...], cos_ref[...], sin_ref[...])).astype(o_ref.dtype)


def _rope_k(x, col0, cosf, sinf, seq, tq):
    m = x.shape[0]
    per_seq = seq // tq
    cb = col0 // LANE
    return pl.pallas_call(
        _rope_kernel,
        grid=(m // tq,),
        in_specs=[pl.BlockSpec((tq, LANE), lambda i: (i, cb)),
                  pl.BlockSpec((tq, LANE), lambda i: (i % per_seq, 0)),
                  pl.BlockSpec((tq, LANE), lambda i: (i % per_seq, 0))],
        out_specs=pl.BlockSpec((1, LANE, tq), lambda i: (i, 0, 0)),
        out_shape=jax.ShapeDtypeStruct((m // tq, LANE, tq), BF16),
        compiler_params=_cparams(("parallel",), 16 * tq * LANE * 4),
        name="rope_k",
    )(x, cosf, sinf)


def _mla_kernel(q_ref, knt_ref, krt_ref, v_ref, cos_ref, sin_ref, o_ref, qs, m_s, l_s, acc_s, *, tq, hps):
    i = pl.program_id(2)
    cos, sin = cos_ref[...], sin_ref[...]
    for hh in range(hps):
        q = q_ref[:, hh * MLA_QK:(hh + 1) * MLA_QK]
        qs[hh, :, :MLA_NOPE] = q[:, :MLA_NOPE].astype(BF16)
        qs[hh, :, MLA_NOPE:] = _rope_lanes(q[:, MLA_NOPE:], cos, sin).astype(BF16)

    def vcols(hh):
        return slice(hh * MLA_V, (hh + 1) * MLA_V)

    def tile(hh, j):
        kt = jnp.concatenate([knt_ref[j, hh * MLA_NOPE:(hh + 1) * MLA_NOPE, :], krt_ref[j]], axis=0)
        s = jnp.dot(qs[hh], kt, preferred_element_type=F32)
        return s, v_ref[pl.ds(pl.multiple_of(j * tq, tq), tq), vcols(hh)]

    rc = lax.broadcasted_iota(jnp.int32, (tq, tq), 0) // CHUNK
    cc = lax.broadcasted_iota(jnp.int32, (tq, tq), 1) // CHUNK
    for hh in range(hps):
        s, v = tile(hh, i)
        s = jnp.where(cc <= rc, s, MASK_VALUE)
        m = jnp.max(s, axis=1, keepdims=True)
        p = jnp.exp2(s - m)
        m_s[hh] = m
        l_s[hh] = jnp.sum(p, axis=1, keepdims=True)
        acc_s[:, vcols(hh)] = jnp.dot(p.astype(BF16), v, preferred_element_type=F32)

    def body(j, carry):
        for hh in range(hps):
            s, v = tile(hh, j)
            m_prev = m_s[hh]
            m_new = jnp.maximum(m_prev, jnp.max(s, axis=1, keepdims=True))
            alpha = jnp.exp2(m_prev - m_new)
            p = jnp.exp2(s - m_new)
            l_s[hh] = alpha * l_s[hh] + jnp.sum(p, axis=1, keepdims=True)
            acc_s[:, vcols(hh)] = (alpha * acc_s[:, vcols(hh)]
                                   + jnp.dot(p.astype(BF16), v, preferred_element_type=F32))
            m_s[hh] = m_new
        return carry

    lax.fori_loop(0, i, body, 0)
    for hh in range(hps):
        o_ref[:, vcols(hh)] = (acc_s[:, vcols(hh)] / l_s[hh]).astype(o_ref.dtype)


MLA_TILE = 1024
MLA_HEADS_PER_STEP = 2


def _mla_tile(seq):
    return _pick(seq, MLA_TILE, CHUNK)


def _mla_attention(q, knt, krt, v, cosf, sinf, batch, seq, heads):
    tq = _mla_tile(seq)
    nq = seq // tq
    hps = MLA_HEADS_PER_STEP if heads % MLA_HEADS_PER_STEP == 0 else 1
    est = hps * (2 * tq * MLA_QK * 4 + 2 * seq * (MLA_QK + LANE) * 2 + 4 * tq * LANE * 4 + 2 * tq * LANE * 2
                 + tq * MLA_QK * 2 + 3 * tq * LANE * 4 + 3 * tq * tq * 4)
    return pl.pallas_call(
        functools.partial(_mla_kernel, tq=tq, hps=hps),
        grid=(batch, heads // hps, nq),
        in_specs=[pl.BlockSpec((tq, hps * MLA_QK), lambda b, h, i: (b * nq + i, h)),
                  pl.BlockSpec((nq, hps * MLA_NOPE, tq), lambda b, h, i: (b, h, 0)),
                  pl.BlockSpec((nq, LANE, tq), lambda b, h, i: (b, 0, 0)),
                  pl.BlockSpec((seq, hps * MLA_V), lambda b, h, i: (b, h)),
                  pl.BlockSpec((tq, LANE), lambda b, h, i: (i, 0)),
                  pl.BlockSpec((tq, LANE), lambda b, h, i: (i, 0))],
        out_specs=pl.BlockSpec((tq, hps * MLA_V), lambda b, h, i: (b * nq + i, h)),
        out_shape=jax.ShapeDtypeStruct((batch * seq, heads * MLA_V), BF16),
        scratch_shapes=[pltpu.VMEM((hps, tq, MLA_QK), BF16), pltpu.VMEM((hps, tq, 1), F32),
                        pltpu.VMEM((hps, tq, 1), F32), pltpu.VMEM((tq, hps * MLA_V), F32)],
        compiler_params=_cparams(("parallel", "parallel", "arbitrary"), est),
        name="mla_attention",
    )(q, knt, krt, v, cosf, sinf)


S5_BLOCK = 16


def _s5t_kernel(x_ref, perm_ref, toep_ref, ere_ref, eim_ref, fre_ref, fim_ref, coef_ref, d_ref, o_ref,
                u_ref, y_ref, er_s, ei_s, hr_s, hi_s, *, gps, nb, nb_seq):
    t_blk = S5_BLOCK
    tw = t_blk * S5_GROUP
    npair = gps // 2
    per_slab = LANE // S5_GROUP
    for hf in range(tw // LANE):
        xh = jnp.concatenate([x_ref[pl.ds(hf * per_slab + tt, nb, stride=t_blk), :].astype(BF16)
                              for tt in range(per_slab)], axis=1)
        uh = jnp.dot(xh, perm_ref[...], preferred_element_type=F32)
        for g in range(gps):
            u_ref[g, :, hf * LANE:(hf + 1) * LANE] = uh[:, g * LANE:(g + 1) * LANE].astype(BF16)
    win = lax.broadcasted_iota(jnp.int32, (nb, LANE), 1) // S5_GROUP
    in_win = [win == w for w in range(per_slab)]

    def place(pieces):
        out = None
        for w, (arr, src) in enumerate(pieces):
            sh = ((w - src) % per_slab) * S5_GROUP
            r = arr if sh == 0 else pltpu.roll(arr, sh, 1)
            out = r if out is None else jnp.where(in_win[w], r, out)
        return out

    for p in range(npair):
        up = jnp.concatenate([u_ref[2 * p], u_ref[2 * p + 1]], axis=1)
        er_s[p] = jnp.dot(up, ere_ref[p], preferred_element_type=F32)
        ei_s[p] = jnp.dot(up, eim_ref[p], preferred_element_type=F32)
    row = lax.broadcasted_iota(jnp.int32, (SUBLANE, LANE), 0)
    groups_per_seq = nb_seq // SUBLANE

    def body(r, carry):
        off = pl.multiple_of(r * SUBLANE, SUBLANE)
        keep = jnp.where((r % groups_per_seq) == 0, 0.0, 1.0)
        new = []
        for p in range(npair):
            xr = er_s[p, pl.ds(off, SUBLANE), :]
            xi = ei_s[p, pl.ds(off, SUBLANE), :]
            for idx, k in enumerate((1, 2, 4)):
                ar, ai = coef_ref[p, 2 * idx], coef_ref[p, 2 * idx + 1]
                sr = jnp.where(row >= k, pltpu.roll(xr, k, 0), 0.0)
                si = jnp.where(row >= k, pltpu.roll(xi, k, 0), 0.0)
                xr, xi = xr + ar * sr - ai * si, xi + ar * si + ai * sr
            cr = jnp.broadcast_to(carry[p][0] * keep, (SUBLANE, LANE))
            ci = jnp.broadcast_to(carry[p][1] * keep, (SUBLANE, LANE))
            pwr, pwi = coef_ref[p, 6], coef_ref[p, 7]
            xr, xi = xr + pwr * cr - pwi * ci, xi + pwr * ci + pwi * cr
            hr_s[p, pl.ds(off, SUBLANE), :] = jnp.where(row == 0, cr, pltpu.roll(xr, 1, 0))
            hi_s[p, pl.ds(off, SUBLANE), :] = jnp.where(row == 0, ci, pltpu.roll(xi, 1, 0))
            new.append((xr[SUBLANE - 1:, :], xi[SUBLANE - 1:, :]))
        return tuple(new)

    zero = jnp.zeros((1, LANE), F32)
    lax.fori_loop(0, nb // SUBLANE, body, tuple((zero, zero) for _ in range(npair)))
    for p in range(npair):
        yc = (jnp.dot(hr_s[p].astype(BF16), fre_ref[p], preferred_element_type=F32)
              + jnp.dot(hi_s[p].astype(BF16), fim_ref[p], preferred_element_type=F32))
        for a in range(2):
            g = 2 * p + a
            y_ref[g] = jnp.dot(u_ref[g], toep_ref[g], preferred_element_type=F32) + yc[:, a * tw:(a + 1) * tw]
    d = d_ref[...]
    for t in range(t_blk):
        hf, tt = divmod(t, per_slab)
        y_tok = place([(y_ref[g, :, hf * LANE:(hf + 1) * LANE], tt) for g in range(gps)])
        x_tok = x_ref[pl.ds(t, nb, stride=t_blk), :]
        o_ref[pl.ds(t, nb, stride=t_blk), :] = jax.nn.gelu(y_tok + d * x_tok, approximate=True)


def _s5t_params(a_re, a_im, log_step, b_re, b_im, c_re, c_im, d_skip):
    g, p = a_re.shape
    t = S5_BLOCK
    lam = lax.complex(a_re.astype(F32), a_im.astype(F32))
    lam_dt = lam * jnp.exp(log_step.astype(F32))[:, None]
    lam_bar = jnp.exp(lam_dt)
    b_bar = ((lam_bar - 1.0) / lam)[..., None] * lax.complex(b_re.astype(F32), b_im.astype(F32))
    c_c = lax.complex(c_re.astype(F32), c_im.astype(F32))
    tau = jnp.arange(t, dtype=F32)
    apow = jnp.exp(lam_dt[None] * tau[:, None, None])
    apow1 = jnp.exp(lam_dt[None] * (tau + 1.0)[:, None, None])
    kern = jnp.real(jnp.einsum('gip,tgp,gpj->gtij', c_c, apow, b_bar, precision=HI))
    lag = jnp.arange(t)[None, :] - jnp.arange(t)[:, None]
    shift = (lag[None] == jnp.arange(t)[:, None, None]).astype(F32)
    toep = jnp.einsum('gtij,tsu->gsjui', kern, shift, precision=HI)
    toep = toep.reshape(g, t * S5_GROUP, t * S5_GROUP)
    e_c = (apow[::-1][:, :, None, :] * jnp.moveaxis(b_bar, 2, 1)[None]).transpose(1, 0, 2, 3)
    e_c = e_c.reshape(g, t * S5_GROUP, p)
    f_c = (c_c[:, None] * apow1.transpose(1, 0, 2)[:, :, None, :])
    f_c = jnp.moveaxis(f_c.reshape(g, t * S5_GROUP, p), 1, 2)
    eye2 = jnp.eye(2, dtype=F32)

    def pair_in(x):
        return jnp.einsum('narp,ab->narbp', x.reshape(g // 2, 2, t * S5_GROUP, p), eye2).reshape(
            g // 2, 2 * t * S5_GROUP, 2 * p)

    def pair_out(x):
        return jnp.einsum('napc,ab->napbc', x.reshape(g // 2, 2, p, t * S5_GROUP), eye2).reshape(
            g // 2, 2 * p, 2 * t * S5_GROUP)

    def planes(z):
        z = jnp.broadcast_to(z, (g, SUBLANE, p)).reshape(g // 2, 2, SUBLANE, p).transpose(0, 2, 1, 3)
        z = z.reshape(g // 2, SUBLANE, 2 * p)
        return [jnp.real(z), jnp.imag(z)]

    coef = []
    for k in (1, 2, 4):
        coef += planes(jnp.exp(lam_dt * float(t * k))[:, None, :])
    coef += planes(jnp.exp(lam_dt[:, None, :] * (float(t) * jnp.arange(1, SUBLANE + 1, dtype=F32))[None, :, None]))
    per_slab = LANE // S5_GROUP
    src = jnp.arange(per_slab * LANE)
    s_step, s_grp, s_ch = src // LANE, (src % LANE) // S5_GROUP, src % S5_GROUP
    dst = s_grp * LANE + s_step * S5_GROUP + s_ch
    perm = (dst[:, None] == jnp.arange(per_slab * LANE)[None, :]).astype(BF16)
    return (perm, toep.astype(BF16), pair_in(jnp.real(e_c)).astype(BF16), pair_in(jnp.imag(e_c)).astype(BF16),
            pair_out(jnp.real(f_c)).astype(BF16), pair_out(-jnp.imag(f_c)).astype(BF16),
            jnp.stack(coef, axis=1), d_skip.astype(F32).reshape(1, g * S5_GROUP))


def _s5_toeplitz(u_src, col0, params, batch, seq):
    perm, toep, ere, eim, fre, fim, coef, d_row = params
    g = toep.shape[0]
    t = S5_BLOCK
    tw = t * S5_GROUP
    ns = 2 * S5_STATE
    m = u_src.shape[0]
    nb = m // t
    gps = LANE // S5_GROUP
    npair = gps // 2
    cb0 = col0 // LANE
    est = 4 * m * LANE * 4 + 2 * gps * nb * tw * 4 + 4 * npair * nb * ns * 4 + 16 * gps * tw * tw * 2 + 12 * nb * tw * 4
    return pl.pallas_call(
        functools.partial(_s5t_kernel, gps=gps, nb=nb, nb_seq=seq // t),
        grid=(g // gps,),
        in_specs=[pl.BlockSpec((m, LANE), lambda i: (0, cb0 + i)),
                  pl.BlockSpec(perm.shape, lambda i: (0, 0)),
                  pl.BlockSpec((gps, tw, tw), lambda i: (i, 0, 0)),
                  pl.BlockSpec((npair, 2 * tw, ns), lambda i: (i, 0, 0)),
                  pl.BlockSpec((npair, 2 * tw, ns), lambda i: (i, 0, 0)),
                  pl.BlockSpec((npair, ns, 2 * tw), lambda i: (i, 0, 0)),
                  pl.BlockSpec((npair, ns, 2 * tw), lambda i: (i, 0, 0)),
                  pl.BlockSpec((npair, 8, SUBLANE, ns), lambda i: (i, 0, 0, 0)),
                  pl.BlockSpec((1, LANE), lambda i: (0, i))],
        out_specs=pl.BlockSpec((m, LANE), lambda i: (0, i)),
        out_shape=jax.ShapeDtypeStruct((m, g * S5_GROUP), F32),
        scratch_shapes=([pltpu.VMEM((gps, nb, tw), BF16), pltpu.VMEM((gps, nb, tw), F32)]
                        + [pltpu.VMEM((npair, nb, ns), F32) for _ in range(4)]),
        compiler_params=_cparams(("parallel",), est),
        name="s5_blocks",
    )(u_src, perm, toep, ere, eim, fre, fim, coef, d_row)


def _glu_kernel(a_ref, w_ref, g_ref, o_ref):
    z = jnp.dot(a_ref[...].astype(BF16), w_ref[...], preferred_element_type=F32)
    o_ref[...] = (g_ref[...] * jax.nn.sigmoid(z)).astype(o_ref.dtype)


def _s5_glu(g, w_glu):
    m, k = g.shape
    bm, bn, est = _mm_tiles(m, [k], k, [4], 2, True)
    return pl.pallas_call(
        _glu_kernel,
        grid=(m // bm, k // bn),
        in_specs=[pl.BlockSpec((bm, k), lambda i, j: (i, 0)),
                  pl.BlockSpec((k, bn), lambda i, j: (0, j)),
                  pl.BlockSpec((bm, bn), lambda i, j: (i, j))],
        out_specs=pl.BlockSpec((bm, bn), lambda i, j: (i, j)),
        out_shape=jax.ShapeDtypeStruct((m, k), BF16),
        compiler_params=_cparams(("parallel", "parallel"), est),
        name="s5_glu",
    )(g, w_glu, g)


def _shift_rows(h, prev, k):
    r = pltpu.roll(h, k, 0)
    p = pltpu.roll(prev, k, 0)
    row = lax.broadcasted_iota(jnp.int32, prev.shape, 0)
    top = jnp.where(row < k, p, r[:SUBLANE])
    return jnp.concatenate([top, r[SUBLANE:]], axis=0)


def _ffn_up_kernel(a_ref, wa_ref, wb_ref, cwa_ref, cwb_ref, o_ref, hbuf, carry, *, tiles_per_seq, nj):
    i, j = pl.program_id(0), pl.program_id(1)
    slot = i % 2
    rows = hbuf.shape[1]

    @pl.when(jnp.logical_and(i == 0, j == 0))
    def _():
        hbuf[...] = jnp.zeros_like(hbuf)
        carry[...] = jnp.zeros_like(carry)

    def epilogue():
        jp = jnp.maximum(j - 1, 0)
        keep = jnp.where((i % tiles_per_seq) == 0, 0.0, 1.0)
        outs = []
        for s, cw_ref in enumerate((cwa_ref, cwb_ref)):
            h = hbuf[s]
            prev = carry[1 - slot, jp, s * SUBLANE:(s + 1) * SUBLANE, :] * keep
            cw = cw_ref[...]
            outs.append(cw[2:3] * h + cw[1:2] * _shift_rows(h, prev, 1) + cw[0:1] * _shift_rows(h, prev, 2))
        o_ref[...] = (jax.nn.silu(outs[0]) * outs[1]).astype(o_ref.dtype)

    @pl.when(j < nj)
    def _():
        epilogue()
        a = a_ref[...]
        ha = jnp.dot(a, wa_ref[...].astype(BF16), preferred_element_type=F32)
        hb = jnp.dot(a, wb_ref[...].astype(BF16), preferred_element_type=F32)
        hbuf[0] = ha
        hbuf[1] = hb
        carry[slot, j, 0:SUBLANE, :] = ha[rows - SUBLANE:, :]
        carry[slot, j, SUBLANE:2 * SUBLANE, :] = hb[rows - SUBLANE:, :]

    @pl.when(j == nj)
    def _():
        epilogue()


def _ffn_up(xn, w_up, layer, conv_w, seq):
    m, k = xn.shape
    f = w_up.shape[2] // 2
    bm = _pick(seq, FFN_UP_TILE[0], SUBLANE)
    bn = _pick(f, FFN_UP_TILE[1], LANE)
    nj = f // bn
    est = (2 * bm * k * 2 + 4 * k * bn * 4 + 2 * k * bn * 2 + 2 * bm * bn * 2 + 2 * nj * 16 * bn * 4
           + 2 * bm * bn * 4 + 12 * bm * bn * 4)
    cur = lambda j: jnp.minimum(j, nj - 1)
    prv = lambda j: jnp.maximum(j - 1, 0)
    return pl.pallas_call(
        functools.partial(_ffn_up_kernel, tiles_per_seq=seq // bm, nj=nj),
        grid=(m // bm, nj + 1),
        in_specs=[pl.BlockSpec((bm, k), lambda i, j: (i, 0)),
                  pl.BlockSpec((None, k, bn), lambda i, j: (layer, 0, cur(j))),
                  pl.BlockSpec((None, k, bn), lambda i, j: (layer, 0, cur(j) + nj)),
                  pl.BlockSpec((SUBLANE, bn), lambda i, j: (0, prv(j))),
                  pl.BlockSpec((SUBLANE, bn), lambda i, j: (0, prv(j) + nj))],
        out_specs=pl.BlockSpec((bm, bn), lambda i, j: (i, prv(j))),
        out_shape=jax.ShapeDtypeStruct((m, f), BF16),
        scratch_shapes=[pltpu.VMEM((2, bm, bn), F32), pltpu.VMEM((2, nj, 2 * SUBLANE, bn), F32)],
        compiler_params=_cparams(("arbitrary", "arbitrary"), est),
        name="ffn_up",
    )(xn, w_up, w_up, conv_w, conv_w)


GDN_ROWS = 256
GDN_HEADS_PER_STEP = 4
GDN_CONV = 4


def _bdot(a, b):
    return jnp.dot(a.astype(BF16), b.astype(BF16), preferred_element_type=F32)


def _bdot_nt(a, b):
    return lax.dot_general(a.astype(BF16), b.astype(BF16), (((1,), (1,)), ((), ())), preferred_element_type=F32)


def _bdot_tn(a, b):
    return lax.dot_general(a.astype(BF16), b.astype(BF16), (((0,), (0,)), ((), ())), preferred_element_type=F32)


def _gdn_head(alog, dtb, q, k, v, a_raw, b_raw, gate, ng, s, masks, rows):
    same, tril, tril_t, strict, eye, eye_f = masks
    a_raw = a_raw + dtb
    softplus = jnp.maximum(a_raw, 0.0) + jnp.log(1.0 + jnp.exp(-jnp.abs(a_raw)))
    g = -jnp.exp(jnp.zeros_like(a_raw) + alog) * softplus
    beta = jax.nn.sigmoid(b_raw)
    g_cols = jnp.broadcast_to(g, (rows, rows))
    g_rows = jnp.broadcast_to(jnp.sum(jnp.where(eye, g_cols, 0.0), axis=0, keepdims=True), (rows, rows))
    gc_col = jnp.sum(jnp.where(tril, g_rows, 0.0), axis=1, keepdims=True)
    gl_col = jnp.sum(jnp.where(same, g_rows, 0.0), axis=1, keepdims=True)
    gc_row = jnp.sum(jnp.where(tril_t, g_cols, 0.0), axis=0, keepdims=True)
    gc = jnp.broadcast_to(gc_col, (rows, LANE))
    gl = jnp.broadcast_to(gl_col, (rows, LANE))
    decay = jnp.exp(jnp.where(tril, gc_col - gc_row, MASK_VALUE))
    kb = k * beta
    vb = v * beta
    lmat = jnp.where(strict, _bdot_nt(kb, k) * decay, 0.0)
    tinv = eye_f - lmat
    pw = lmat
    for _ in range(int(math.log2(CHUNK)) - 1):
        pw = _bdot(pw, pw)
        tinv = tinv + _bdot(tinv, pw)
    eg = jnp.exp(gc)
    sol = _bdot(tinv, jnp.concatenate([vb, kb * eg], axis=1))
    u_c, w_c = sol[:, :GDN_DV], sol[:, GDN_DV:]
    intra = _bdot_nt(q, k) * decay
    q_dec = q * eg
    k_dec = k * jnp.exp(gl - gc)
    egl = jnp.exp(gl)
    iuw = _bdot(intra, sol)
    o_loc = iuw[:, :GDN_DV]
    q_eff = q_dec - iuw[:, GDN_DV:]
    outs = []
    for c in range(rows // CHUNK):
        lo, hi = c * CHUNK, (c + 1) * CHUNK
        kuw = _bdot_tn(k_dec[lo:hi], sol[lo:hi])
        outs.append(_bdot(q_eff[lo:hi], s) + o_loc[lo:hi])
        s = s * egl[lo:lo + 1, :] + kuw[:, :GDN_DV] - _bdot(kuw[:, GDN_DV:], s)
    o = jnp.concatenate(outs, axis=0)
    o = o * lax.rsqrt(jnp.mean(o * o, axis=-1, keepdims=True) + NORM_EPS) * ng
    return o * jax.nn.silu(gate), s


def _gdn_kernel(alog_ref, dtb_ref, q_ref, k_ref, v_ref, cq_ref, ck_ref, cv_ref, ab_ref, gate_ref, ng_ref,
                o_ref, state, hist, *, rows, hps, heads):
    hb = pl.program_id(1)
    r = pl.program_id(2)
    ab = ab_ref[...]
    ab_lane = lax.broadcasted_iota(jnp.int32, ab.shape, 1)

    @pl.when(r == 0)
    def _():
        state[...] = jnp.zeros_like(state)

    ri = lax.broadcasted_iota(jnp.int32, (rows, rows), 0)
    ci = lax.broadcasted_iota(jnp.int32, (rows, rows), 1)
    same = (ri // CHUNK) == (ci // CHUNK)
    tril = jnp.logical_and(same, ci <= ri)
    tril_t = jnp.logical_and(same, ri <= ci)
    strict = jnp.logical_and(same, ci < ri)
    eye = ri == ci
    masks = (same, tril, tril_t, strict, eye, jnp.where(eye, 1.0, 0.0))
    ng = ng_ref[...]

    @pl.when(r == 0)
    def _():
        hist[...] = jnp.zeros_like(hist)

    qkv = []
    for idx, (x_ref, w_ref) in enumerate(((q_ref, cq_ref), (k_ref, ck_ref), (v_ref, cv_ref))):
        x = x_ref[...]
        prev = hist[idx]
        hist[idx] = x[rows - SUBLANE:, :]
        w = w_ref[...]
        kw = GDN_CONV
        acc = w[kw - 1:kw] * x
        for sft in range(1, kw):
            acc = acc + w[kw - 1 - sft:kw - sft] * _shift_rows(x, prev, sft)
        qkv.append(jax.nn.silu(acc))

    def l2n(y):
        return y * lax.rsqrt(jnp.sum(y * y, axis=-1, keepdims=True) + NORM_EPS)

    for i in range(hps):
        cs = slice(i * GDN_DK, (i + 1) * GDN_DK)
        h = hb * hps + i
        a_raw = jnp.sum(jnp.where(ab_lane == h, ab, 0.0), axis=1, keepdims=True)
        b_raw = jnp.sum(jnp.where(ab_lane == heads + h, ab, 0.0), axis=1, keepdims=True)
        o, s = _gdn_head(alog_ref[h], dtb_ref[h], l2n(qkv[0][:, cs]) * (GDN_DK ** -0.5), l2n(qkv[1][:, cs]),
                         qkv[2][:, cs], a_raw, b_raw,
                         gate_ref[:, cs], ng, state[i], masks, rows)
        state[i] = s
        o_ref[:, cs] = o.astype(o_ref.dtype)


def _gdn_core(qkv, conv_w, ab, gate_src, gate_col0, a_log, dt_bias, norm_g, batch, seq, heads):
    rows = _pick(seq, GDN_ROWS, CHUNK)
    nr = seq // rows
    hps = GDN_HEADS_PER_STEP if heads % GDN_HEADS_PER_STEP == 0 else 1
    nhb = heads // hps
    wblk = hps * GDN_DK
    gcb = gate_col0 // wblk
    est = hps * (24 * rows * LANE * 4 + 16 * rows * rows * 4 + 8 * rows * LANE * 4)
    tok = lambda b, h, r: b * nr + r
    return pl.pallas_call(
        functools.partial(_gdn_kernel, rows=rows, hps=hps, heads=heads),
        grid=(batch, nhb, nr),
        in_specs=[pl.BlockSpec(memory_space=pltpu.SMEM),
                  pl.BlockSpec(memory_space=pltpu.SMEM),
                  pl.BlockSpec((rows, wblk), lambda b, h, r: (tok(b, h, r), h)),
                  pl.BlockSpec((rows, wblk), lambda b, h, r: (tok(b, h, r), nhb + h)),
                  pl.BlockSpec((rows, wblk), lambda b, h, r: (tok(b, h, r), 2 * nhb + h)),
                  pl.BlockSpec((SUBLANE, wblk), lambda b, h, r: (0, h)),
                  pl.BlockSpec((SUBLANE, wblk), lambda b, h, r: (0, nhb + h)),
                  pl.BlockSpec((SUBLANE, wblk), lambda b, h, r: (0, 2 * nhb + h)),
                  pl.BlockSpec((rows, LANE), lambda b, h, r: (tok(b, h, r), 0)),
                  pl.BlockSpec((rows, wblk), lambda b, h, r: (tok(b, h, r), gcb + h)),
                  pl.BlockSpec((1, GDN_DV), lambda b, h, r: (0, 0))],
        out_specs=pl.BlockSpec((rows, wblk), lambda b, h, r: (tok(b, h, r), h)),
        out_shape=jax.ShapeDtypeStruct((batch * seq, heads * GDN_DV), BF16),
        scratch_shapes=[pltpu.VMEM((hps, GDN_DK, GDN_DV), F32), pltpu.VMEM((3, SUBLANE, wblk), F32)],
        compiler_params=_cparams(("parallel", "parallel", "arbitrary"), est),
        name="gdn_core",
    )(a_log.astype(F32), dt_bias.astype(F32), qkv, qkv, qkv, conv_w, conv_w, conv_w, ab, gate_src,
      norm_g.reshape(1, GDN_DV).astype(F32))


SB_TQ = 512
SB_TK = 256
SB_HEADS_PER_STEP = 2


def _sb_kernel(q_ref, kt_ref, v_ref, o_ref, acc_s, c_s, *, tq, tk, hps):
    i = pl.program_id(2)
    r = tq // tk
    ur = lax.broadcasted_iota(jnp.int32, (2 * tk, tk), 0) % tk
    uc = lax.broadcasted_iota(jnp.int32, (2 * tk, tk), 1)
    upper2 = jnp.where(ur > uc, 1.0, 0.0).astype(BF16)
    ri = lax.broadcasted_iota(jnp.int32, (tq, tk), 0)
    ci = lax.broadcasted_iota(jnp.int32, (tq, tk), 1)
    acc_s[...] = jnp.zeros_like(acc_s)
    c_s[...] = jnp.zeros_like(c_s)

    def cols(hh):
        return slice(hh * SB_D, (hh + 1) * SB_D)

    def logits(hh, j):
        z = jnp.dot(q_ref[:, cols(hh)], kt_ref[j, cols(hh), :], preferred_element_type=F32)
        return z, jnp.maximum(z, 0.0) + jnp.log(1.0 + jnp.exp2(-jnp.abs(z))) * LOG2E

    def weights(z, lm, c):
        hi = lm.astype(BF16)
        lo = (lm - hi.astype(F32)).astype(BF16)
        rest = jnp.dot(jnp.concatenate([hi, lo], axis=1), upper2, preferred_element_type=F32)
        return jnp.exp2(z - lm - rest - c)

    def values(hh, j):
        return v_ref[pl.ds(pl.multiple_of(j * tk, tk), tk), cols(hh)]

    for d in range(r - 1, -1, -1):
        j = i * r + d
        before = (ci + d * tk) < ri
        for hh in range(hps):
            z, lm = logits(hh, j)
            lm = jnp.where(before, lm, 0.0)
            w = jnp.where(before, weights(z, lm, c_s[hh]), 0.0)
            acc_s[:, cols(hh)] += jnp.dot(w.astype(BF16), values(hh, j), preferred_element_type=F32)
            c_s[hh] += jnp.sum(lm, axis=1, keepdims=True)

    def single(j):
        for hh in range(hps):
            z, lm = logits(hh, j)
            w = weights(z, lm, c_s[hh])
            acc_s[:, cols(hh)] += jnp.dot(w.astype(BF16), values(hh, j), preferred_element_type=F32)
            c_s[hh] += jnp.sum(lm, axis=1, keepdims=True)

    def pair(p, carry):
        ja = i * r - 1 - 2 * p
        jb = ja - 1
        for hh in range(hps):
            za, lma = logits(hh, ja)
            zb, lmb = logits(hh, jb)
            c = c_s[hh]
            ta = jnp.sum(lma, axis=1, keepdims=True)
            wa = weights(za, lma, c)
            wb = weights(zb, lmb, c + ta)
            acc_s[:, cols(hh)] += (jnp.dot(wa.astype(BF16), values(hh, ja), preferred_element_type=F32)
                                   + jnp.dot(wb.astype(BF16), values(hh, jb), preferred_element_type=F32))
            c_s[hh] = c + ta + jnp.sum(lmb, axis=1, keepdims=True)
        return carry

    n_full = i * r
    lax.fori_loop(0, n_full // 2, pair, 0)
    if r % 2 == 1:
        @pl.when(n_full % 2 == 1)
        def _():
            single(0)
    o_ref[...] = acc_s[...].astype(o_ref.dtype)


def _sb_tiles(seq):
    tq = _pick(seq, SB_TQ, LANE)
    return tq, _pick(tq, SB_TK, LANE)


def _sb_attention(qv, kt, batch, seq, heads):
    tq, tk = _sb_tiles(seq)
    nq, nk = seq // tq, seq // tk
    hps = SB_HEADS_PER_STEP if heads % SB_HEADS_PER_STEP == 0 else 1
    nhb = heads // hps
    wblk = hps * SB_D
    est = hps * (4 * tq * SB_D * 2 + 4 * seq * SB_D * 2 + 2 * tq * SB_D * 4 + 24 * tq * tk * 4)
    return pl.pallas_call(
        functools.partial(_sb_kernel, tq=tq, tk=tk, hps=hps),
        grid=(batch, nhb, nq),
        in_specs=[pl.BlockSpec((tq, wblk), lambda b, h, i: (b * nq + i, h)),
                  pl.BlockSpec((nk, wblk, tk), lambda b, h, i: (b, h, 0)),
                  pl.BlockSpec((seq, wblk), lambda b, h, i: (b, nhb + h))],
        out_specs=pl.BlockSpec((tq, wblk), lambda b, h, i: (b * nq + i, h)),
        out_shape=jax.ShapeDtypeStruct((batch * seq, heads * SB_D), BF16),
        scratch_shapes=[pltpu.VMEM((tq, wblk), F32), pltpu.VMEM((hps, tq, 1), F32)],
        compiler_params=_cparams(("parallel", "parallel", "arbitrary"), est),
        name="sb_attention",
    )(qv, kt, qv)


def _pad_cols(w, n):
    return jnp.pad(w, ((0, 0), (0, n - w.shape[1])))


def _ffn(h, ln_g, w_up, layer, conv_w, w_down_bf16, seq):
    xn = _rmsnorm(h, ln_g, BF16)
    conv8 = jnp.pad(conv_w.astype(F32), ((0, SUBLANE - conv_w.shape[0]), (0, 0)))
    act = _ffn_up(xn, w_up, layer, conv8, seq)
    return _matmul([(act, w_down_bf16)], F32, residual=h, b_layer=layer)


def _even_layer(h, ln_g, w_in, w_out, q_norm, kv_norm, w_uq, w_ukv,
                a_re, a_im, log_step, b_re, b_im, c_re, c_im, d_skip, w_glu, batch, seq):
    d = h.shape[1]
    q_rank, kv_rank = q_norm.shape[0], kv_norm.shape[0]
    heads = w_uq.shape[1] // (MLA_NOPE + MLA_ROPE)
    s5_width = d_skip.shape[0]
    o_kr = q_rank + kv_rank
    o_s5 = o_kr + MLA_ROPE
    latent_w = -(-(o_kr + LANE) // s5_width) * s5_width if s5_width >= LANE else o_kr + LANE
    w_lat = _pad_cols(w_in[:, :o_s5], latent_w)
    w_cat = jnp.concatenate([w_lat, w_in[:, o_s5:]], axis=1).astype(BF16)
    hn = _rmsnorm(h, ln_g, BF16)
    proj = _matmul([(hn, w_cat)], F32)
    cq = _rmsnorm(proj, q_norm, BF16, col0=0, width=q_rank)
    ckv = _rmsnorm(proj, kv_norm, BF16, col0=q_rank, width=kv_rank)
    q_scale = (MLA_NOPE + MLA_ROPE) ** -0.5 * LOG2E
    w_uq_p = jnp.pad(w_uq.reshape(q_rank, heads, MLA_NOPE + MLA_ROPE) * q_scale,
                     ((0, 0), (0, 0), (0, MLA_QK - MLA_NOPE - MLA_ROPE))).reshape(q_rank, heads * MLA_QK)
    q = _matmul([(cq, w_uq_p.astype(BF16))], F32)
    tq = _mla_tile(seq)
    w_kv = w_ukv.reshape(kv_rank, heads, MLA_NOPE + MLA_V)
    w_kn = w_kv[:, :, :MLA_NOPE].reshape(kv_rank, heads * MLA_NOPE).astype(BF16)
    w_v = w_kv[:, :, MLA_NOPE:].reshape(kv_rank, heads * MLA_V).astype(BF16)
    knt = _matmul_keys_t(ckv, w_kn, tq)
    v = _matmul([(ckv, w_v)], BF16)
    cosf, sinf = _rope_tables(seq)
    krt = _rope_k(proj, o_kr, cosf, sinf, seq, tq)
    out_a = _mla_attention(q, knt, krt, v, cosf, sinf, batch, seq, heads)
    params = _s5t_params(a_re, a_im, log_step, b_re, b_im, c_re, c_im, d_skip)
    g = _s5_toeplitz(proj, latent_w, params, batch, seq)
    out_b = _s5_glu(g, w_glu.astype(BF16))
    na = heads * MLA_V
    w_out_all, w_out_layer = w_out
    return _matmul([(out_a, w_out_all), (out_b, w_out_all)], F32, residual=h,
                   b_layer=w_out_layer, b_row0=[0, na])


def _odd_layer(h, ln_g, w_in, w_out, conv_w, a_log, dt_bias, norm_g, batch, seq):
    heads = a_log.shape[0]
    qkv_w = heads * (2 * GDN_DK + GDN_DV)
    o2 = qkv_w + 2 * heads
    o3 = o2 + heads * GDN_DV
    hn = _rmsnorm(h, ln_g, BF16)
    w_t = jnp.swapaxes(w_in, 0, 1)
    w_main = jnp.concatenate([w_t[:qkv_w], w_t[o2:o3]], axis=0).astype(BF16)
    w_ab = jnp.pad(w_t[qkv_w:o2], ((0, LANE - (o2 - qkv_w)), (0, 0))).astype(BF16)
    sb_heads = (w_in.shape[1] - o3) // (3 * SB_D)
    nsq = sb_heads * SB_D
    w_sb_qv = jnp.concatenate([w_t[o3:o3 + nsq] * (SB_D ** -0.5 * LOG2E), w_t[o3 + 2 * nsq:]], axis=0).astype(BF16)
    w_sb_k = w_t[o3 + nsq:o3 + 2 * nsq].astype(BF16)
    main = _matmul([(hn, w_main)], F32, b_nk=True)
    ab = _matmul([(hn, w_ab)], F32, b_nk=True)
    sb_qv = _matmul([(hn, w_sb_qv)], BF16, b_nk=True)
    sb_kt = _matmul_keys_t(hn, w_sb_k, _sb_tiles(seq)[1], b_nk=True)
    conv8 = jnp.pad(conv_w.astype(F32), ((0, SUBLANE - conv_w.shape[0]), (0, 0)))
    out_c = _gdn_core(main, conv8, ab, main, qkv_w, a_log, dt_bias, norm_g, batch, seq, heads)
    out_d = _sb_attention(sb_qv, sb_kt, batch, seq, sb_heads)
    nc = heads * GDN_DV
    w_out_all, w_out_layer = w_out
    return _matmul([(out_c, w_out_all), (out_d, w_out_all)], F32, residual=h,
                   b_layer=w_out_layer, b_row0=[0, nc])


def kernel(x, ln_mix, ln_ffn, ln_final, ffn_w_up, ffn_conv, ffn_w_down, ev_w_in, ev_w_out, mla_q_norm,
           mla_kv_norm, mla_w_uq, mla_w_ukv, s5_a_re, s5_a_im, s5_log_step, s5_b_re, s5_b_im, s5_c_re,
           s5_c_im, s5_d, s5_w_glu, od_w_in, od_w_out, gdn_conv, gdn_a_log, gdn_dt_bias, gdn_norm):
    batch, seq, d = x.shape
    h = x.reshape(batch * seq, d).astype(F32)
    ffn_w_up = ffn_w_up.astype(F32)
    w_down_bf16 = ffn_w_down.astype(BF16)
    for layer in range(ln_mix.shape[0]):
        i = layer // 2
        if layer % 2 == 0:
            h = _even_layer(h, ln_mix[layer], ev_w_in[i], (ev_w_out.astype(F32), i), mla_q_norm[i], mla_kv_norm[i],
                            mla_w_uq[i], mla_w_ukv[i], s5_a_re[i], s5_a_im[i], s5_log_step[i], s5_b_re[i],
                            s5_b_im[i], s5_c_re[i], s5_c_im[i], s5_d[i], s5_w_glu[i], batch, seq)
        else:
            h = _odd_layer(h, ln_mix[layer], od_w_in[i], (od_w_out.astype(F32), i), gdn_conv[i], gdn_a_log[i],
                           gdn_dt_bias[i], gdn_norm[i], batch, seq)
        h = _ffn(h, ln_ffn[layer], ffn_w_up, layer, ffn_conv[layer], w_down_bf16, seq)
    return _rmsnorm(h, ln_final, x.dtype).reshape(batch, seq, d)
```

```python
import functools
import math

import jax
import jax.numpy as jnp
from jax import lax
from jax.experimental import pallas as pl
from jax.experimental.pallas import tpu as pltpu

F32 = jnp.float32
BF16 = jnp.bfloat16
HI = lax.Precision.HIGHEST

CHUNK = 64
NORM_EPS = 1e-6
MLA_NOPE = 128
MLA_ROPE = 64
MLA_V = 128
MLA_QK = 256
ROPE_BASE = 10000.0
S5_GROUP = 16
S5_STATE = 64
GDN_DK = 128
GDN_DV = 128
SB_D = 128

LANE = 128
SUBLANE = 8
VMEM_LIMIT_MAX = 58 * 1024 * 1024
VMEM_LIMIT_MIN = 32 * 1024 * 1024
MASK_VALUE = -1e30
LOG2E = math.log2(math.e)

RMSNORM_ROWS = 512
MATMUL_TILES = ((1024, 1024), (1024, 512), (512, 512), (512, 256), (256, 256), (256, 128), (128, 128))
MATMUL_VMEM_BUDGET = 52 * 1024 * 1024
KEYS_T_TILE = (1024, 512)
FFN_UP_TILE = (1024, 256)


def _cparams(sem, vmem_est):
    limit = int(min(max(vmem_est * 5 // 4 + (4 << 20), VMEM_LIMIT_MIN), VMEM_LIMIT_MAX))
    return pltpu.CompilerParams(dimension_semantics=sem, vmem_limit_bytes=limit)


def _pick(n, pref, mult):
    if n <= pref:
        return n
    t = (pref // mult) * mult
    while t >= mult:
        if n % t == 0:
            return t
        t -= mult
    return n


def _rmsnorm_kernel(x_ref, g_ref, o_ref):
    x = x_ref[...].astype(F32)
    ms = jnp.mean(x * x, axis=-1, keepdims=True)
    o_ref[...] = (x * lax.rsqrt(ms + NORM_EPS) * g_ref[...]).astype(o_ref.dtype)


def _rmsnorm(x, g, out_dtype, *, col0=0, width=None):
    m = x.shape[0]
    width = x.shape[1] if width is None else width
    tm = _pick(m, RMSNORM_ROWS, SUBLANE)
    cb = col0 // width
    est = 2 * tm * width * (x.dtype.itemsize + jnp.dtype(out_dtype).itemsize) + 4 * tm * width * 4
    return pl.pallas_call(
        _rmsnorm_kernel,
        grid=(m // tm,),
        in_specs=[pl.BlockSpec((tm, width), lambda i: (i, cb)),
                  pl.BlockSpec((1, width), lambda i: (0, 0))],
        out_specs=pl.BlockSpec((tm, width), lambda i: (i, 0)),
        out_shape=jax.ShapeDtypeStruct((m, width), out_dtype),
        compiler_params=_cparams(("parallel",), est),
        name="rmsnorm",
    )(x, g.reshape(1, width).astype(F32))


def _mm_kernel(*refs, n_pairs, has_res, b_nk):
    o_ref = refs[-1]
    acc = None
    for p in range(n_pairs):
        a = refs[2 * p][...].astype(BF16)
        b = refs[2 * p + 1][...].astype(BF16)
        if b_nk:
            d = lax.dot_general(a, b, (((1,), (1,)), ((), ())), preferred_element_type=F32)
        else:
            d = jnp.dot(a, b, preferred_element_type=F32)
        acc = d if acc is None else acc + d
    if has_res:
        acc = acc + refs[2 * n_pairs][...]
    o_ref[...] = acc.astype(o_ref.dtype)


def _mm_tiles(m, ks, n, a_bytes, out_bytes, has_res, budget=MATMUL_VMEM_BUDGET, b_bytes=2):
    for bm_p, bn_p in MATMUL_TILES:
        bm = _pick(m, bm_p, SUBLANE)
        bn = _pick(n, bn_p, LANE)
        est = bm * bn * 4
        for k, ab in zip(ks, a_bytes):
            est += 2 * (bm * k * ab + k * bn * b_bytes) + (k * bn * 2 if b_bytes > 2 else 0)
        est += 2 * bm * bn * (out_bytes + (4 if has_res else 0))
        if est <= budget:
            return bm, bn, est
    return bm, bn, est


def _matmul(pairs, out_dtype, residual=None, *, a_col0=None, b_layer=None, b_row0=None, b_nk=False):
    m = pairs[0][0].shape[0]
    n = pairs[0][1].shape[0] if b_nk else pairs[0][1].shape[-1]
    if b_row0 is not None:
        ks = [a.shape[1] for a, _ in pairs]
    else:
        ks = [b.shape[1] if b_nk else b.shape[-2] for _, b in pairs]
        b_row0 = [0] * len(pairs)
    a_col0 = [0] * len(pairs) if a_col0 is None else a_col0
    a_bytes = [a.dtype.itemsize for a, _ in pairs]
    bm, bn, est = _mm_tiles(m, ks, n, a_bytes, jnp.dtype(out_dtype).itemsize, residual is not None,
                            b_bytes=pairs[0][1].dtype.itemsize)
    in_specs, args = [], []
    for (a, b), k, c0, r0 in zip(pairs, ks, a_col0, b_row0):
        cb = c0 // k
        rb = r0 // k
        in_specs.append(pl.BlockSpec((bm, k), lambda i, j, cb=cb: (i, cb)))
        if b_nk:
            in_specs.append(pl.BlockSpec((bn, k), lambda i, j: (j, 0)))
        elif b_layer is None:
            in_specs.append(pl.BlockSpec((k, bn), lambda i, j: (0, j)))
        else:
            in_specs.append(pl.BlockSpec((None, k, bn), lambda i, j, rb=rb: (b_layer, rb, j)))
        args += [a, b]
    if residual is not None:
        in_specs.append(pl.BlockSpec((bm, bn), lambda i, j: (i, j)))
        args.append(residual)
    return pl.pallas_call(
        functools.partial(_mm_kernel, n_pairs=len(pairs), has_res=residual is not None, b_nk=b_nk),
        grid=(m // bm, n // bn),
        in_specs=in_specs,
        out_specs=pl.BlockSpec((bm, bn), lambda i, j: (i, j)),
        out_shape=jax.ShapeDtypeStruct((m, n), out_dtype),
        compiler_params=_cparams(("parallel", "parallel"), est),
        name="matmul",
    )(*args)


def _mm_t_kernel(a_ref, b_ref, o_ref, acc_ref, *, tk, b_nk):
    a = a_ref[...].astype(BF16)
    if b_nk:
        acc_ref[...] = lax.dot_general(a, b_ref[...], (((1,), (1,)), ((), ())), preferred_element_type=F32)
    else:
        acc_ref[...] = jnp.dot(a, b_ref[...], preferred_element_type=F32)
    res_t = jnp.transpose(acc_ref[...])
    for c in range(o_ref.shape[0]):
        o_ref[c] = res_t[:, c * tk:(c + 1) * tk].astype(o_ref.dtype)


def _matmul_keys_t(a, b, tk, *, b_nk=False):
    m, k = a.shape
    n = b.shape[0] if b_nk else b.shape[1]
    bm = _pick(m, KEYS_T_TILE[0], tk)
    bn = _pick(n, KEYS_T_TILE[1], LANE)
    est = 2 * (bm * k * a.dtype.itemsize + k * bn * 2 + bm * bn * 2) + 3 * bm * bn * 4
    return pl.pallas_call(
        functools.partial(_mm_t_kernel, tk=tk, b_nk=b_nk),
        grid=(m // bm, n // bn),
        in_specs=[pl.BlockSpec((bm, k), lambda i, j: (i, 0)),
                  pl.BlockSpec((bn, k), lambda i, j: (j, 0)) if b_nk else
                  pl.BlockSpec((k, bn), lambda i, j: (0, j))],
        out_specs=pl.BlockSpec((bm // tk, bn, tk), lambda i, j: (i, j, 0)),
        out_shape=jax.ShapeDtypeStruct((m // tk, n, tk), BF16),
        scratch_shapes=[pltpu.VMEM((bm, bn), F32)],
        compiler_params=_cparams(("parallel", "parallel"), est),
        name="matmul_keys_t",
    )(a, b)


def _rope_tables(seq):
    half = MLA_ROPE // 2
    inv_freq = ROPE_BASE ** (-jnp.arange(half, dtype=F32) / half)
    ang = jnp.arange(seq, dtype=F32)[:, None] * inv_freq[None, :]
    cos, sin = jnp.cos(ang), jnp.sin(ang)
    zeros = jnp.zeros((seq, LANE - MLA_ROPE), F32)
    return (jnp.concatenate([cos, cos, zeros], axis=1),
            jnp.concatenate([-sin, sin, zeros], axis=1))


def _rope_lanes(x, cosf, sinf):
    half = MLA_ROPE // 2
    lane = lax.broadcasted_iota(jnp.int32, x.shape, 1)
    swapped = jnp.where(lane < half, pltpu.roll(x, LANE - half, 1), pltpu.roll(x, half, 1))
    return x * cosf + swapped * sinf


def _rope_kernel(x_ref, cos_ref, sin_ref, o_ref):
    o_ref[0] = jnp.transpose(_rope_lanes(x_ref[...], cos_ref[...], sin_ref[...])).astype(o_ref.dtype)


def _rope_k(x, col0, cosf, sinf, seq, tq):
    m = x.shape[0]
    per_seq = seq // tq
    cb = col0 // LANE
    return pl.pallas_call(
        _rope_kernel,
        grid=(m // tq,),
        in_specs=[pl.BlockSpec((tq, LANE), lambda i: (i, cb)),
                  pl.BlockSpec((tq, LANE), lambda i: (i % per_seq, 0)),
                  pl.BlockSpec((tq, LANE), lambda i: (i % per_seq, 0))],
        out_specs=pl.BlockSpec((1, LANE, tq), lambda i: (i, 0, 0)),
        out_shape=jax.ShapeDtypeStruct((m // tq, LANE, tq), BF16),
        compiler_params=_cparams(("parallel",), 16 * tq * LANE * 4),
        name="rope_k",
    )(x, cosf, sinf)


def _mla_kernel(q_ref, knt_ref, krt_ref, v_ref, cos_ref, sin_ref, o_ref, qs, m_s, l_s, acc_s, *, tq, hps):
    i = pl.program_id(2)
    cos, sin = cos_ref[...], sin_ref[...]
    for hh in range(hps):
        q = q_ref[:, hh * MLA_QK:(hh + 1) * MLA_QK]
        qs[hh, :, :MLA_NOPE] = q[:, :MLA_NOPE].astype(BF16)
        qs[hh, :, MLA_NOPE:] = _rope_lanes(q[:, MLA_NOPE:], cos, sin).astype(BF16)

    def vcols(hh):
        return slice(hh * MLA_V, (hh + 1) * MLA_V)

    def tile(hh, j):
        kt = jnp.concatenate([knt_ref[j, hh * MLA_NOPE:(hh + 1) * MLA_NOPE, :], krt_ref[j]], axis=0)
        s = jnp.dot(qs[hh], kt, preferred_element_type=F32)
        return s, v_ref[pl.ds(pl.multiple_of(j * tq, tq), tq), vcols(hh)]

    rc = lax.broadcasted_iota(jnp.int32, (tq, tq), 0) // CHUNK
    cc = lax.broadcasted_iota(jnp.int32, (tq, tq), 1) // CHUNK
    for hh in range(hps):
        s, v = tile(hh, i)
        s = jnp.where(cc <= rc, s, MASK_VALUE)
        m = jnp.max(s, axis=1, keepdims=True)
        p = jnp.exp2(s - m)
        m_s[hh] = m
        l_s[hh] = jnp.sum(p, axis=1, keepdims=True)
        acc_s[:, vcols(hh)] = jnp.dot(p.astype(BF16), v, preferred_element_type=F32)

    def body(j, carry):
        for hh in range(hps):
            s, v = tile(hh, j)
            m_prev = m_s[hh]
            m_new = jnp.maximum(m_prev, jnp.max(s, axis=1, keepdims=True))
            alpha = jnp.exp2(m_prev - m_new)
            p = jnp.exp2(s - m_new)
            l_s[hh] = alpha * l_s[hh] + jnp.sum(p, axis=1, keepdims=True)
            acc_s[:, vcols(hh)] = (alpha * acc_s[:, vcols(hh)]
                                   + jnp.dot(p.astype(BF16), v, preferred_element_type=F32))
            m_s[hh] = m_new
        return carry

    lax.fori_loop(0, i, body, 0)
    for hh in range(hps):
        o_ref[:, vcols(hh)] = (acc_s[:, vcols(hh)] / l_s[hh]).astype(o_ref.dtype)


MLA_TILE = 1024
MLA_HEADS_PER_STEP = 2


def _mla_tile(seq):
    return _pick(seq, MLA_TILE, CHUNK)


def _mla_attention(q, knt, krt, v, cosf, sinf, batch, seq, heads):
    tq = _mla_tile(seq)
    nq = seq // tq
    hps = MLA_HEADS_PER_STEP if heads % MLA_HEADS_PER_STEP == 0 else 1
    est = hps * (2 * tq * MLA_QK * 4 + 2 * seq * (MLA_QK + LANE) * 2 + 4 * tq * LANE * 4 + 2 * tq * LANE * 2
                 + tq * MLA_QK * 2 + 3 * tq * LANE * 4 + 3 * tq * tq * 4)
    return pl.pallas_call(
        functools.partial(_mla_kernel, tq=tq, hps=hps),
        grid=(batch, heads // hps, nq),
        in_specs=[pl.BlockSpec((tq, hps * MLA_QK), lambda b, h, i: (b * nq + i, h)),
                  pl.BlockSpec((nq, hps * MLA_NOPE, tq), lambda b, h, i: (b, h, 0)),
                  pl.BlockSpec((nq, LANE, tq), lambda b, h, i: (b, 0, 0)),
                  pl.BlockSpec((seq, hps * MLA_V), lambda b, h, i: (b, h)),
                  pl.BlockSpec((tq, LANE), lambda b, h, i: (i, 0)),
                  pl.BlockSpec((tq, LANE), lambda b, h, i: (i, 0))],
        out_specs=pl.BlockSpec((tq, hps * MLA_V), lambda b, h, i: (b * nq + i, h)),
        out_shape=jax.ShapeDtypeStruct((batch * seq, heads * MLA_V), BF16),
        scratch_shapes=[pltpu.VMEM((hps, tq, MLA_QK), BF16), pltpu.VMEM((hps, tq, 1), F32),
                        pltpu.VMEM((hps, tq, 1), F32), pltpu.VMEM((tq, hps * MLA_V), F32)],
        compiler_params=_cparams(("parallel", "parallel", "arbitrary"), est),
        name="mla_attention",
    )(q, knt, krt, v, cosf, sinf)


S5_BLOCK = 16


def _s5t_kernel(x_ref, perm_ref, toep_ref, ere_ref, eim_ref, fre_ref, fim_ref, coef_ref, d_ref, o_ref,
                u_ref, y_ref, er_s, ei_s, hr_s, hi_s, *, gps, nb, nb_seq):
    t_blk = S5_BLOCK
    tw = t_blk * S5_GROUP
    npair = gps // 2
    per_slab = LANE // S5_GROUP
    for hf in range(tw // LANE):
        xh = jnp.concatenate([x_ref[pl.ds(hf * per_slab + tt, nb, stride=t_blk), :].astype(BF16)
                              for tt in range(per_slab)], axis=1)
        uh = jnp.dot(xh, perm_ref[...], preferred_element_type=F32)
        for g in range(gps):
            u_ref[g, :, hf * LANE:(hf + 1) * LANE] = uh[:, g * LANE:(g + 1) * LANE].astype(BF16)
    win = lax.broadcasted_iota(jnp.int32, (nb, LANE), 1) // S5_GROUP
    in_win = [win == w for w in range(per_slab)]

    def place(pieces):
        out = None
        for w, (arr, src) in enumerate(pieces):
            sh = ((w - src) % per_slab) * S5_GROUP
            r = arr if sh == 0 else pltpu.roll(arr, sh, 1)
            out = r if out is None else jnp.where(in_win[w], r, out)
        return out

    for p in range(npair):
        up = jnp.concatenate([u_ref[2 * p], u_ref[2 * p + 1]], axis=1)
        er_s[p] = jnp.dot(up, ere_ref[p], preferred_element_type=F32)
        ei_s[p] = jnp.dot(up, eim_ref[p], preferred_element_type=F32)
    row = lax.broadcasted_iota(jnp.int32, (SUBLANE, LANE), 0)
    groups_per_seq = nb_seq // SUBLANE

    def body(r, carry):
        off = pl.multiple_of(r * SUBLANE, SUBLANE)
        keep = jnp.where((r % groups_per_seq) == 0, 0.0, 1.0)
        new = []
        for p in range(npair):
            xr = er_s[p, pl.ds(off, SUBLANE), :]
            xi = ei_s[p, pl.ds(off, SUBLANE), :]
            for idx, k in enumerate((1, 2, 4)):
                ar, ai = coef_ref[p, 2 * idx], coef_ref[p, 2 * idx + 1]
                sr = jnp.where(row >= k, pltpu.roll(xr, k, 0), 0.0)
                si = jnp.where(row >= k, pltpu.roll(xi, k, 0), 0.0)
                xr, xi = xr + ar * sr - ai * si, xi + ar * si + ai * sr
            cr = jnp.broadcast_to(carry[p][0] * keep, (SUBLANE, LANE))
            ci = jnp.broadcast_to(carry[p][1] * keep, (SUBLANE, LANE))
            pwr, pwi = coef_ref[p, 6], coef_ref[p, 7]
            xr, xi = xr + pwr * cr - pwi * ci, xi + pwr * ci + pwi * cr
            hr_s[p, pl.ds(off, SUBLANE), :] = jnp.where(row == 0, cr, pltpu.roll(xr, 1, 0))
            hi_s[p, pl.ds(off, SUBLANE), :] = jnp.where(row == 0, ci, pltpu.roll(xi, 1, 0))
            new.append((xr[SUBLANE - 1:, :], xi[SUBLANE - 1:, :]))
        return tuple(new)

    zero = jnp.zeros((1, LANE), F32)
    lax.fori_loop(0, nb // SUBLANE, body, tuple((zero, zero) for _ in range(npair)))
    for p in range(npair):
        yc = (jnp.dot(hr_s[p].astype(BF16), fre_ref[p], preferred_element_type=F32)
              + jnp.dot(hi_s[p].astype(BF16), fim_ref[p], preferred_element_type=F32))
        for a in range(2):
            g = 2 * p + a
            y_ref[g] = jnp.dot(u_ref[g], toep_ref[g], preferred_element_type=F32) + yc[:, a * tw:(a + 1) * tw]
    d = d_ref[...]
    for t in range(t_blk):
        hf, tt = divmod(t, per_slab)
        y_tok = place([(y_ref[g, :, hf * LANE:(hf + 1) * LANE], tt) for g in range(gps)])
        x_tok = x_ref[pl.ds(t, nb, stride=t_blk), :]
        o_ref[pl.ds(t, nb, stride=t_blk), :] = jax.nn.gelu(y_tok + d * x_tok, approximate=True)


def _s5t_params(a_re, a_im, log_step, b_re, b_im, c_re, c_im, d_skip):
    g, p = a_re.shape
    t = S5_BLOCK
    lam = lax.complex(a_re.astype(F32), a_im.astype(F32))
    lam_dt = lam * jnp.exp(log_step.astype(F32))[:, None]
    lam_bar = jnp.exp(lam_dt)
    b_bar = ((lam_bar - 1.0) / lam)[..., None] * lax.complex(b_re.astype(F32), b_im.astype(F32))
    c_c = lax.complex(c_re.astype(F32), c_im.astype(F32))
    tau = jnp.arange(t, dtype=F32)
    apow = jnp.exp(lam_dt[None] * tau[:, None, None])
    apow1 = jnp.exp(lam_dt[None] * (tau + 1.0)[:, None, None])
    kern = jnp.real(jnp.einsum('gip,tgp,gpj->gtij', c_c, apow, b_bar, precision=HI))
    lag = jnp.arange(t)[None, :] - jnp.arange(t)[:, None]
    shift = (lag[None] == jnp.arange(t)[:, None, None]).astype(F32)
    toep = jnp.einsum('gtij,tsu->gsjui', kern, shift, precision=HI)
    toep = toep.reshape(g, t * S5_GROUP, t * S5_GROUP)
    e_c = (apow[::-1][:, :, None, :] * jnp.moveaxis(b_bar, 2, 1)[None]).transpose(1, 0, 2, 3)
    e_c = e_c.reshape(g, t * S5_GROUP, p)
    f_c = (c_c[:, None] * apow1.transpose(1, 0, 2)[:, :, None, :])
    f_c = jnp.moveaxis(f_c.reshape(g, t * S5_GROUP, p), 1, 2)
    eye2 = jnp.eye(2, dtype=F32)

    def pair_in(x):
        return jnp.einsum('narp,ab->narbp', x.reshape(g // 2, 2, t * S5_GROUP, p), eye2).reshape(
            g // 2, 2 * t * S5_GROUP, 2 * p)

    def pair_out(x):
        return jnp.einsum('napc,ab->napbc', x.reshape(g // 2, 2, p, t * S5_GROUP), eye2).reshape(
            g // 2, 2 * p, 2 * t * S5_GROUP)

    def planes(z):
        z = jnp.broadcast_to(z, (g, SUBLANE, p)).reshape(g // 2, 2, SUBLANE, p).transpose(0, 2, 1, 3)
        z = z.reshape(g // 2, SUBLANE, 2 * p)
        return [jnp.real(z), jnp.imag(z)]

    coef = []
    for k in (1, 2, 4):
        coef += planes(jnp.exp(lam_dt * float(t * k))[:, None, :])
    coef += planes(jnp.exp(lam_dt[:, None, :] * (float(t) * jnp.arange(1, SUBLANE + 1, dtype=F32))[None, :, None]))
    per_slab = LANE // S5_GROUP
    src = jnp.arange(per_slab * LANE)
    s_step, s_grp, s_ch = src // LANE, (src % LANE) // S5_GROUP, src % S5_GROUP
    dst = s_grp * LANE + s_step * S5_GROUP + s_ch
    perm = (dst[:, None] == jnp.arange(per_slab * LANE)[None, :]).astype(BF16)
    return (perm, toep.astype(BF16), pair_in(jnp.real(e_c)).astype(BF16), pair_in(jnp.imag(e_c)).astype(BF16),
            pair_out(jnp.real(f_c)).astype(BF16), pair_out(-jnp.imag(f_c)).astype(BF16),
            jnp.stack(coef, axis=1), d_skip.astype(F32).reshape(1, g * S5_GROUP))


def _s5_toeplitz(u_src, col0, params, batch, seq):
    perm, toep, ere, eim, fre, fim, coef, d_row = params
    g = toep.shape[0]
    t = S5_BLOCK
    tw = t * S5_GROUP
    ns = 2 * S5_STATE
    m = u_src.shape[0]
    nb = m // t
    gps = LANE // S5_GROUP
    npair = gps // 2
    cb0 = col0 // LANE
    est = 4 * m * LANE * 4 + 2 * gps * nb * tw * 4 + 4 * npair * nb * ns * 4 + 16 * gps * tw * tw * 2 + 12 * nb * tw * 4
    return pl.pallas_call(
        functools.partial(_s5t_kernel, gps=gps, nb=nb, nb_seq=seq // t),
        grid=(g // gps,),
        in_specs=[pl.BlockSpec((m, LANE), lambda i: (0, cb0 + i)),
                  pl.BlockSpec(perm.shape, lambda i: (0, 0)),
                  pl.BlockSpec((gps, tw, tw), lambda i: (i, 0, 0)),
                  pl.BlockSpec((npair, 2 * tw, ns), lambda i: (i, 0, 0)),
                  pl.BlockSpec((npair, 2 * tw, ns), lambda i: (i, 0, 0)),
                  pl.BlockSpec((npair, ns, 2 * tw), lambda i: (i, 0, 0)),
                  pl.BlockSpec((npair, ns, 2 * tw), lambda i: (i, 0, 0)),
                  pl.BlockSpec((npair, 8, SUBLANE, ns), lambda i: (i, 0, 0, 0)),
                  pl.BlockSpec((1, LANE), lambda i: (0, i))],
        out_specs=pl.BlockSpec((m, LANE), lambda i: (0, i)),
        out_shape=jax.ShapeDtypeStruct((m, g * S5_GROUP), F32),
        scratch_shapes=([pltpu.VMEM((gps, nb, tw), BF16), pltpu.VMEM((gps, nb, tw), F32)]
                        + [pltpu.VMEM((npair, nb, ns), F32) for _ in range(4)]),
        compiler_params=_cparams(("parallel",), est),
        name="s5_blocks",
    )(u_src, perm, toep, ere, eim, fre, fim, coef, d_row)


def _glu_kernel(a_ref, w_ref, g_ref, o_ref):
    z = jnp.dot(a_ref[...].astype(BF16), w_ref[...], preferred_element_type=F32)
    o_ref[...] = (g_ref[...] * jax.nn.sigmoid(z)).astype(o_ref.dtype)


def _s5_glu(g, w_glu):
    m, k = g.shape
    bm, bn, est = _mm_tiles(m, [k], k, [4], 2, True)
    return pl.pallas_call(
        _glu_kernel,
        grid=(m // bm, k // bn),
        in_specs=[pl.BlockSpec((bm, k), lambda i, j: (i, 0)),
                  pl.BlockSpec((k, bn), lambda i, j: (0, j)),
                  pl.BlockSpec((bm, bn), lambda i, j: (i, j))],
        out_specs=pl.BlockSpec((bm, bn), lambda i, j: (i, j)),
        out_shape=jax.ShapeDtypeStruct((m, k), BF16),
        compiler_params=_cparams(("parallel", "parallel"), est),
        name="s5_glu",
    )(g, w_glu, g)


def _shift_rows(h, prev, k):
    r = pltpu.roll(h, k, 0)
    p = pltpu.roll(prev, k, 0)
    row = lax.broadcasted_iota(jnp.int32, prev.shape, 0)
    top = jnp.where(row < k, p, r[:SUBLANE])
    return jnp.concatenate([top, r[SUBLANE:]], axis=0)


def _ffn_up_kernel(a_ref, wa_ref, wb_ref, cwa_ref, cwb_ref, o_ref, hbuf, carry, *, tiles_per_seq, nj):
    i, j = pl.program_id(0), pl.program_id(1)
    slot = i % 2
    rows = hbuf.shape[1]

    @pl.when(jnp.logical_and(i == 0, j == 0))
    def _():
        hbuf[...] = jnp.zeros_like(hbuf)
        carry[...] = jnp.zeros_like(carry)

    def epilogue():
        jp = jnp.maximum(j - 1, 0)
        keep = jnp.where((i % tiles_per_seq) == 0, 0.0, 1.0)
        outs = []
        for s, cw_ref in enumerate((cwa_ref, cwb_ref)):
            h = hbuf[s]
            prev = carry[1 - slot, jp, s * SUBLANE:(s + 1) * SUBLANE, :] * keep
            cw = cw_ref[...]
            outs.append(cw[2:3] * h + cw[1:2] * _shift_rows(h, prev, 1) + cw[0:1] * _shift_rows(h, prev, 2))
        o_ref[...] = (jax.nn.silu(outs[0]) * outs[1]).astype(o_ref.dtype)

    @pl.when(j < nj)
    def _():
        epilogue()
        a = a_ref[...]
        ha = jnp.dot(a, wa_ref[...].astype(BF16), preferred_element_type=F32)
        hb = jnp.dot(a, wb_ref[...].astype(BF16), preferred_element_type=F32)
        hbuf[0] = ha
        hbuf[1] = hb
        carry[slot, j, 0:SUBLANE, :] = ha[rows - SUBLANE:, :]
        carry[slot, j, SUBLANE:2 * SUBLANE, :] = hb[rows - SUBLANE:, :]

    @pl.when(j == nj)
    def _():
        epilogue()


def _ffn_up(xn, w_up, layer, conv_w, seq):
    m, k = xn.shape
    f = w_up.shape[2] // 2
    bm = _pick(seq, FFN_UP_TILE[0], SUBLANE)
    bn = _pick(f, FFN_UP_TILE[1], LANE)
    nj = f // bn
    est = (2 * bm * k * 2 + 4 * k * bn * 4 + 2 * k * bn * 2 + 2 * bm * bn * 2 + 2 * nj * 16 * bn * 4
           + 2 * bm * bn * 4 + 12 * bm * bn * 4)
    cur = lambda j: jnp.minimum(j, nj - 1)
    prv = lambda j: jnp.maximum(j - 1, 0)
    return pl.pallas_call(
        functools.partial(_ffn_up_kernel, tiles_per_seq=seq // bm, nj=nj),
        grid=(m // bm, nj + 1),
        in_specs=[pl.BlockSpec((bm, k), lambda i, j: (i, 0)),
                  pl.BlockSpec((None, k, bn), lambda i, j: (layer, 0, cur(j))),
                  pl.BlockSpec((None, k, bn), lambda i, j: (layer, 0, cur(j) + nj)),
                  pl.BlockSpec((SUBLANE, bn), lambda i, j: (0, prv(j))),
                  pl.BlockSpec((SUBLANE, bn), lambda i, j: (0, prv(j) + nj))],
        out_specs=pl.BlockSpec((bm, bn), lambda i, j: (i, prv(j))),
        out_shape=jax.ShapeDtypeStruct((m, f), BF16),
        scratch_shapes=[pltpu.VMEM((2, bm, bn), F32), pltpu.VMEM((2, nj, 2 * SUBLANE, bn), F32)],
        compiler_params=_cparams(("arbitrary", "arbitrary"), est),
        name="ffn_up",
    )(xn, w_up, w_up, conv_w, conv_w)


GDN_ROWS = 256
GDN_HEADS_PER_STEP = 8
GDN_CONV = 4


def _bdot(a, b):
    return jnp.dot(a.astype(BF16), b.astype(BF16), preferred_element_type=F32)


def _bdot_nt(a, b):
    return lax.dot_general(a.astype(BF16), b.astype(BF16), (((1,), (1,)), ((), ())), preferred_element_type=F32)


def _bdot_tn(a, b):
    return lax.dot_general(a.astype(BF16), b.astype(BF16), (((0,), (0,)), ((), ())), preferred_element_type=F32)


def _gdn_head(alog, dtb, q, k, v, a_raw, b_raw, gate, ng, s, masks, rows):
    same, tril, tril_t, strict, eye, eye_f = masks
    a_raw = a_raw + dtb
    softplus = jnp.maximum(a_raw, 0.0) + jnp.log(1.0 + jnp.exp(-jnp.abs(a_raw)))
    g = -jnp.exp(jnp.zeros_like(a_raw) + alog) * softplus
    beta = jax.nn.sigmoid(b_raw)
    g_cols = jnp.broadcast_to(g, (rows, rows))
    g_rows = jnp.broadcast_to(jnp.sum(jnp.where(eye, g_cols, 0.0), axis=0, keepdims=True), (rows, rows))
    gc_col = jnp.sum(jnp.where(tril, g_rows, 0.0), axis=1, keepdims=True)
    gl_col = jnp.sum(jnp.where(same, g_rows, 0.0), axis=1, keepdims=True)
    gc_row = jnp.sum(jnp.where(tril_t, g_cols, 0.0), axis=0, keepdims=True)
    gc = jnp.broadcast_to(gc_col, (rows, LANE))
    gl = jnp.broadcast_to(gl_col, (rows, LANE))
    decay = jnp.exp(jnp.where(tril, gc_col - gc_row, MASK_VALUE))
    kb = k * beta
    vb = v * beta
    lmat = jnp.where(strict, _bdot_nt(kb, k) * decay, 0.0)
    tinv = eye_f - lmat
    pw = lmat
    for _ in range(int(math.log2(CHUNK)) - 1):
        pw = _bdot(pw, pw)
        tinv = tinv + _bdot(tinv, pw)
    eg = jnp.exp(gc)
    sol = _bdot(tinv, jnp.concatenate([vb, kb * eg], axis=1))
    u_c, w_c = sol[:, :GDN_DV], sol[:, GDN_DV:]
    intra = _bdot_nt(q, k) * decay
    q_dec = q * eg
    k_dec = k * jnp.exp(gl - gc)
    egl = jnp.exp(gl)
    iuw = _bdot(intra, sol)
    o_loc = iuw[:, :GDN_DV]
    q_eff = q_dec - iuw[:, GDN_DV:]
    outs = []
    for c in range(rows // CHUNK):
        lo, hi = c * CHUNK, (c + 1) * CHUNK
        kuw = _bdot_tn(k_dec[lo:hi], sol[lo:hi])
        outs.append(_bdot(q_eff[lo:hi], s) + o_loc[lo:hi])
        s = s * egl[lo:lo + 1, :] + kuw[:, :GDN_DV] - _bdot(kuw[:, GDN_DV:], s)
    o = jnp.concatenate(outs, axis=0)
    o = o * lax.rsqrt(jnp.mean(o * o, axis=-1, keepdims=True) + NORM_EPS) * ng
    return o * jax.nn.silu(gate), s


def _gdn_kernel(alog_ref, dtb_ref, q_ref, k_ref, v_ref, cq_ref, ck_ref, cv_ref, ab_ref, gate_ref, ng_ref,
                o_ref, state, hist, *, rows, hps, heads):
    hb = pl.program_id(1)
    r = pl.program_id(2)
    ab = ab_ref[...]
    ab_lane = lax.broadcasted_iota(jnp.int32, ab.shape, 1)

    @pl.when(r == 0)
    def _():
        state[...] = jnp.zeros_like(state)

    ri = lax.broadcasted_iota(jnp.int32, (rows, rows), 0)
    ci = lax.broadcasted_iota(jnp.int32, (rows, rows), 1)
    same = (ri // CHUNK) == (ci // CHUNK)
    tril = jnp.logical_and(same, ci <= ri)
    tril_t = jnp.logical_and(same, ri <= ci)
    strict = jnp.logical_and(same, ci < ri)
    eye = ri == ci
    masks = (same, tril, tril_t, strict, eye, jnp.where(eye, 1.0, 0.0))
    ng = ng_ref[...]

    @pl.when(r == 0)
    def _():
        hist[...] = jnp.zeros_like(hist)

    qkv = []
    for idx, (x_ref, w_ref) in enumerate(((q_ref, cq_ref), (k_ref, ck_ref), (v_ref, cv_ref))):
        x = x_ref[...]
        prev = hist[idx]
        hist[idx] = x[rows - SUBLANE:, :]
        w = w_ref[...]
        kw = GDN_CONV
        acc = w[kw - 1:kw] * x
        for sft in range(1, kw):
            acc = acc + w[kw - 1 - sft:kw - sft] * _shift_rows(x, prev, sft)
        qkv.append(jax.nn.silu(acc))

    def l2n(y):
        return y * lax.rsqrt(jnp.sum(y * y, axis=-1, keepdims=True) + NORM_EPS)

    for i in range(hps):
        cs = slice(i * GDN_DK, (i + 1) * GDN_DK)
        h = hb * hps + i
        a_raw = jnp.sum(jnp.where(ab_lane == h, ab, 0.0), axis=1, keepdims=True)
        b_raw = jnp.sum(jnp.where(ab_lane == heads + h, ab, 0.0), axis=1, keepdims=True)
        o, s = _gdn_head(alog_ref[h], dtb_ref[h], l2n(qkv[0][:, cs]) * (GDN_DK ** -0.5), l2n(qkv[1][:, cs]),
                         qkv[2][:, cs], a_raw, b_raw,
                         gate_ref[:, cs], ng, state[i], masks, rows)
        state[i] = s
        o_ref[:, cs] = o.astype(o_ref.dtype)


def _gdn_core(qkv, conv_w, ab, gate_src, gate_col0, a_log, dt_bias, norm_g, batch, seq, heads):
    rows = _pick(seq, GDN_ROWS, CHUNK)
    nr = seq // rows
    hps = GDN_HEADS_PER_STEP if heads % GDN_HEADS_PER_STEP == 0 else 1
    nhb = heads // hps
    wblk = hps * GDN_DK
    gcb = gate_col0 // wblk
    est = hps * (24 * rows * LANE * 4 + 16 * rows * rows * 4 + 8 * rows * LANE * 4)
    tok = lambda b, h, r: b * nr + r
    return pl.pallas_call(
        functools.partial(_gdn_kernel, rows=rows, hps=hps, heads=heads),
        grid=(batch, nhb, nr),
        in_specs=[pl.BlockSpec(memory_space=pltpu.SMEM),
                  pl.BlockSpec(memory_space=pltpu.SMEM),
                  pl.BlockSpec((rows, wblk), lambda b, h, r: (tok(b, h, r), h)),
                  pl.BlockSpec((rows, wblk), lambda b, h, r: (tok(b, h, r), nhb + h)),
                  pl.BlockSpec((rows, wblk), lambda b, h, r: (tok(b, h, r), 2 * nhb + h)),
                  pl.BlockSpec((SUBLANE, wblk), lambda b, h, r: (0, h)),
                  pl.BlockSpec((SUBLANE, wblk), lambda b, h, r: (0, nhb + h)),
                  pl.BlockSpec((SUBLANE, wblk), lambda b, h, r: (0, 2 * nhb + h)),
                  pl.BlockSpec((rows, LANE), lambda b, h, r: (tok(b, h, r), 0)),
                  pl.BlockSpec((rows, wblk), lambda b, h, r: (tok(b, h, r), gcb + h)),
                  pl.BlockSpec((1, GDN_DV), lambda b, h, r: (0, 0))],
        out_specs=pl.BlockSpec((rows, wblk), lambda b, h, r: (tok(b, h, r), h)),
        out_shape=jax.ShapeDtypeStruct((batch * seq, heads * GDN_DV), BF16),
        scratch_shapes=[pltpu.VMEM((hps, GDN_DK, GDN_DV), F32), pltpu.VMEM((3, SUBLANE, wblk), F32)],
        compiler_params=_cparams(("parallel", "parallel", "arbitrary"), est),
        name="gdn_core",
    )(a_log.astype(F32), dt_bias.astype(F32), qkv, qkv, qkv, conv_w, conv_w, conv_w, ab, gate_src,
      norm_g.reshape(1, GDN_DV).astype(F32))


SB_TQ = 512
SB_TK = 256
SB_HEADS_PER_STEP = 2


def _sb_kernel(q_ref, kt_ref, v_ref, o_ref, acc_s, c_s, *, tq, tk, hps):
    i = pl.program_id(2)
    r = tq // tk
    ur = lax.broadcasted_iota(jnp.int32, (2 * tk, tk), 0) % tk
    uc = lax.broadcasted_iota(jnp.int32, (2 * tk, tk), 1)
    upper2 = jnp.where(ur > uc, 1.0, 0.0).astype(BF16)
    ri = lax.broadcasted_iota(jnp.int32, (tq, tk), 0)
    ci = lax.broadcasted_iota(jnp.int32, (tq, tk), 1)
    acc_s[...] = jnp.zeros_like(acc_s)
    c_s[...] = jnp.zeros_like(c_s)

    def cols(hh):
        return slice(hh * SB_D, (hh + 1) * SB_D)

    def logits(hh, j):
        z = jnp.dot(q_ref[:, cols(hh)], kt_ref[j, cols(hh), :], preferred_element_type=F32)
        return z, jnp.maximum(z, 0.0) + jnp.log(1.0 + jnp.exp2(-jnp.abs(z))) * LOG2E

    def weights(z, lm, c):
        hi = lm.astype(BF16)
        lo = (lm - hi.astype(F32)).astype(BF16)
        rest = jnp.dot(jnp.concatenate([hi, lo], axis=1), upper2, preferred_element_type=F32)
        return jnp.exp2(z - lm - rest - c)

    def values(hh, j):
        return v_ref[pl.ds(pl.multiple_of(j * tk, tk), tk), cols(hh)]

    for d in range(r - 1, -1, -1):
        j = i * r + d
        before = (ci + d * tk) < ri
        for hh in range(hps):
            z, lm = logits(hh, j)
            lm = jnp.where(before, lm, 0.0)
            w = jnp.where(before, weights(z, lm, c_s[hh]), 0.0)
            acc_s[:, cols(hh)] += jnp.dot(w.astype(BF16), values(hh, j), preferred_element_type=F32)
            c_s[hh] += jnp.sum(lm, axis=1, keepdims=True)

    def single(j):
        for hh in range(hps):
            z, lm = logits(hh, j)
            w = weights(z, lm, c_s[hh])
            acc_s[:, cols(hh)] += jnp.dot(w.astype(BF16), values(hh, j), preferred_element_type=F32)
            c_s[hh] += jnp.sum(lm, axis=1, keepdims=True)

    def pair(p, carry):
        ja = i * r - 1 - 2 * p
        jb = ja - 1
        for hh in range(hps):
            za, lma = logits(hh, ja)
            zb, lmb = logits(hh, jb)
            c = c_s[hh]
            ta = jnp.sum(lma, axis=1, keepdims=True)
            wa = weights(za, lma, c)
            wb = weights(zb, lmb, c + ta)
            acc_s[:, cols(hh)] += (jnp.dot(wa.astype(BF16), values(hh, ja), preferred_element_type=F32)
                                   + jnp.dot(wb.astype(BF16), values(hh, jb), preferred_element_type=F32))
            c_s[hh] = c + ta + jnp.sum(lmb, axis=1, keepdims=True)
        return carry

    n_full = i * r
    lax.fori_loop(0, n_full // 2, pair, 0)
    if r % 2 == 1:
        @pl.when(n_full % 2 == 1)
        def _():
            single(0)
    o_ref[...] = acc_s[...].astype(o_ref.dtype)


def _sb_tiles(seq):
    tq = _pick(seq, SB_TQ, LANE)
    return tq, _pick(tq, SB_TK, LANE)


def _sb_attention(qv, kt, batch, seq, heads):
    tq, tk = _sb_tiles(seq)
    nq, nk = seq // tq, seq // tk
    hps = SB_HEADS_PER_STEP if heads % SB_HEADS_PER_STEP == 0 else 1
    nhb = heads // hps
    wblk = hps * SB_D
    est = hps * (4 * tq * SB_D * 2 + 4 * seq * SB_D * 2 + 2 * tq * SB_D * 4 + 24 * tq * tk * 4)
    return pl.pallas_call(
        functools.partial(_sb_kernel, tq=tq, tk=tk, hps=hps),
        grid=(batch, nhb, nq),
        in_specs=[pl.BlockSpec((tq, wblk), lambda b, h, i: (b * nq + i, h)),
                  pl.BlockSpec((nk, wblk, tk), lambda b, h, i: (b, h, 0)),
                  pl.BlockSpec((seq, wblk), lambda b, h, i: (b, nhb + h))],
        out_specs=pl.BlockSpec((tq, wblk), lambda b, h, i: (b * nq + i, h)),
        out_shape=jax.ShapeDtypeStruct((batch * seq, heads * SB_D), BF16),
        scratch_shapes=[pltpu.VMEM((tq, wblk), F32), pltpu.VMEM((hps, tq, 1), F32)],
        compiler_params=_cparams(("parallel", "parallel", "arbitrary"), est),
        name="sb_attention",
    )(qv, kt, qv)


def _pad_cols(w, n):
    return jnp.pad(w, ((0, 0), (0, n - w.shape[1])))


def _ffn(h, ln_g, w_up, layer, conv_w, w_down_bf16, seq):
    xn = _rmsnorm(h, ln_g, BF16)
    conv8 = jnp.pad(conv_w.astype(F32), ((0, SUBLANE - conv_w.shape[0]), (0, 0)))
    act = _ffn_up(xn, w_up, layer, conv8, seq)
    return _matmul([(act, w_down_bf16)], F32, residual=h, b_layer=layer)


def _even_layer(h, ln_g, w_in, w_out, q_norm, kv_norm, w_uq, w_ukv,
                a_re, a_im, log_step, b_re, b_im, c_re, c_im, d_skip, w_glu, batch, seq):
    d = h.shape[1]
    q_rank, kv_rank = q_norm.shape[0], kv_norm.shape[0]
    heads = w_uq.shape[1] // (MLA_NOPE + MLA_ROPE)
    s5_width = d_skip.shape[0]
    o_kr = q_rank + kv_rank
    o_s5 = o_kr + MLA_ROPE
    latent_w = -(-(o_kr + LANE) // (2 * LANE)) * (2 * LANE)
    w_lat = _pad_cols(w_in[:, :o_s5], latent_w)
    w_cat = jnp.concatenate([w_lat, w_in[:, o_s5:]], axis=1).astype(BF16)
    hn = _rmsnorm(h, ln_g, BF16)
    proj = _matmul([(hn, w_cat)], F32)
    cq = _rmsnorm(proj, q_norm, BF16, col0=0, width=q_rank)
    ckv = _rmsnorm(proj, kv_norm, BF16, col0=q_rank, width=kv_rank)
    q_scale = (MLA_NOPE + MLA_ROPE) ** -0.5 * LOG2E
    w_uq_p = jnp.pad(w_uq.reshape(q_rank, heads, MLA_NOPE + MLA_ROPE) * q_scale,
                     ((0, 0), (0, 0), (0, MLA_QK - MLA_NOPE - MLA_ROPE))).reshape(q_rank, heads * MLA_QK)
    q = _matmul([(cq, w_uq_p.astype(BF16))], F32)
    tq = _mla_tile(seq)
    w_kv = w_ukv.reshape(kv_rank, heads, MLA_NOPE + MLA_V)
    w_kn = w_kv[:, :, :MLA_NOPE].reshape(kv_rank, heads * MLA_NOPE).astype(BF16)
    w_v = w_kv[:, :, MLA_NOPE:].reshape(kv_rank, heads * MLA_V).astype(BF16)
    knt = _matmul_keys_t(ckv, w_kn, tq)
    v = _matmul([(ckv, w_v)], BF16)
    cosf, sinf = _rope_tables(seq)
    krt = _rope_k(proj, o_kr, cosf, sinf, seq, tq)
    out_a = _mla_attention(q, knt, krt, v, cosf, sinf, batch, seq, heads)
    params = _s5t_params(a_re, a_im, log_step, b_re, b_im, c_re, c_im, d_skip)
    g = _s5_toeplitz(proj, latent_w, params, batch, seq)
    out_b = _s5_glu(g, w_glu.astype(BF16))
    na = heads * MLA_V
    w_out_all, w_out_layer = w_out
    return _matmul([(out_a, w_out_all), (out_b, w_out_all)], F32, residual=h,
                   b_layer=w_out_layer, b_row0=[0, na])


def _odd_layer(h, ln_g, w_in, w_out, conv_w, a_log, dt_bias, norm_g, batch, seq):
    heads = a_log.shape[0]
    qkv_w = heads * (2 * GDN_DK + GDN_DV)
    o2 = qkv_w + 2 * heads
    o3 = o2 + heads * GDN_DV
    hn = _rmsnorm(h, ln_g, BF16)
    w_t = jnp.swapaxes(w_in, 0, 1)
    w_main = jnp.concatenate([w_t[:qkv_w], w_t[o2:o3]], axis=0).astype(BF16)
    w_ab = jnp.pad(w_t[qkv_w:o2], ((0, LANE - (o2 - qkv_w)), (0, 0))).astype(BF16)
    sb_heads = (w_in.shape[1] - o3) // (3 * SB_D)
    nsq = sb_heads * SB_D
    w_sb_qv = jnp.concatenate([w_t[o3:o3 + nsq] * (SB_D ** -0.5 * LOG2E), w_t[o3 + 2 * nsq:]], axis=0).astype(BF16)
    w_sb_k = w_t[o3 + nsq:o3 + 2 * nsq].astype(BF16)
    main = _matmul([(hn, w_main)], F32, b_nk=True)
    ab = _matmul([(hn, w_ab)], F32, b_nk=True)
    sb_qv = _matmul([(hn, w_sb_qv)], BF16, b_nk=True)
    sb_kt = _matmul_keys_t(hn, w_sb_k, _sb_tiles(seq)[1], b_nk=True)
    conv8 = jnp.pad(conv_w.astype(F32), ((0, SUBLANE - conv_w.shape[0]), (0, 0)))
    out_c = _gdn_core(main, conv8, ab, main, qkv_w, a_log, dt_bias, norm_g, batch, seq, heads)
    out_d = _sb_attention(sb_qv, sb_kt, batch, seq, sb_heads)
    nc = heads * GDN_DV
    w_out_all, w_out_layer = w_out
    return _matmul([(out_c, w_out_all), (out_d, w_out_all)], F32, residual=h,
                   b_layer=w_out_layer, b_row0=[0, nc])


def kernel(x, ln_mix, ln_ffn, ln_final, ffn_w_up, ffn_conv, ffn_w_down, ev_w_in, ev_w_out, mla_q_norm,
           mla_kv_norm, mla_w_uq, mla_w_ukv, s5_a_re, s5_a_im, s5_log_step, s5_b_re, s5_b_im, s5_c_re,
           s5_c_im, s5_d, s5_w_glu, od_w_in, od_w_out, gdn_conv, gdn_a_log, gdn_dt_bias, gdn_norm):
    batch, seq, d = x.shape
    h = x.reshape(batch * seq, d).astype(F32)
    ffn_w_up = ffn_w_up.astype(F32)
    w_down_bf16 = ffn_w_down.astype(BF16)
    for layer in range(ln_mix.shape[0]):
        i = layer // 2
        if layer % 2 == 0:
            h = _even_layer(h, ln_mix[layer], ev_w_in[i], (ev_w_out.astype(F32), i), mla_q_norm[i], mla_kv_norm[i],
                            mla_w_uq[i], mla_w_ukv[i], s5_a_re[i], s5_a_im[i], s5_log_step[i], s5_b_re[i],
                            s5_b_im[i], s5_c_re[i], s5_c_im[i], s5_d[i], s5_w_glu[i], batch, seq)
        else:
            h = _odd_layer(h, ln_mix[layer], od_w_in[i], (od_w_out.astype(F32), i), gdn_conv[i], gdn_a_log[i],
                           gdn_dt_bias[i], gdn_norm[i], batch, seq)
        h = _ffn(h, ln_ffn[layer], ffn_w_up, layer, ffn_conv[layer], w_down_bf16, seq)
    return _rmsnorm(h, ln_final, x.dtype).reshape(batch, seq, d)
```
